```python
import math
import jax, jax.numpy as jnp
from jax import lax
import numpy as np

D_MODEL = 1024
BATCH = 8
SEQ = 4096
DEPTH = 4

N_MEM = 256
HEAD_DIM = 64
MIX_WIDTH = D_MODEL
MEM_HEADS = 4
MEM_WIDTH = MEM_HEADS * HEAD_DIM
MAIN_WIDTH = MIX_WIDTH - MEM_WIDTH
SB_HEADS = MAIN_WIDTH // HEAD_DIM
CONV_WIDTH = 3
D_FF = -(-8 * D_MODEL // (3 * 256)) * 256
N_A_LAYERS = DEPTH // 2
N_B_LAYERS = DEPTH - N_A_LAYERS
BLOCK_Q = 128
EPS = 1e-6

kernel_name = "shortconv_stickbreaking_yoco_hybrid"


def rmsnorm(x, g):
    xf = x.astype(jnp.float32)
    y = xf * lax.rsqrt(jnp.mean(xf * xf, axis=-1, keepdims=True) + EPS)
    return (y * g.astype(jnp.float32)).astype(x.dtype)


def causal_short_conv(u, w):
    c = u.shape[-1]
    return lax.conv_general_dilated(
        u, w[:, None, :].astype(u.dtype), window_strides=(1,),
        padding=[(CONV_WIDTH - 1, 0)],
        dimension_numbers=("NWC", "WIO", "NWC"),
        feature_group_count=c)


def memory_cross_attention(q, mem_k, mem_v):
    scale = 1.0 / math.sqrt(HEAD_DIM)
    s = jnp.einsum("bshd,bmhd->bhsm", q.astype(jnp.float32), mem_k.astype(jnp.float32)) * scale
    p = jax.nn.softmax(s, axis=-1)
    o = jnp.einsum("bhsm,bmhd->bshd", p, mem_v.astype(jnp.float32))
    return o.astype(q.dtype)


def stick_breaking_attention(q, k, v):
    b, s_len, h, d = q.shape
    scale = 1.0 / math.sqrt(d)
    qh = jnp.transpose(q, (0, 2, 1, 3)).astype(jnp.float32)
    kh = jnp.transpose(k, (0, 2, 1, 3)).astype(jnp.float32)
    vh = jnp.transpose(v, (0, 2, 1, 3)).astype(jnp.float32)
    outs = []
    for blk in range(s_len // BLOCK_Q):
        start = blk * BLOCK_Q
        end = start + BLOCK_Q
        qb = qh[:, :, start:end]
        kb = kh[:, :, :end]
        vb = vh[:, :, :end]
        z = jnp.einsum("bhtd,bhsd->bhts", qb, kb) * scale
        t_idx = start + jnp.arange(BLOCK_Q)[:, None]
        s_idx = jnp.arange(end)[None, :]
        causal = s_idx < t_idx
        log_not = jnp.where(causal, jax.nn.log_sigmoid(-z), 0.0)
        tail = lax.cumsum(log_not, axis=3, reverse=True) - log_not
        log_a = jax.nn.log_sigmoid(z) + tail
        a = jnp.where(causal, jnp.exp(log_a), 0.0)
        outs.append(jnp.einsum("bhts,bhsd->bhtd", a, vb))
    o = jnp.concatenate(outs, axis=2)
    return jnp.transpose(o, (0, 2, 1, 3)).astype(q.dtype)


def swiglu(h, w_gate, w_up, w_down):
    return (jax.nn.silu(h @ w_gate) * (h @ w_up)) @ w_down


def _fwd_setup_inputs(seed: int = 0) -> dict:
    key = jax.random.key(seed)
    ks = jax.random.split(key, 16)
    f32 = jnp.float32

    def nrm(k, shape, fan_in):
        return jax.random.normal(k, shape, f32) * (fan_in ** -0.5)

    def gain(k, shape):
        return jnp.ones(shape, f32) + 0.02 * jax.random.normal(k, shape, f32)

    x = jax.random.normal(ks[0], (BATCH, SEQ, D_MODEL), f32)
    mem = jax.random.normal(ks[1], (BATCH, N_MEM, D_MODEL), f32)
    return {
        "x": x,
        "mem": mem,
        "mix_norm": gain(ks[2], (DEPTH, D_MODEL)),
        "a_in": nrm(ks[3], (N_A_LAYERS, D_MODEL, 3 * MAIN_WIDTH + MEM_WIDTH), D_MODEL),
        "conv_w": nrm(ks[4], (N_A_LAYERS, CONV_WIDTH, MAIN_WIDTH), CONV_WIDTH),
        "b_in": nrm(ks[5], (N_B_LAYERS, D_MODEL, MAIN_WIDTH + MEM_WIDTH), D_MODEL),
        "kv_norm": gain(ks[6], (D_MODEL,)),
        "w_kv_shared": nrm(ks[7], (D_MODEL, 2 * MAIN_WIDTH), D_MODEL),
        "w_mem_kv": nrm(ks[8], (DEPTH, D_MODEL, 2 * MEM_WIDTH), D_MODEL),
        "w_o": nrm(ks[9], (DEPTH, MIX_WIDTH, D_MODEL), MIX_WIDTH),
        "ffn_norm": gain(ks[10], (DEPTH, D_MODEL)),
        "w_gate": nrm(ks[11], (DEPTH, D_MODEL, D_FF), D_MODEL),
        "w_up": nrm(ks[12], (DEPTH, D_MODEL, D_FF), D_MODEL),
        "w_down": nrm(ks[13], (DEPTH, D_FF, D_MODEL), D_FF),
        "mem_norm": gain(ks[14], (D_MODEL,)),
        "final_norm": gain(ks[15], (D_MODEL,)),
    }


def _fwd_reference(x, mem, mix_norm, a_in, conv_w, b_in, kv_norm, w_kv_shared, w_mem_kv,
              w_o, ffn_norm, w_gate, w_up, w_down, mem_norm, final_norm):
    b, s_len, _ = x.shape
    m_len = mem.shape[1]
    mem_n = rmsnorm(mem, mem_norm)
    k_sh = None
    v_sh = None
    for i in range(DEPTH):
        h = rmsnorm(x, mix_norm[i])
        mkv = (mem_n @ w_mem_kv[i]).reshape(b, m_len, 2, MEM_HEADS, HEAD_DIM)
        mem_k, mem_v = mkv[:, :, 0], mkv[:, :, 1]
        if i < N_A_LAYERS:
            p = h @ a_in[i]
            b_gate = p[..., :MAIN_WIDTH]
            c_gate = p[..., MAIN_WIDTH:2 * MAIN_WIDTH]
            u = p[..., 2 * MAIN_WIDTH:3 * MAIN_WIDTH]
            q_mem = p[..., 3 * MAIN_WIDTH:]
            y_main = b_gate * causal_short_conv(c_gate * u, conv_w[i])
        else:
            j = i - N_A_LAYERS
            p = h @ b_in[j]
            q_sb = p[..., :MAIN_WIDTH].reshape(b, s_len, SB_HEADS, HEAD_DIM)
            q_mem = p[..., MAIN_WIDTH:]
            y_main = stick_breaking_attention(q_sb, k_sh, v_sh).reshape(b, s_len, MAIN_WIDTH)
        y_mem = memory_cross_attention(
            q_mem.reshape(b, s_len, MEM_HEADS, HEAD_DIM), mem_k, mem_v
        ).reshape(b, s_len, MEM_WIDTH)
        x = x + jnp.concatenate([y_main, y_mem], axis=-1) @ w_o[i]
        x = x + swiglu(rmsnorm(x, ffn_norm[i]), w_gate[i], w_up[i], w_down[i])
        if i == N_A_LAYERS - 1:
            kv = (rmsnorm(x, kv_norm) @ w_kv_shared).reshape(b, s_len, 2, SB_HEADS, HEAD_DIM)
            k_sh, v_sh = kv[:, :, 0], kv[:, :, 1]
    return rmsnorm(x, final_norm)


import jax as _jax
import jax.numpy as _jnp

TWIN_FORMAT = 'train_step'
FWD_PARAMS = ['x', 'mem', 'mix_norm', 'a_in', 'conv_w', 'b_in', 'kv_norm', 'w_kv_shared', 'w_mem_kv', 'w_o', 'ffn_norm', 'w_gate', 'w_up', 'w_down', 'mem_norm', 'final_norm']
TWIN_WEIGHTS = ['mix_norm', 'a_in', 'conv_w', 'b_in', 'kv_norm', 'w_kv_shared', 'w_mem_kv', 'w_o', 'ffn_norm', 'w_gate', 'w_up', 'w_down', 'mem_norm', 'final_norm']
TWIN_DIFF_INPUT = 'x'
TWIN_INPUTS = ['x', 'mem', 'mix_norm', 'a_in', 'conv_w', 'b_in', 'kv_norm', 'w_kv_shared', 'w_mem_kv', 'w_o', 'ffn_norm', 'w_gate', 'w_up', 'w_down', 'mem_norm', 'final_norm', 'loss_target', 'm_mix_norm', 'm_a_in', 'm_conv_w', 'm_b_in', 'm_kv_norm', 'm_w_kv_shared', 'm_w_mem_kv', 'm_w_o', 'm_ffn_norm', 'm_w_gate', 'm_w_up', 'm_w_down', 'm_mem_norm', 'm_final_norm', 'v_mix_norm', 'v_a_in', 'v_conv_w', 'v_b_in', 'v_kv_norm', 'v_w_kv_shared', 'v_w_mem_kv', 'v_w_o', 'v_ffn_norm', 'v_w_gate', 'v_w_up', 'v_w_down', 'v_mem_norm', 'v_final_norm']
TWIN_OUTPUTS = ['loss', 'grad_x', 'grad_mix_norm', 'grad_a_in', 'grad_conv_w', 'grad_b_in', 'grad_kv_norm', 'grad_w_kv_shared', 'grad_w_mem_kv', 'grad_w_o', 'grad_ffn_norm', 'grad_w_gate', 'grad_w_up', 'grad_w_down', 'grad_mem_norm', 'grad_final_norm', 'delta_mix_norm', 'delta_a_in', 'delta_conv_w', 'delta_b_in', 'delta_kv_norm', 'delta_w_kv_shared', 'delta_w_mem_kv', 'delta_w_o', 'delta_ffn_norm', 'delta_w_gate', 'delta_w_up', 'delta_w_down', 'delta_mem_norm', 'delta_final_norm', 'new_m_mix_norm', 'new_m_a_in', 'new_m_conv_w', 'new_m_b_in', 'new_m_kv_norm', 'new_m_w_kv_shared', 'new_m_w_mem_kv', 'new_m_w_o', 'new_m_ffn_norm', 'new_m_w_gate', 'new_m_w_up', 'new_m_w_down', 'new_m_mem_norm', 'new_m_final_norm', 'new_v_mix_norm', 'new_v_a_in', 'new_v_conv_w', 'new_v_b_in', 'new_v_kv_norm', 'new_v_w_kv_shared', 'new_v_w_mem_kv', 'new_v_w_o', 'new_v_ffn_norm', 'new_v_w_gate', 'new_v_w_up', 'new_v_w_down', 'new_v_mem_norm', 'new_v_final_norm']
TWIN_LEAF_KINDS = {'loss': 'loss', 'grad_x': 'grad_x', 'grad_mix_norm': 'grad_w', 'grad_a_in': 'grad_w', 'grad_conv_w': 'grad_w', 'grad_b_in': 'grad_w', 'grad_kv_norm': 'grad_w', 'grad_w_kv_shared': 'grad_w', 'grad_w_mem_kv': 'grad_w', 'grad_w_o': 'grad_w', 'grad_ffn_norm': 'grad_w', 'grad_w_gate': 'grad_w', 'grad_w_up': 'grad_w', 'grad_w_down': 'grad_w', 'grad_mem_norm': 'grad_w', 'grad_final_norm': 'grad_w', 'delta_mix_norm': 'delta_w', 'delta_a_in': 'delta_w', 'delta_conv_w': 'delta_w', 'delta_b_in': 'delta_w', 'delta_kv_norm': 'delta_w', 'delta_w_kv_shared': 'delta_w', 'delta_w_mem_kv': 'delta_w', 'delta_w_o': 'delta_w', 'delta_ffn_norm': 'delta_w', 'delta_w_gate': 'delta_w', 'delta_w_up': 'delta_w', 'delta_w_down': 'delta_w', 'delta_mem_norm': 'delta_w', 'delta_final_norm': 'delta_w', 'new_m_mix_norm': 'new_m', 'new_m_a_in': 'new_m', 'new_m_conv_w': 'new_m', 'new_m_b_in': 'new_m', 'new_m_kv_norm': 'new_m', 'new_m_w_kv_shared': 'new_m', 'new_m_w_mem_kv': 'new_m', 'new_m_w_o': 'new_m', 'new_m_ffn_norm': 'new_m', 'new_m_w_gate': 'new_m', 'new_m_w_up': 'new_m', 'new_m_w_down': 'new_m', 'new_m_mem_norm': 'new_m', 'new_m_final_norm': 'new_m', 'new_v_mix_norm': 'new_v', 'new_v_a_in': 'new_v', 'new_v_conv_w': 'new_v', 'new_v_b_in': 'new_v', 'new_v_kv_norm': 'new_v', 'new_v_w_kv_shared': 'new_v', 'new_v_w_mem_kv': 'new_v', 'new_v_w_o': 'new_v', 'new_v_ffn_norm': 'new_v', 'new_v_w_gate': 'new_v', 'new_v_w_up': 'new_v', 'new_v_w_down': 'new_v', 'new_v_mem_norm': 'new_v', 'new_v_final_norm': 'new_v'}


def _forward(args):
    return _fwd_reference(*[args[k] for k in FWD_PARAMS])


def _output_shape():
    def fwd():
        inp = _fwd_setup_inputs(0)
        return _fwd_reference(*[inp[k] for k in FWD_PARAMS])
    out = _jax.eval_shape(fwd)
    return out.shape, out.dtype

N_MICROBATCH = 1
ADAM_LR = 0.001
ADAM_B1 = 0.9
ADAM_B2 = 0.999
ADAM_EPS = 1e-08
ADAM_WD = 0.01
ADAM_STEP = 10
PER_EXAMPLE_BATCH_AXIS = {'x': 0, 'mem': 0, 'loss_target': 0}
SHARED_INPUTS = []
_WEIGHT_DTYPES = {'mix_norm': _jnp.float32, 'a_in': _jnp.float32, 'conv_w': _jnp.float32, 'b_in': _jnp.float32, 'kv_norm': _jnp.float32, 'w_kv_shared': _jnp.float32, 'w_mem_kv': _jnp.float32, 'w_o': _jnp.float32, 'ffn_norm': _jnp.float32, 'w_gate': _jnp.float32, 'w_up': _jnp.float32, 'w_down': _jnp.float32, 'mem_norm': _jnp.float32, 'final_norm': _jnp.float32}
MOMENT_SCALE = {'mix_norm': 1.868961e-01, 'a_in': 1.660632e-01, 'conv_w': 1.778560e-01, 'b_in': 2.517671e-02, 'kv_norm': 8.561778e-02, 'w_kv_shared': 6.748052e-02, 'w_mem_kv': 1.519302e-02, 'w_o': 1.133347e-01, 'ffn_norm': 1.100314e-01, 'w_gate': 4.564622e-02, 'w_up': 4.421302e-02, 'w_down': 7.326268e-02, 'mem_norm': 2.280073e-02, 'final_norm': 3.199327e+01}


def _to_microbatches(a, axis):
    t = _jnp.moveaxis(a, axis, 0)
    t = t.reshape((N_MICROBATCH, t.shape[0] // N_MICROBATCH) + t.shape[1:])
    return _jnp.moveaxis(t, 1, axis + 1)


def setup_inputs(seed: int = 0) -> dict:
    inp = _fwd_setup_inputs(seed)
    key = _jax.random.fold_in(_jax.random.key(seed), 7919)
    shape, _ = _output_shape()
    out = dict(inp)
    out["loss_target"] = _jax.random.normal(_jax.random.fold_in(key, 0), shape, _jnp.float32)
    for i, name in enumerate(TWIN_WEIGHTS):
        w = inp[name].astype(_jnp.float32)
        if MOMENT_SCALE is None:
            s = _jnp.sqrt(_jnp.mean(_jnp.square(w)) + 1e-30)
        else:
            s = MOMENT_SCALE[name]
        km, kv = _jax.random.split(_jax.random.fold_in(key, i + 1))
        out[name] = w
        out["m_" + name] = s * _jax.random.normal(km, w.shape, _jnp.float32)
        out["v_" + name] = (s * s) * _jax.random.uniform(kv, w.shape, _jnp.float32, 0.5, 1.5)
    if N_MICROBATCH > 1:
        for name, axis in PER_EXAMPLE_BATCH_AXIS.items():
            out[name] = _to_microbatches(out[name], axis)
    return {'x': out['x'], 'mem': out['mem'], 'mix_norm': out['mix_norm'], 'a_in': out['a_in'], 'conv_w': out['conv_w'], 'b_in': out['b_in'], 'kv_norm': out['kv_norm'], 'w_kv_shared': out['w_kv_shared'], 'w_mem_kv': out['w_mem_kv'], 'w_o': out['w_o'], 'ffn_norm': out['ffn_norm'], 'w_gate': out['w_gate'], 'w_up': out['w_up'], 'w_down': out['w_down'], 'mem_norm': out['mem_norm'], 'final_norm': out['final_norm'], 'loss_target': out['loss_target'], 'm_mix_norm': out['m_mix_norm'], 'm_a_in': out['m_a_in'], 'm_conv_w': out['m_conv_w'], 'm_b_in': out['m_b_in'], 'm_kv_norm': out['m_kv_norm'], 'm_w_kv_shared': out['m_w_kv_shared'], 'm_w_mem_kv': out['m_w_mem_kv'], 'm_w_o': out['m_w_o'], 'm_ffn_norm': out['m_ffn_norm'], 'm_w_gate': out['m_w_gate'], 'm_w_up': out['m_w_up'], 'm_w_down': out['m_w_down'], 'm_mem_norm': out['m_mem_norm'], 'm_final_norm': out['m_final_norm'], 'v_mix_norm': out['v_mix_norm'], 'v_a_in': out['v_a_in'], 'v_conv_w': out['v_conv_w'], 'v_b_in': out['v_b_in'], 'v_kv_norm': out['v_kv_norm'], 'v_w_kv_shared': out['v_w_kv_shared'], 'v_w_mem_kv': out['v_w_mem_kv'], 'v_w_o': out['v_w_o'], 'v_ffn_norm': out['v_ffn_norm'], 'v_w_gate': out['v_w_gate'], 'v_w_up': out['v_w_up'], 'v_w_down': out['v_w_down'], 'v_mem_norm': out['v_mem_norm'], 'v_final_norm': out['v_final_norm']}


def _loss(weights, diff, rest, loss_target):
    with _jax.named_scope("forward"):
        args = {**rest, TWIN_DIFF_INPUT: diff, **{k: w.astype(_WEIGHT_DTYPES[k]) for k, w in weights.items()}}
        y = _forward(args)
    with _jax.named_scope("loss_head"):
        err = _jnp.square(y.astype(_jnp.float32) - loss_target)
        return 0.5 * _jnp.sum(_jnp.mean(err, axis=-1)) if err.ndim else 0.5 * err


def _adamw(w, g, m, v):
    m = ADAM_B1 * m + (1.0 - ADAM_B1) * g
    v = ADAM_B2 * v + (1.0 - ADAM_B2) * _jnp.square(g)
    m_hat = m / (1.0 - ADAM_B1 ** ADAM_STEP)
    v_hat = v / (1.0 - ADAM_B2 ** ADAM_STEP)
    delta = -ADAM_LR * (m_hat / (_jnp.sqrt(v_hat) + ADAM_EPS) + ADAM_WD * w)
    return delta, m, v


def reference(x, mem, mix_norm, a_in, conv_w, b_in, kv_norm, w_kv_shared, w_mem_kv, w_o, ffn_norm, w_gate, w_up, w_down, mem_norm, final_norm, loss_target, m_mix_norm, m_a_in, m_conv_w, m_b_in, m_kv_norm, m_w_kv_shared, m_w_mem_kv, m_w_o, m_ffn_norm, m_w_gate, m_w_up, m_w_down, m_mem_norm, m_final_norm, v_mix_norm, v_a_in, v_conv_w, v_b_in, v_kv_norm, v_w_kv_shared, v_w_mem_kv, v_w_o, v_ffn_norm, v_w_gate, v_w_up, v_w_down, v_mem_norm, v_final_norm):
    given = dict(x=x, mem=mem, mix_norm=mix_norm, a_in=a_in, conv_w=conv_w, b_in=b_in, kv_norm=kv_norm, w_kv_shared=w_kv_shared, w_mem_kv=w_mem_kv, w_o=w_o, ffn_norm=ffn_norm, w_gate=w_gate, w_up=w_up, w_down=w_down, mem_norm=mem_norm, final_norm=final_norm, loss_target=loss_target, m_mix_norm=m_mix_norm, m_a_in=m_a_in, m_conv_w=m_conv_w, m_b_in=m_b_in, m_kv_norm=m_kv_norm, m_w_kv_shared=m_w_kv_shared, m_w_mem_kv=m_w_mem_kv, m_w_o=m_w_o, m_ffn_norm=m_ffn_norm, m_w_gate=m_w_gate, m_w_up=m_w_up, m_w_down=m_w_down, m_mem_norm=m_mem_norm, m_final_norm=m_final_norm, v_mix_norm=v_mix_norm, v_a_in=v_a_in, v_conv_w=v_conv_w, v_b_in=v_b_in, v_kv_norm=v_kv_norm, v_w_kv_shared=v_w_kv_shared, v_w_mem_kv=v_w_mem_kv, v_w_o=v_w_o, v_ffn_norm=v_ffn_norm, v_w_gate=v_w_gate, v_w_up=v_w_up, v_w_down=v_w_down, v_mem_norm=v_mem_norm, v_final_norm=v_final_norm)
    weights = {n: given[n] for n in TWIN_WEIGHTS}
    shared = {n: given[n] for n in SHARED_INPUTS}
    per_example = {n: given[n] for n in ['x', 'mem']}
    grad_fn = _jax.value_and_grad(_loss, argnums=(0, 1))

    def one_microbatch(ex, loss_target):
        ex = dict(ex)
        diff = ex.pop(TWIN_DIFF_INPUT)
        return grad_fn(weights, diff, {**shared, **ex}, loss_target)

    if N_MICROBATCH == 1:
        loss, (grad_w, grad_x) = one_microbatch(per_example, given["loss_target"])
    else:
        def body(carry, xs):
            loss_sum, grad_sum = carry
            l_k, (gw_k, gx_k) = one_microbatch(xs[0], xs[1])
            with _jax.named_scope("update"):
                return (loss_sum + l_k, _jax.tree.map(_jnp.add, grad_sum, gw_k)), gx_k

        init = (_jnp.zeros((), _jnp.float32), _jax.tree.map(_jnp.zeros_like, weights))
        (loss, grad_w), grad_x = _jax.lax.scan(body, init, (per_example, given["loss_target"]))
    with _jax.named_scope("update"):
        delta_w, new_m, new_v = {}, {}, {}
        for n in TWIN_WEIGHTS:
            delta_w[n], new_m[n], new_v[n] = _adamw(weights[n], grad_w[n], given["m_" + n], given["v_" + n])
    return (loss, grad_x, *[grad_w[n] for n in TWIN_WEIGHTS], *[delta_w[n] for n in TWIN_WEIGHTS],
            *[new_m[n] for n in TWIN_WEIGHTS], *[new_v[n] for n in TWIN_WEIGHTS])
```

```python
import math
from typing import NamedTuple, Optional

import jax
import jax.numpy as jnp
from jax import lax
from jax.experimental import pallas as pl
from jax.experimental.pallas import tpu as pltpu

F32 = jnp.float32
BF16 = jnp.bfloat16
MESH = pl.DeviceIdType.MESH

N_CHIPS = 4
N_DEVICES = 8
HEAD_DIM = 64
CONV_TAPS = 3
NORM_EPS = 1e-6
V7X_VMEM_BYTES = 64 * 1024 * 1024
VMEM_LIMIT = V7X_VMEM_BYTES - 8 * 1024 * 1024
LANES = 128
SMALL_ROWS = 16

ADAM_LR = 0.001
ADAM_B1 = 0.9
ADAM_B2 = 0.999
ADAM_EPS = 1e-08
ADAM_WD = 0.01
ADAM_STEP = 10


def _params(**kw):
    return pltpu.CompilerParams(vmem_limit_bytes=VMEM_LIMIT, **kw)


def _blk(n, pref):
    b = min(n, pref)
    assert n % b == 0, (n, pref)
    return b


class Op(NamedTuple):
    arr: jax.Array
    chunk: Optional[str] = None
    layer: Optional[int] = None


def _op_spec(op_chunk, op_layer, shape2, br, bc, pick):
    r, c = shape2
    lead = () if op_layer is None else (op_layer,)
    none = (None,) * len(lead)
    if op_chunk is None:
        def imap(i, j, k):
            rb, cb = pick(i, j, k)
            return (*lead, rb, cb)
        return pl.BlockSpec((*none, br, bc), imap)
    if op_chunk == 'r':
        n = r // br
        assert r % br == 0

        def imap(i, j, k):
            rb, cb = pick(i, j, k)
            return (rb // n, *lead, rb % n, cb)
        return pl.BlockSpec((None, *none, br, bc), imap)
    n = c // bc
    assert c % bc == 0

    def imap(i, j, k):
        rb, cb = pick(i, j, k)
        return (cb // n, *lead, rb, cb % n)
    return pl.BlockSpec((None, *none, br, bc), imap)


def _mm(a, b, *, name, ta=False, tb=False, bm, bn, bk, out_dtype, out_chunk=None, res=None):
    def dims(op):
        r, c = op.arr.shape[-2:]
        return (r * N_CHIPS if op.chunk == 'r' else r, c * N_CHIPS if op.chunk == 'c' else c)

    ar, ac = dims(a)
    br_, bc_ = dims(b)
    m, ka = (ac, ar) if ta else (ar, ac)
    kb, n = (bc_, br_) if tb else (br_, bc_)
    assert ka == kb, (name, ka, kb)
    assert m % bm == 0 and n % bn == 0 and ka % bk == 0, (name, m, n, ka, bm, bn, bk)
    nk = ka // bk

    if ta:
        a_spec = _op_spec(a.chunk, a.layer, a.arr.shape[-2:], bk, bm, lambda i, j, k: (k, i))
    else:
        a_spec = _op_spec(a.chunk, a.layer, a.arr.shape[-2:], bm, bk, lambda i, j, k: (i, k))
    if tb:
        b_spec = _op_spec(b.chunk, b.layer, b.arr.shape[-2:], bn, bk, lambda i, j, k: (j, k))
    else:
        b_spec = _op_spec(b.chunk, b.layer, b.arr.shape[-2:], bk, bn, lambda i, j, k: (k, j))

    if out_chunk == 'r':
        out_shape2 = (m // N_CHIPS, n)
    elif out_chunk == 'c':
        out_shape2 = (m, n // N_CHIPS)
    else:
        out_shape2 = (m, n)
    o_spec = _op_spec(out_chunk, None, out_shape2, bm, bn, lambda i, j, k: (i, j))
    out_full = out_shape2 if out_chunk is None else (N_CHIPS, *out_shape2)

    contract = (((0 if ta else 1,), (1 if tb else 0,)), ((), ()))
    has_res = res is not None

    def body(*refs):
        if has_res:
            a_ref, b_ref, r_ref, o_ref, acc_ref = refs
        else:
            a_ref, b_ref, o_ref, acc_ref = refs
        k = pl.program_id(2)
        prod = lax.dot_general(a_ref[...].astype(BF16), b_ref[...].astype(BF16), contract,
                               preferred_element_type=F32)

        @pl.when(k == 0)
        def _():
            acc_ref[...] = prod

        @pl.when(k > 0)
        def _():
            acc_ref[...] += prod

        @pl.when(k == nk - 1)
        def _():
            acc = acc_ref[...]
            if has_res:
                acc = acc + r_ref[...].astype(F32)
            o_ref[...] = acc.astype(o_ref.dtype)

    in_specs = [a_spec, b_spec]
    operands = [a.arr, b.arr]
    if has_res:
        in_specs.append(_op_spec(res.chunk, res.layer, res.arr.shape[-2:], bm, bn, lambda i, j, k: (i, j)))
        operands.append(res.arr)
    return pl.pallas_call(
        body, name=name, grid=(m // bm, n // bn, nk),
        in_specs=in_specs, out_specs=o_spec,
        out_shape=jax.ShapeDtypeStruct(out_full, out_dtype),
        scratch_shapes=[pltpu.VMEM((bm, bn), F32)],
        compiler_params=_params(dimension_semantics=("parallel", "parallel", "arbitrary")),
    )(*operands)


def _rms_fwd(x, g, *, name):
    r, d = x.shape
    bm = _blk(r, 512)

    def body(x_ref, g_ref, o_ref):
        xv = x_ref[...]
        rstd = lax.rsqrt(jnp.mean(xv * xv, axis=-1, keepdims=True) + NORM_EPS)
        o_ref[...] = ((xv * rstd) * g_ref[...]).astype(o_ref.dtype)

    return pl.pallas_call(
        body, name=name, grid=(r // bm,),
        in_specs=[pl.BlockSpec((bm, d), lambda i: (i, 0)), pl.BlockSpec((1, d), lambda i: (0, 0))],
        out_specs=pl.BlockSpec((bm, d), lambda i: (i, 0)),
        out_shape=jax.ShapeDtypeStruct((r, d), BF16),
        compiler_params=_params(dimension_semantics=("parallel",)),
    )(x, g)


def _rms_bwd(x, g, dh, dres, *, name):
    r, d = x.shape
    bm = _blk(r, 512)
    has_res = dres is not None

    def body(*refs):
        if has_res:
            x_ref, g_ref, dh_ref, dres_ref, dx_ref, dg_ref = refs
        else:
            x_ref, g_ref, dh_ref, dx_ref, dg_ref = refs

        @pl.when(pl.program_id(0) == 0)
        def _():
            dg_ref[...] = jnp.zeros_like(dg_ref)

        xv = x_ref[...]
        rstd = lax.rsqrt(jnp.mean(xv * xv, axis=-1, keepdims=True) + NORM_EPS)
        xh = xv * rstd
        dhv = dh_ref[...].astype(F32)
        dg_ref[...] += jnp.sum(dhv * xh, axis=0, keepdims=True)
        dxh = dhv * g_ref[...]
        dx = rstd * (dxh - xh * jnp.mean(dxh * xh, axis=-1, keepdims=True))
        if has_res:
            dx = dres_ref[...] + dx
        dx_ref[...] = dx

    row = pl.BlockSpec((bm, d), lambda i: (i, 0))
    vec = pl.BlockSpec((1, d), lambda i: (0, 0))
    in_specs = [row, vec, row] + ([row] if has_res else [])
    operands = [x, g, dh] + ([dres] if has_res else [])
    return pl.pallas_call(
        body, name=name, grid=(r // bm,),
        in_specs=in_specs, out_specs=[row, vec],
        out_shape=[jax.ShapeDtypeStruct((r, d), F32), jax.ShapeDtypeStruct((1, d), F32)],
        compiler_params=_params(dimension_semantics=("arbitrary",)),
    )(*operands)


def _final_loss(x, g, tgt, *, name):
    r, d = x.shape
    bm = _blk(r, 512)

    def body(x_ref, g_ref, t_ref, dx_ref, dg_ref, loss_ref):
        @pl.when(pl.program_id(0) == 0)
        def _():
            dg_ref[...] = jnp.zeros_like(dg_ref)
            loss_ref[...] = jnp.zeros_like(loss_ref)

        xv = x_ref[...]
        gv = g_ref[...]
        rstd = lax.rsqrt(jnp.mean(xv * xv, axis=-1, keepdims=True) + NORM_EPS)
        xh = xv * rstd
        diff = xh * gv - t_ref[...]
        loss_ref[...] += jnp.sum(diff * diff) * (0.5 / d)
        dy = diff * (1.0 / d)
        dg_ref[...] += jnp.sum(dy * xh, axis=0, keepdims=True)
        dxh = dy * gv
        dx_ref[...] = rstd * (dxh - xh * jnp.mean(dxh * xh, axis=-1, keepdims=True))

    row = pl.BlockSpec((bm, d), lambda i: (i, 0))
    vec = pl.BlockSpec((1, d), lambda i: (0, 0))
    return pl.pallas_call(
        body, name=name, grid=(r // bm,),
        in_specs=[row, vec, row],
        out_specs=[row, vec, pl.BlockSpec((1, LANES), lambda i: (0, 0))],
        out_shape=[jax.ShapeDtypeStruct((r, d), F32), jax.ShapeDtypeStruct((1, d), F32),
                   jax.ShapeDtypeStruct((1, LANES), F32)],
        compiler_params=_params(dimension_semantics=("arbitrary",)),
    )(x, g, tgt)


def _shift_down(v, k, row):
    return jnp.where(row >= k, pltpu.roll(v, k, 0), 0.0)


def _shift_up(v, k, row, s):
    return jnp.where(row < s - k, pltpu.roll(v, s - k, 0), 0.0)


def _conv_fwd(p, w, *, name):
    s = p.shape[0]
    width = w.shape[1]
    nb = width // LANES

    def body(b_ref, c_ref, u_ref, w_ref, y_ref):
        cu = c_ref[...].astype(F32) * u_ref[...].astype(F32)
        row = lax.broadcasted_iota(jnp.int32, cu.shape, 0)
        wv = w_ref[...]
        conv = wv[2:3] * cu + wv[1:2] * _shift_down(cu, 1, row) + wv[0:1] * _shift_down(cu, 2, row)
        y_ref[...] = (b_ref[...].astype(F32) * conv).astype(y_ref.dtype)

    def col(o):
        return pl.BlockSpec((s, LANES), lambda j: (0, j + o * nb))

    return pl.pallas_call(
        body, name=name, grid=(nb,),
        in_specs=[col(0), col(1), col(2), pl.BlockSpec((CONV_TAPS, LANES), lambda j: (0, j))],
        out_specs=col(0),
        out_shape=jax.ShapeDtypeStruct((s, width), BF16),
        compiler_params=_params(dimension_semantics=("parallel",)),
    )(p, p, p, w)


def _conv_bwd(p, w, dy, *, name):
    s = p.shape[0]
    width = w.shape[1]
    nb = width // LANES

    def body(b_ref, c_ref, u_ref, w_ref, dy_ref, db_ref, dc_ref, du_ref, dw_ref):
        bv = b_ref[...].astype(F32)
        cv = c_ref[...].astype(F32)
        uv = u_ref[...].astype(F32)
        dyv = dy_ref[...].astype(F32)
        cu = cv * uv
        row = lax.broadcasted_iota(jnp.int32, cu.shape, 0)
        wv = w_ref[...]
        cu1 = _shift_down(cu, 1, row)
        cu2 = _shift_down(cu, 2, row)
        conv = wv[2:3] * cu + wv[1:2] * cu1 + wv[0:1] * cu2
        db_ref[...] = (dyv * conv).astype(db_ref.dtype)
        dconv = dyv * bv
        dcu = wv[2:3] * dconv + wv[1:2] * _shift_up(dconv, 1, row, s) + wv[0:1] * _shift_up(dconv, 2, row, s)
        dc_ref[...] = (dcu * uv).astype(dc_ref.dtype)
        du_ref[...] = (dcu * cv).astype(du_ref.dtype)
        dw_ref[0:1, :] = jnp.sum(dconv * cu2, axis=0, keepdims=True)
        dw_ref[1:2, :] = jnp.sum(dconv * cu1, axis=0, keepdims=True)
        dw_ref[2:3, :] = jnp.sum(dconv * cu, axis=0, keepdims=True)

    def col(o):
        return pl.BlockSpec((s, LANES), lambda j: (0, j + o * nb))

    wspec = pl.BlockSpec((CONV_TAPS, LANES), lambda j: (0, j))
    act = jax.ShapeDtypeStruct((s, width), BF16)
    return pl.pallas_call(
        body, name=name, grid=(nb,),
        in_specs=[col(0), col(1), col(2), wspec, col(0)],
        out_specs=[col(0), col(0), col(0), wspec],
        out_shape=[act, act, act, jax.ShapeDtypeStruct((CONV_TAPS, width), F32)],
        compiler_params=_params(dimension_semantics=("parallel",)),
    )(p, p, p, w, dy)


def _swiglu_fwd(g, u, *, name):
    nc, s, f = g.shape
    bm = _blk(s, 512)

    def body(g_ref, u_ref, o_ref):
        gv = g_ref[...].astype(F32)
        o_ref[...] = (gv * jax.nn.sigmoid(gv) * u_ref[...].astype(F32)).astype(o_ref.dtype)

    spec = pl.BlockSpec((None, bm, f), lambda c, i: (c, i, 0))
    return pl.pallas_call(
        body, name=name, grid=(nc, s // bm), in_specs=[spec, spec], out_specs=spec,
        out_shape=jax.ShapeDtypeStruct(g.shape, BF16),
        compiler_params=_params(dimension_semantics=("parallel", "parallel")),
    )(g, u)


def _swiglu_bwd(dact, g, u, *, name):
    nc, s, f = g.shape
    bm = _blk(s, 512)

    def body(d_ref, g_ref, u_ref, dg_ref, du_ref):
        gv = g_ref[...].astype(F32)
        dv = d_ref[...].astype(F32)
        sg = jax.nn.sigmoid(gv)
        silu = gv * sg
        dg_ref[...] = (dv * u_ref[...].astype(F32) * (sg * (1.0 + gv * (1.0 - sg)))).astype(dg_ref.dtype)
        du_ref[...] = (dv * silu).astype(du_ref.dtype)

    spec = pl.BlockSpec((None, bm, f), lambda c, i: (c, i, 0))
    out = jax.ShapeDtypeStruct(g.shape, BF16)
    return pl.pallas_call(
        body, name=name, grid=(nc, s // bm), in_specs=[spec, spec, spec], out_specs=[spec, spec],
        out_shape=[out, out],
        compiler_params=_params(dimension_semantics=("parallel", "parallel")),
    )(dact, g, u)


_NT = (((1,), (1,)), ((), ()))
_TN = (((0,), (0,)), ((), ()))


def _mem_probs(q, k, scale):
    s = lax.dot_general(q, k, _NT, preferred_element_type=F32) * scale
    e = jnp.exp(s - jnp.max(s, axis=-1, keepdims=True))
    return e / jnp.sum(e, axis=-1, keepdims=True)


def _mem_fwd(q, k, v, *, name):
    h, s, d = q.shape
    m = k.shape[1]
    bq = _blk(s, 1024)
    scale = 1.0 / math.sqrt(d)

    def body(q_ref, k_ref, v_ref, o_ref):
        p = _mem_probs(q_ref[...], k_ref[...], scale)
        o_ref[...] = jnp.dot(p.astype(BF16), v_ref[...], preferred_element_type=F32).astype(o_ref.dtype)

    qs = pl.BlockSpec((None, bq, d), lambda hh, i: (hh, i, 0))
    ks = pl.BlockSpec((None, m, d), lambda hh, i: (hh, 0, 0))
    return pl.pallas_call(
        body, name=name, grid=(h, s // bq), in_specs=[qs, ks, ks], out_specs=qs,
        out_shape=jax.ShapeDtypeStruct(q.shape, BF16),
        compiler_params=_params(dimension_semantics=("parallel", "parallel")),
    )(q, k, v)


def _mem_bwd(q, k, v, do, *, name):
    h, s, d = q.shape
    m = k.shape[1]
    bq = _blk(s, 1024)
    scale = 1.0 / math.sqrt(d)

    def body(q_ref, k_ref, v_ref, do_ref, dq_ref, dk_ref, dv_ref):
        @pl.when(pl.program_id(1) == 0)
        def _():
            dk_ref[...] = jnp.zeros_like(dk_ref)
            dv_ref[...] = jnp.zeros_like(dv_ref)

        qv = q_ref[...]
        kv = k_ref[...]
        dov = do_ref[...]
        p = _mem_probs(qv, kv, scale)
        pb = p.astype(BF16)
        dp = lax.dot_general(dov, v_ref[...], _NT, preferred_element_type=F32)
        pf = pb.astype(F32)
        ds = (pf * (dp - jnp.sum(pf * dp, axis=-1, keepdims=True)) * scale).astype(BF16)
        dq_ref[...] = jnp.dot(ds, kv, preferred_element_type=F32).astype(dq_ref.dtype)
        dk_ref[...] += lax.dot_general(ds, qv, _TN, preferred_element_type=F32)
        dv_ref[...] += lax.dot_general(pb, dov, _TN, preferred_element_type=F32)

    qs = pl.BlockSpec((None, bq, d), lambda hh, i: (hh, i, 0))
    ks = pl.BlockSpec((None, m, d), lambda hh, i: (hh, 0, 0))
    kvout = jax.ShapeDtypeStruct(k.shape, F32)
    return pl.pallas_call(
        body, name=name, grid=(h, s // bq), in_specs=[qs, ks, ks, qs], out_specs=[qs, ks, ks],
        out_shape=[jax.ShapeDtypeStruct(q.shape, BF16), kvout, kvout],
        compiler_params=_params(dimension_semantics=("parallel", "arbitrary")),
    )(q, k, v, do)


def _split_dot(val, tri):
    hi = val.astype(BF16)
    lo = (val - hi.astype(F32)).astype(BF16)
    return (jnp.dot(hi, tri, preferred_element_type=F32) + jnp.dot(lo, tri, preferred_element_type=F32))


def _sb_tile_logs(qv, kblk, scale, causal):
    z = lax.dot_general(qv, kblk, _NT, preferred_element_type=F32) * scale
    e = jnp.exp(-jnp.abs(z))
    lg = jnp.log(1.0 + e)
    log_b = jnp.minimum(z, 0.0) - lg
    log_n = jnp.minimum(-z, 0.0) - lg
    if causal is not None:
        log_n = jnp.where(causal, log_n, 0.0)
    return z, e, log_b, log_n


def _sb_fwd(q, k, v, *, name):
    h, s, d = q.shape
    t = _blk(s, 256)
    scale = 1.0 / math.sqrt(d)

    def body(q_ref, k_ref, v_ref, o_ref, o32_ref):
        qi = pl.program_id(1)
        qv = q_ref[...]
        row = lax.broadcasted_iota(jnp.int32, (t, t), 0)
        col = lax.broadcasted_iota(jnp.int32, (t, t), 1)
        tri = (row > col).astype(BF16)
        diag = col < row

        def tile(kb, rsum, acc, causal):
            off = pl.multiple_of(kb * t, t)
            kblk = k_ref[pl.ds(off, t), :]
            vblk = v_ref[pl.ds(off, t), :]
            _, _, log_b, log_n = _sb_tile_logs(qv, kblk, scale, causal)
            a = jnp.exp(log_b + _split_dot(log_n, tri) + rsum)
            if causal is not None:
                a = jnp.where(causal, a, 0.0)
            acc = acc + jnp.dot(a.astype(BF16), vblk, preferred_element_type=F32)
            return rsum + jnp.sum(log_n, axis=1, keepdims=True), acc

        carry = tile(qi, jnp.zeros((t, 1), F32), jnp.zeros((t, d), F32), diag)
        _, acc = lax.fori_loop(0, qi, lambda n, c: tile(qi - 1 - n, c[0], c[1], None), carry)
        o_ref[...] = acc.astype(o_ref.dtype)
        o32_ref[...] = acc

    qs = pl.BlockSpec((None, t, d), lambda hh, i: (hh, i, 0))
    ks = pl.BlockSpec((None, s, d), lambda hh, i: (hh, 0, 0))
    return pl.pallas_call(
        body, name=name, grid=(h, s // t), in_specs=[qs, ks, ks], out_specs=[qs, qs],
        out_shape=[jax.ShapeDtypeStruct(q.shape, BF16), jax.ShapeDtypeStruct(q.shape, F32)],
        compiler_params=_params(dimension_semantics=("parallel", "parallel")),
    )(q, k, v)


def _sb_bwd(q, k, v, o32, do, dk0, dv0, *, name):
    h, s, d = q.shape
    t = _blk(s, 256)
    scale = 1.0 / math.sqrt(d)
    has_init = dk0 is not None

    def body(*refs):
        if has_init:
            q_ref, k_ref, v_ref, o_ref, do_ref, dk0_ref, dv0_ref, dq_ref, dk_ref, dv_ref = refs
        else:
            q_ref, k_ref, v_ref, o_ref, do_ref, dq_ref, dk_ref, dv_ref = refs
        qi = pl.program_id(1)

        @pl.when(qi == 0)
        def _():
            if has_init:
                dk_ref[...] = dk0_ref[...]
                dv_ref[...] = dv0_ref[...]
            else:
                dk_ref[...] = jnp.zeros_like(dk_ref)
                dv_ref[...] = jnp.zeros_like(dv_ref)

        qv = q_ref[...]
        dov = do_ref[...]
        dsum = jnp.sum(o_ref[...] * dov.astype(F32), axis=1, keepdims=True)
        row = lax.broadcasted_iota(jnp.int32, (t, t), 0)
        col = lax.broadcasted_iota(jnp.int32, (t, t), 1)
        tri = (row > col).astype(BF16)
        tri_inc = (row >= col).astype(BF16)
        diag = col < row

        def tile(kb, rsum, gsum, dq, causal):
            off = pl.multiple_of(kb * t, t)
            kblk = k_ref[pl.ds(off, t), :]
            vblk = v_ref[pl.ds(off, t), :]
            z, e, log_b, log_n = _sb_tile_logs(qv, kblk, scale, causal)
            a = jnp.exp(log_b + _split_dot(log_n, tri) + rsum)
            if causal is not None:
                a = jnp.where(causal, a, 0.0)
            ab = a.astype(BF16)
            da = lax.dot_general(dov, vblk, _NT, preferred_element_type=F32)
            g = ab.astype(F32) * da
            pre = dsum - gsum - _split_dot(g, tri_inc)
            inv = 1.0 / (1.0 + e)
            pos = z >= 0.0
            beta = jnp.where(pos, inv, e * inv)
            one_m_beta = jnp.where(pos, e * inv, inv)
            dz = g * one_m_beta - pre * beta
            if causal is not None:
                dz = jnp.where(causal, dz, 0.0)
            dzb = (dz * scale).astype(BF16)
            dq = dq + jnp.dot(dzb, kblk, preferred_element_type=F32)
            dk_ref[pl.ds(off, t), :] += lax.dot_general(dzb, qv, _TN, preferred_element_type=F32)
            dv_ref[pl.ds(off, t), :] += lax.dot_general(ab, dov, _TN, preferred_element_type=F32)
            return (rsum + jnp.sum(log_n, axis=1, keepdims=True),
                    gsum + jnp.sum(g, axis=1, keepdims=True), dq)

        zero = jnp.zeros((t, 1), F32)
        carry = tile(qi, zero, zero, jnp.zeros((t, d), F32), diag)
        _, _, dq = lax.fori_loop(0, qi, lambda n, c: tile(qi - 1 - n, c[0], c[1], c[2], None), carry)
        dq_ref[...] = dq.astype(dq_ref.dtype)

    qs = pl.BlockSpec((None, t, d), lambda hh, i: (hh, i, 0))
    ks = pl.BlockSpec((None, s, d), lambda hh, i: (hh, 0, 0))
    in_specs = [qs, ks, ks, qs, qs] + ([ks, ks] if has_init else [])
    operands = [q, k, v, o32, do] + ([dk0, dv0] if has_init else [])
    acc = jax.ShapeDtypeStruct(q.shape, F32)
    return pl.pallas_call(
        body, name=name, grid=(h, s // t), in_specs=in_specs, out_specs=[qs, ks, ks],
        out_shape=[jax.ShapeDtypeStruct(q.shape, BF16), acc, acc],
        compiler_params=_params(dimension_semantics=("parallel", "arbitrary")),
    )(*operands)


def _position():
    x, y, c = lax.axis_index("x"), lax.axis_index("y"), lax.axis_index("c")
    return x, y, c, [(1 - x, y), (x, 1 - y), (1 - x, 1 - y)]


_ANY = pl.BlockSpec(memory_space=pl.ANY)
N_PEER_CHIPS = N_CHIPS - 1


def _all_gather_chips(shards, *, name):
    n = len(shards)

    def body(*refs):
        ins, outs = refs[:n], refs[n:2 * n]
        send_sems, recv_sems, local_sems = refs[2 * n:]
        x, y, c, peers = _position()
        me = 2 * x + y
        copies = []
        for a in range(n):
            copies.append(pltpu.make_async_copy(ins[a], outs[a].at[me], local_sems.at[a]))
            for j, (px, py) in enumerate(peers):
                copies.append(pltpu.make_async_remote_copy(
                    src_ref=ins[a], dst_ref=outs[a].at[me],
                    send_sem=send_sems.at[a * N_PEER_CHIPS + j], recv_sem=recv_sems.at[a * N_PEER_CHIPS + j],
                    device_id=(px, py, c), device_id_type=MESH))
        for cp in copies:
            cp.start()
        for cp in copies:
            cp.wait()

    return pl.pallas_call(
        body, name=name, in_specs=[_ANY] * n, out_specs=[_ANY] * n,
        out_shape=[jax.ShapeDtypeStruct((N_CHIPS, *s.shape), s.dtype) for s in shards],
        scratch_shapes=[pltpu.SemaphoreType.DMA((n * N_PEER_CHIPS,)), pltpu.SemaphoreType.DMA((n * N_PEER_CHIPS,)),
                        pltpu.SemaphoreType.DMA((n,))],
        compiler_params=pltpu.CompilerParams(has_side_effects=True),
    )(*shards)


def _scatter_to_chips(parts, *, name):
    n = len(parts)

    def body(*refs):
        ins, outs = refs[:n], refs[n:2 * n]
        send_sems, recv_sems, local_sems = refs[2 * n:]
        x, y, c, peers = _position()
        me = 2 * x + y
        copies = []
        for a in range(n):
            copies.append(pltpu.make_async_copy(ins[a].at[me], outs[a].at[me], local_sems.at[a]))
            for j, (px, py) in enumerate(peers):
                copies.append(pltpu.make_async_remote_copy(
                    src_ref=ins[a].at[2 * px + py], dst_ref=outs[a].at[me],
                    send_sem=send_sems.at[a * N_PEER_CHIPS + j], recv_sem=recv_sems.at[a * N_PEER_CHIPS + j],
                    device_id=(px, py, c), device_id_type=MESH))
        for cp in copies:
            cp.start()
        for cp in copies:
            cp.wait()

    return pl.pallas_call(
        body, name=name, in_specs=[_ANY] * n, out_specs=[_ANY] * n,
        out_shape=[jax.ShapeDtypeStruct(p.shape, p.dtype) for p in parts],
        scratch_shapes=[pltpu.SemaphoreType.DMA((n * N_PEER_CHIPS,)), pltpu.SemaphoreType.DMA((n * N_PEER_CHIPS,)),
                        pltpu.SemaphoreType.DMA((n,))],
        compiler_params=pltpu.CompilerParams(has_side_effects=True),
    )(*parts)


def _sibling_exchange(arrs, *, name):
    n = len(arrs)

    def body(*refs):
        ins, outs = refs[:n], refs[n:2 * n]
        send_sems, recv_sems = refs[2 * n:]
        x, y, c, _ = _position()
        copies = [pltpu.make_async_remote_copy(
            src_ref=ins[a], dst_ref=outs[a], send_sem=send_sems.at[a], recv_sem=recv_sems.at[a],
            device_id=(x, y, 1 - c), device_id_type=MESH) for a in range(n)]
        for cp in copies:
            cp.start()
        for cp in copies:
            cp.wait()

    return pl.pallas_call(
        body, name=name, in_specs=[_ANY] * n, out_specs=[_ANY] * n,
        out_shape=[jax.ShapeDtypeStruct(a.shape, a.dtype) for a in arrs],
        scratch_shapes=[pltpu.SemaphoreType.DMA((n,)), pltpu.SemaphoreType.DMA((n,))],
        compiler_params=pltpu.CompilerParams(has_side_effects=True),
    )(*arrs)


def _small_all_reduce(v, *, name):
    r, cdim = v.shape

    def body(v_ref, o_ref, slots, send_sems, recv_sems):
        x, y, c, _ = _position()
        me = 4 * x + 2 * y + c
        slots[me] = v_ref[...]
        copies = []
        for j in range(1, N_DEVICES):
            peer = (x ^ ((j >> 2) & 1), y ^ ((j >> 1) & 1), c ^ (j & 1))
            copies.append(pltpu.make_async_remote_copy(
                src_ref=v_ref, dst_ref=slots.at[me], send_sem=send_sems.at[j - 1], recv_sem=recv_sems.at[j - 1],
                device_id=peer, device_id_type=MESH))
        for cp in copies:
            cp.start()
        for cp in copies:
            cp.wait()
        acc = slots[0]
        for dev in range(1, N_DEVICES):
            acc = acc + slots[dev]
        o_ref[...] = acc

    vm = pl.BlockSpec(memory_space=pltpu.VMEM)
    return pl.pallas_call(
        body, name=name, in_specs=[vm], out_specs=vm,
        out_shape=jax.ShapeDtypeStruct(v.shape, F32),
        scratch_shapes=[pltpu.VMEM((N_DEVICES, r, cdim), F32),
                        pltpu.SemaphoreType.DMA((N_DEVICES - 1,)), pltpu.SemaphoreType.DMA((N_DEVICES - 1,))],
        compiler_params=pltpu.CompilerParams(has_side_effects=True),
    )(v)


def _sum_chips(parts, *, name):
    _, r, c = parts.shape
    bm = _blk(r, 256)

    def body(p_ref, o_ref):
        acc = p_ref[0].astype(F32)
        for kk in range(1, N_CHIPS):
            acc = acc + p_ref[kk].astype(F32)
        o_ref[...] = acc

    return pl.pallas_call(
        body, name=name, grid=(r // bm,),
        in_specs=[pl.BlockSpec((N_CHIPS, bm, c), lambda i: (0, i, 0))],
        out_specs=pl.BlockSpec((bm, c), lambda i: (i, 0)),
        out_shape=jax.ShapeDtypeStruct((r, c), F32),
        compiler_params=_params(dimension_semantics=("parallel",)),
    )(parts)


def _adamw(w, m, v, g_a, g_b, *, name):
    r, c = w.shape
    bm = _blk(r, 256)
    two = g_b is not None
    bc1 = 1.0 - ADAM_B1 ** ADAM_STEP
    bc2 = 1.0 - ADAM_B2 ** ADAM_STEP

    def body(*refs):
        if two:
            w_ref, m_ref, v_ref, ga_ref, gb_ref, g_ref, d_ref, nm_ref, nv_ref = refs
            g = ga_ref[...] + gb_ref[...]
        else:
            w_ref, m_ref, v_ref, ga_ref, g_ref, d_ref, nm_ref, nv_ref = refs
            g = ga_ref[...]
        nm = ADAM_B1 * m_ref[...] + (1.0 - ADAM_B1) * g
        nv = ADAM_B2 * v_ref[...] + (1.0 - ADAM_B2) * (g * g)
        g_ref[...] = g
        nm_ref[...] = nm
        nv_ref[...] = nv
        d_ref[...] = -ADAM_LR * ((nm / bc1) / (jnp.sqrt(nv / bc2) + ADAM_EPS) + ADAM_WD * w_ref[...])

    spec = pl.BlockSpec((bm, c), lambda i: (i, 0))
    out = jax.ShapeDtypeStruct((r, c), F32)
    operands = [w, m, v, g_a] + ([g_b] if two else [])
    return pl.pallas_call(
        body, name=name, grid=(r // bm,),
        in_specs=[spec] * len(operands), out_specs=[spec] * 4, out_shape=[out] * 4,
        compiler_params=_params(dimension_semantics=("parallel",)),
    )(*operands)


def _heads(a):
    s, w = a.shape
    return a.reshape(s, w // HEAD_DIM, HEAD_DIM).transpose(1, 0, 2)


def _unheads(a):
    h, s, d = a.shape
    return a.transpose(1, 0, 2).reshape(s, h * d)


def _pack_small(mix, ffn, kvn, memn, fin, conv):
    d = mix.shape[-1]
    flat = conv.reshape(-1)
    rows_conv = SMALL_ROWS - 11
    flat = jnp.pad(flat, (0, rows_conv * d - flat.shape[0]))
    return jnp.concatenate([mix, ffn, kvn.reshape(1, d), memn.reshape(1, d), fin.reshape(1, d),
                            flat.reshape(rows_conv, d)], axis=0)


def _unpack_small(buf, conv_shape):
    n = math.prod(conv_shape)
    return (buf[0:4], buf[4:8], buf[8], buf[9], buf[10], buf[11:].reshape(-1)[:n].reshape(conv_shape))


def kernel(x, mem, mix_norm, a_in, conv_w, b_in, kv_norm, w_kv_shared, w_mem_kv, w_o, ffn_norm, w_gate, w_up, w_down, mem_norm, final_norm, loss_target, m_mix_norm, m_a_in, m_conv_w, m_b_in, m_kv_norm, m_w_kv_shared, m_w_mem_kv, m_w_o, m_ffn_norm, m_w_gate, m_w_up, m_w_down, m_mem_norm, m_final_norm, v_mix_norm, v_a_in, v_conv_w, v_b_in, v_kv_norm, v_w_kv_shared, v_w_mem_kv, v_w_o, v_ffn_norm, v_w_gate, v_w_up, v_w_down, v_mem_norm, v_final_norm):
    s, d = x.shape[1], x.shape[2]
    n_mem = mem.shape[1]
    depth = mix_norm.shape[0]
    n_a = a_in.shape[0]
    main_w = conv_w.shape[2] * N_CHIPS
    mem_w = w_mem_kv.shape[2] // 2
    ffn_c = w_gate.shape[2]
    kv_c = w_kv_shared.shape[1]
    a_c = a_in.shape[2]
    chip = 2 * lax.axis_index("x") + lax.axis_index("y")

    x0 = x[0]
    mem0 = mem[0]
    tgt = loss_target[0]
    bs = _blk(s, 1024)

    w_kv3 = w_kv_shared[None]
    gathered = _all_gather_chips(
        [a_in.astype(BF16), b_in.astype(BF16), w_kv3.astype(BF16), w_mem_kv.astype(BF16), w_o.astype(BF16),
         w_gate.astype(BF16), w_up.astype(BF16), w_down.astype(BF16), conv_w], name="gather_weights")
    wa, wb, wkv, wm, wo, wg, wu, wd, conv_parts = gathered
    conv_full = jnp.concatenate([conv_parts[kk] for kk in range(N_CHIPS)], axis=-1)

    def gain(vec):
        return vec.reshape(1, d)

    mem_n = _rms_fwd(mem0, gain(mem_norm), name="mem_norm_fwd")
    saved = []
    k_sh = v_sh = hk = x_kv = None
    xc = x0
    for i in range(depth):
        st = {"x_in": xc}
        h = _rms_fwd(xc, gain(mix_norm[i]), name="mix_norm_fwd")
        mkv = _mm(Op(mem_n), Op(wm, 'r', i), name="mem_kv_proj", bm=n_mem, bn=2 * mem_w, bk=wm.shape[2],
                  out_dtype=BF16)
        mem_k, mem_v = _heads(mkv[:, :mem_w]), _heads(mkv[:, mem_w:])
        if i < n_a:
            p = _mm(Op(h), Op(wa, 'c', i), name="a_in_proj", bm=bs, bn=a_c, bk=d, out_dtype=BF16)
            y_main = _conv_fwd(p, conv_full[i], name="conv_fwd")
            q_mem = _heads(p[:, 3 * main_w:])
        else:
            p = _mm(Op(h), Op(wb, 'r', i - n_a), name="b_in_proj", bm=bs, bn=d, bk=wb.shape[2], out_dtype=BF16)
            q_sb = _heads(p[:, :main_w])
            o_sb, o_sb32 = _sb_fwd(q_sb, k_sh, v_sh, name="sb_fwd")
            y_main = _unheads(o_sb)
            q_mem = _heads(p[:, main_w:])
            st.update(q_sb=q_sb, o_sb32=o_sb32)
        y_mem = _mem_fwd(q_mem, mem_k, mem_v, name="mem_attn_fwd")
        y = jnp.concatenate([y_main, _unheads(y_mem)], axis=-1)
        x_mid = _mm(Op(y), Op(wo, 'r', i), name="w_o_proj", bm=bs, bn=d, bk=wo.shape[2], out_dtype=F32,
                    res=Op(xc))
        h2 = _rms_fwd(x_mid, gain(ffn_norm[i]), name="ffn_norm_fwd")
        gate = _mm(Op(h2), Op(wg, 'c', i), name="w_gate_proj", bm=bs, bn=ffn_c, bk=d, out_dtype=BF16,
                   out_chunk='c')
        up = _mm(Op(h2), Op(wu, 'c', i), name="w_up_proj", bm=bs, bn=ffn_c, bk=d, out_dtype=BF16,
                 out_chunk='c')
        act = _swiglu_fwd(gate, up, name="swiglu_fwd")
        xc = _mm(Op(act, 'c'), Op(wd, 'r', i), name="w_down_proj", bm=bs, bn=d, bk=ffn_c, out_dtype=F32,
                 res=Op(x_mid))
        st.update(h=h, p=p, mem_k=mem_k, mem_v=mem_v, q_mem=q_mem, y=y, x_mid=x_mid, h2=h2, gate=gate, up=up,
                  act=act)
        saved.append(st)
        if i == n_a - 1:
            x_kv = xc
            hk = _rms_fwd(xc, gain(kv_norm), name="kv_norm_fwd")
            kv = _mm(Op(hk), Op(wkv, 'c', 0), name="kv_proj", bm=bs, bn=kv_c, bk=d, out_dtype=BF16)
            k_sh, v_sh = _heads(kv[:, :main_w]), _heads(kv[:, main_w:])

    dx, dg_final, loss_part = _final_loss(xc, gain(final_norm), tgt, name="final_norm_loss")
    loss = lax.psum(loss_part[0, 0], ("x", "y", "c"))

    g_a, g_b, g_m, g_o, g_g, g_u, g_d = ([None] * n_a, [None] * (depth - n_a), [None] * depth, [None] * depth,
                                         [None] * depth, [None] * depth, [None] * depth)
    dg_mix, dg_ffn, dconv = [None] * depth, [None] * depth, [None] * n_a
    dk_sh = dv_sh = None
    dmem_n = None
    g_kv = dg_kv = None
    for i in reversed(range(depth)):
        st = saved[i]
        dact = _mm(Op(dx), Op(wd, 'r', i), name="w_down_dact", tb=True, bm=bs, bn=ffn_c, bk=d, out_dtype=BF16,
                   out_chunk='c')
        g_d[i] = _mm(Op(st["act"], 'c'), Op(dx), name="w_down_grad", ta=True, bm=ffn_c, bn=d, bk=bs,
                     out_dtype=BF16, out_chunk='r')
        dgate, dup = _swiglu_bwd(dact, st["gate"], st["up"], name="swiglu_bwd")
        g_g[i] = _mm(Op(st["h2"]), Op(dgate, 'c'), name="w_gate_grad", ta=True, bm=d, bn=ffn_c, bk=bs,
                     out_dtype=BF16, out_chunk='c')
        g_u[i] = _mm(Op(st["h2"]), Op(dup, 'c'), name="w_up_grad", ta=True, bm=d, bn=ffn_c, bk=bs,
                     out_dtype=BF16, out_chunk='c')
        dh2 = _mm(Op(dgate, 'c'), Op(wg, 'c', i), name="w_gate_dh", tb=True, bm=bs, bn=d, bk=ffn_c,
                  out_dtype=F32)
        dh2 = _mm(Op(dup, 'c'), Op(wu, 'c', i), name="w_up_dh", tb=True, bm=bs, bn=d, bk=ffn_c, out_dtype=F32,
                  res=Op(dh2))
        dx_mid, dg_ffn[i] = _rms_bwd(st["x_mid"], gain(ffn_norm[i]), dh2, dx, name="ffn_norm_bwd")
        dy = _mm(Op(dx_mid), Op(wo, 'r', i), name="w_o_dy", tb=True, bm=bs, bn=wo.shape[2], bk=d,
                 out_dtype=BF16)
        g_o[i] = _mm(Op(st["y"]), Op(dx_mid), name="w_o_grad", ta=True, bm=wo.shape[2], bn=d, bk=bs,
                     out_dtype=BF16, out_chunk='r')
        dq_mem, dmk, dmv = _mem_bwd(st["q_mem"], st["mem_k"], st["mem_v"], _heads(dy[:, main_w:]),
                                    name="mem_attn_bwd")
        dmkv = jnp.concatenate([_unheads(dmk), _unheads(dmv)], axis=-1)
        g_m[i] = _mm(Op(mem_n), Op(dmkv), name="mem_kv_grad", ta=True, bm=wm.shape[2], bn=2 * mem_w, bk=n_mem,
                     out_dtype=BF16, out_chunk='r')
        dmem_n = _mm(Op(dmkv), Op(wm, 'r', i), name="mem_kv_dmem", tb=True, bm=n_mem, bn=wm.shape[2],
                     bk=2 * mem_w, out_dtype=F32, res=None if dmem_n is None else Op(dmem_n))
        if i < n_a:
            db, dc, du, dconv[i] = _conv_bwd(st["p"], conv_full[i], dy[:, :main_w], name="conv_bwd")
            dp = jnp.concatenate([db, dc, du, _unheads(dq_mem)], axis=-1)
            g_a[i] = _mm(Op(st["h"]), Op(dp), name="a_in_grad", ta=True, bm=d, bn=a_c, bk=bs, out_dtype=BF16,
                         out_chunk='c')
            dh = _mm(Op(dp), Op(wa, 'c', i), name="a_in_dh", tb=True, bm=bs, bn=d, bk=a_c, out_dtype=F32)
        else:
            dq_sb, dk_sh, dv_sh = _sb_bwd(st["q_sb"], k_sh, v_sh, st["o_sb32"], _heads(dy[:, :main_w]),
                                          dk_sh, dv_sh, name="sb_bwd")
            dp = jnp.concatenate([_unheads(dq_sb), _unheads(dq_mem)], axis=-1)
            g_b[i - n_a] = _mm(Op(st["h"]), Op(dp), name="b_in_grad", ta=True, bm=wb.shape[2], bn=d, bk=bs,
                               out_dtype=BF16, out_chunk='r')
            dh = _mm(Op(dp), Op(wb, 'r', i - n_a), name="b_in_dh", tb=True, bm=bs, bn=wb.shape[2], bk=d,
                     out_dtype=F32)
        dx, dg_mix[i] = _rms_bwd(st["x_in"], gain(mix_norm[i]), dh, dx_mid, name="mix_norm_bwd")
        if i == n_a:
            dkv = jnp.concatenate([_unheads(dk_sh), _unheads(dv_sh)], axis=-1)
            g_kv = _mm(Op(hk), Op(dkv), name="kv_grad", ta=True, bm=d, bn=kv_c, bk=bs, out_dtype=BF16,
                       out_chunk='c')
            dhk = _mm(Op(dkv), Op(wkv, 'c', 0), name="kv_dh", tb=True, bm=bs, bn=d, bk=kv_c, out_dtype=F32)
            dx, dg_kv = _rms_bwd(x_kv, gain(kv_norm), dhk, dx, name="kv_norm_bwd")
    _, dg_mem = _rms_bwd(mem0, gain(mem_norm), dmem_n, None, name="mem_norm_bwd")

    big = [("a_in", a_in, m_a_in, v_a_in, jnp.stack(g_a, axis=1)),
           ("b_in", b_in, m_b_in, v_b_in, jnp.stack(g_b, axis=1)),
           ("w_kv_shared", w_kv3, m_w_kv_shared[None], v_w_kv_shared[None], g_kv[:, None]),
           ("w_mem_kv", w_mem_kv, m_w_mem_kv, v_w_mem_kv, jnp.stack(g_m, axis=1)),
           ("w_o", w_o, m_w_o, v_w_o, jnp.stack(g_o, axis=1)),
           ("w_gate", w_gate, m_w_gate, v_w_gate, jnp.stack(g_g, axis=1)),
           ("w_up", w_up, m_w_up, v_w_up, jnp.stack(g_u, axis=1)),
           ("w_down", w_down, m_w_down, v_w_down, jnp.stack(g_d, axis=1))]
    landed = _scatter_to_chips([b[4] for b in big], name="scatter_weight_grads")
    core_sums = [_sum_chips(p.reshape(N_CHIPS, -1, p.shape[-1]), name="sum_chip_partials") for p in landed]
    sibling_sums = _sibling_exchange(core_sums, name="exchange_core_sums")
    results = {}
    for (wname, w, mm_, vv_, _), own, sib in zip(big, core_sums, sibling_sums):
        flat = lambda t: t.reshape(-1, t.shape[-1])
        outs = _adamw(flat(w), flat(mm_), flat(vv_), own, sib, name="adamw")
        results[wname] = [o.reshape(w.shape[1:] if wname == "w_kv_shared" else w.shape) for o in outs]

    small_g = _pack_small(jnp.concatenate(dg_mix, axis=0), jnp.concatenate(dg_ffn, axis=0), dg_kv, dg_mem,
                          dg_final, jnp.stack(dconv, axis=0))
    small_g = _small_all_reduce(small_g, name="all_reduce_small_grads")
    conv_shape_full = (n_a, CONV_TAPS, main_w)
    gs = list(_unpack_small(small_g, conv_shape_full))
    gs[5] = lax.dynamic_slice_in_dim(gs[5], chip * conv_w.shape[2], conv_w.shape[2], axis=2)
    small_outs = _adamw(_pack_small(mix_norm, ffn_norm, kv_norm, mem_norm, final_norm, conv_w),
                        _pack_small(m_mix_norm, m_ffn_norm, m_kv_norm, m_mem_norm, m_final_norm, m_conv_w),
                        _pack_small(v_mix_norm, v_ffn_norm, v_kv_norm, v_mem_norm, v_final_norm, v_conv_w),
                        _pack_small(*gs), None, name="adamw_small")
    small_names = ["mix_norm", "ffn_norm", "kv_norm", "mem_norm", "final_norm", "conv_w"]
    for kind, buf in enumerate(small_outs):
        for wname, val in zip(small_names, _unpack_small(buf, conv_w.shape)):
            results.setdefault(wname, [None] * 4)[kind] = val

    order = ["mix_norm", "a_in", "conv_w", "b_in", "kv_norm", "w_kv_shared", "w_mem_kv", "w_o", "ffn_norm",
             "w_gate", "w_up", "w_down", "mem_norm", "final_norm"]
    return (loss, dx[None], *[results[nm][0] for nm in order], *[results[nm][1] for nm in order],
            *[results[nm][2] for nm in order], *[results[nm][3] for nm in order])
```

```python
import math
from typing import NamedTuple, Optional

import jax
import jax.numpy as jnp
from jax import lax
from jax.experimental import pallas as pl
from jax.experimental.pallas import tpu as pltpu

F32 = jnp.float32
BF16 = jnp.bfloat16
MESH = pl.DeviceIdType.MESH

N_CHIPS = 4
N_DEVICES = 8
HEAD_DIM = 64
CONV_TAPS = 3
NORM_EPS = 1e-6
V7X_VMEM_BYTES = 64 * 1024 * 1024
VMEM_LIMIT = V7X_VMEM_BYTES - 8 * 1024 * 1024
LANES = 128
SMALL_ROWS = 16

ADAM_LR = 0.001
ADAM_B1 = 0.9
ADAM_B2 = 0.999
ADAM_EPS = 1e-08
ADAM_WD = 0.01
ADAM_STEP = 10


def _params(**kw):
    return pltpu.CompilerParams(vmem_limit_bytes=VMEM_LIMIT, **kw)


def _blk(n, pref):
    b = min(n, pref)
    assert n % b == 0, (n, pref)
    return b


class Op(NamedTuple):
    arr: jax.Array
    chunk: Optional[str] = None
    layer: Optional[int] = None


def _op_spec(op_chunk, op_layer, shape2, br, bc, pick):
    r, c = shape2
    lead = () if op_layer is None else (op_layer,)
    none = (None,) * len(lead)
    if op_chunk is None:
        def imap(i, j, k):
            rb, cb = pick(i, j, k)
            return (*lead, rb, cb)
        return pl.BlockSpec((*none, br, bc), imap)
    if op_chunk == 'r' and br == N_CHIPS * r:
        def imap(i, j, k):
            rb, cb = pick(i, j, k)
            return (0, *lead, 0, cb)
        return pl.BlockSpec((N_CHIPS, *none, r, bc), imap)
    if op_chunk == 'r':
        n = r // br
        assert r % br == 0

        def imap(i, j, k):
            rb, cb = pick(i, j, k)
            return (rb // n, *lead, rb % n, cb)
        return pl.BlockSpec((None, *none, br, bc), imap)
    n = c // bc
    assert c % bc == 0

    def imap(i, j, k):
        rb, cb = pick(i, j, k)
        return (cb // n, *lead, rb, cb % n)
    return pl.BlockSpec((None, *none, br, bc), imap)


def _mm(a, b, *, name, ta=False, tb=False, bm, bn, bk, out_dtype, out_chunk=None, res=None):
    def dims(op):
        r, c = op.arr.shape[-2:]
        return (r * N_CHIPS if op.chunk == 'r' else r, c * N_CHIPS if op.chunk == 'c' else c)

    ar, ac = dims(a)
    br_, bc_ = dims(b)
    m, ka = (ac, ar) if ta else (ar, ac)
    kb, n = (bc_, br_) if tb else (br_, bc_)
    assert ka == kb, (name, ka, kb)
    assert m % bm == 0 and n % bn == 0 and ka % bk == 0, (name, m, n, ka, bm, bn, bk)
    nk = ka // bk

    if ta:
        a_spec = _op_spec(a.chunk, a.layer, a.arr.shape[-2:], bk, bm, lambda i, j, k: (k, i))
    else:
        a_spec = _op_spec(a.chunk, a.layer, a.arr.shape[-2:], bm, bk, lambda i, j, k: (i, k))
    if tb:
        b_spec = _op_spec(b.chunk, b.layer, b.arr.shape[-2:], bn, bk, lambda i, j, k: (j, k))
    else:
        b_spec = _op_spec(b.chunk, b.layer, b.arr.shape[-2:], bk, bn, lambda i, j, k: (k, j))

    if out_chunk == 'r':
        out_shape2 = (m // N_CHIPS, n)
    elif out_chunk == 'c':
        out_shape2 = (m, n // N_CHIPS)
    else:
        out_shape2 = (m, n)
    o_spec = _op_spec(out_chunk, None, out_shape2, bm, bn, lambda i, j, k: (i, j))
    out_full = out_shape2 if out_chunk is None else (N_CHIPS, *out_shape2)

    contract = (((0 if ta else 1,), (1 if tb else 0,)), ((), ()))
    has_res = res is not None

    def block2(ref):
        v = ref[...]
        return v.reshape(-1, v.shape[-1]).astype(BF16)

    def body(*refs):
        r_ref = refs[2] if has_res else None
        a_ref, b_ref = refs[:2]
        o_ref = refs[3 if has_res else 2]
        prod = lax.dot_general(block2(a_ref), block2(b_ref), contract, preferred_element_type=F32)
        if nk == 1:
            if has_res:
                prod = prod + r_ref[...].astype(F32)
            o_ref[...] = prod.astype(o_ref.dtype)
            return
        acc_ref = refs[-1]
        k = pl.program_id(2)

        @pl.when(k == 0)
        def _():
            acc_ref[...] = prod

        @pl.when(k > 0)
        def _():
            acc_ref[...] += prod

        @pl.when(k == nk - 1)
        def _():
            acc = acc_ref[...]
            if has_res:
                acc = acc + r_ref[...].astype(F32)
            o_ref[...] = acc.astype(o_ref.dtype)

    in_specs = [a_spec, b_spec]
    operands = [a.arr, b.arr]
    if has_res:
        in_specs.append(_op_spec(res.chunk, res.layer, res.arr.shape[-2:], bm, bn, lambda i, j, k: (i, j)))
        operands.append(res.arr)
    return pl.pallas_call(
        body, name=name, grid=(m // bm, n // bn, nk),
        in_specs=in_specs, out_specs=o_spec,
        out_shape=jax.ShapeDtypeStruct(out_full, out_dtype),
        scratch_shapes=[pltpu.VMEM((bm, bn), F32)] if nk > 1 else [],
        compiler_params=_params(dimension_semantics=("parallel", "parallel", "arbitrary")),
    )(*operands)


def _rms_fwd(x, g, *, name):
    r, d = x.shape
    bm = _blk(r, 512)

    def body(x_ref, g_ref, o_ref):
        xv = x_ref[...]
        rstd = lax.rsqrt(jnp.mean(xv * xv, axis=-1, keepdims=True) + NORM_EPS)
        o_ref[...] = ((xv * rstd) * g_ref[...]).astype(o_ref.dtype)

    return pl.pallas_call(
        body, name=name, grid=(r // bm,),
        in_specs=[pl.BlockSpec((bm, d), lambda i: (i, 0)), pl.BlockSpec((1, d), lambda i: (0, 0))],
        out_specs=pl.BlockSpec((bm, d), lambda i: (i, 0)),
        out_shape=jax.ShapeDtypeStruct((r, d), BF16),
        compiler_params=_params(dimension_semantics=("parallel",)),
    )(x, g)


def _rms_bwd(x, g, dh, dres, *, name):
    r, d = x.shape
    bm = _blk(r, 512)
    has_res = dres is not None

    def body(*refs):
        if has_res:
            x_ref, g_ref, dh_ref, dres_ref, dx_ref, dg_ref = refs
        else:
            x_ref, g_ref, dh_ref, dx_ref, dg_ref = refs

        @pl.when(pl.program_id(0) == 0)
        def _():
            dg_ref[...] = jnp.zeros_like(dg_ref)

        xv = x_ref[...]
        rstd = lax.rsqrt(jnp.mean(xv * xv, axis=-1, keepdims=True) + NORM_EPS)
        xh = xv * rstd
        dhv = dh_ref[...].astype(F32)
        dg_ref[...] += jnp.sum(dhv * xh, axis=0, keepdims=True)
        dxh = dhv * g_ref[...]
        dx = rstd * (dxh - xh * jnp.mean(dxh * xh, axis=-1, keepdims=True))
        if has_res:
            dx = dres_ref[...] + dx
        dx_ref[...] = dx

    row = pl.BlockSpec((bm, d), lambda i: (i, 0))
    vec = pl.BlockSpec((1, d), lambda i: (0, 0))
    in_specs = [row, vec, row] + ([row] if has_res else [])
    operands = [x, g, dh] + ([dres] if has_res else [])
    return pl.pallas_call(
        body, name=name, grid=(r // bm,),
        in_specs=in_specs, out_specs=[row, vec],
        out_shape=[jax.ShapeDtypeStruct((r, d), F32), jax.ShapeDtypeStruct((1, d), F32)],
        compiler_params=_params(dimension_semantics=("arbitrary",)),
    )(*operands)


def _final_loss(x, g, tgt, *, name):
    r, d = x.shape
    bm = _blk(r, 512)

    def body(x_ref, g_ref, t_ref, dx_ref, dg_ref, loss_ref):
        @pl.when(pl.program_id(0) == 0)
        def _():
            dg_ref[...] = jnp.zeros_like(dg_ref)
            loss_ref[...] = jnp.zeros_like(loss_ref)

        xv = x_ref[...]
        gv = g_ref[...]
        rstd = lax.rsqrt(jnp.mean(xv * xv, axis=-1, keepdims=True) + NORM_EPS)
        xh = xv * rstd
        diff = xh * gv - t_ref[...]
        loss_ref[...] += jnp.sum(diff * diff) * (0.5 / d)
        dy = diff * (1.0 / d)
        dg_ref[...] += jnp.sum(dy * xh, axis=0, keepdims=True)
        dxh = dy * gv
        dx_ref[...] = rstd * (dxh - xh * jnp.mean(dxh * xh, axis=-1, keepdims=True))

    row = pl.BlockSpec((bm, d), lambda i: (i, 0))
    vec = pl.BlockSpec((1, d), lambda i: (0, 0))
    return pl.pallas_call(
        body, name=name, grid=(r // bm,),
        in_specs=[row, vec, row],
        out_specs=[row, vec, pl.BlockSpec((1, LANES), lambda i: (0, 0))],
        out_shape=[jax.ShapeDtypeStruct((r, d), F32), jax.ShapeDtypeStruct((1, d), F32),
                   jax.ShapeDtypeStruct((1, LANES), F32)],
        compiler_params=_params(dimension_semantics=("arbitrary",)),
    )(x, g, tgt)


def _shift_down(v, k, row):
    return jnp.where(row >= k, pltpu.roll(v, k, 0), 0.0)


def _shift_up(v, k, row, s):
    return jnp.where(row < s - k, pltpu.roll(v, s - k, 0), 0.0)


def _conv_fwd(p, w, *, name):
    s = p.shape[0]
    width = w.shape[1]
    nb = width // LANES

    def body(b_ref, c_ref, u_ref, w_ref, y_ref):
        cu = c_ref[...].astype(F32) * u_ref[...].astype(F32)
        row = lax.broadcasted_iota(jnp.int32, cu.shape, 0)
        wv = w_ref[...]
        conv = wv[2:3] * cu + wv[1:2] * _shift_down(cu, 1, row) + wv[0:1] * _shift_down(cu, 2, row)
        y_ref[...] = (b_ref[...].astype(F32) * conv).astype(y_ref.dtype)

    def col(o):
        return pl.BlockSpec((s, LANES), lambda j: (0, j + o * nb))

    return pl.pallas_call(
        body, name=name, grid=(nb,),
        in_specs=[col(0), col(1), col(2), pl.BlockSpec((CONV_TAPS, LANES), lambda j: (0, j))],
        out_specs=col(0),
        out_shape=jax.ShapeDtypeStruct((s, width), BF16),
        compiler_params=_params(dimension_semantics=("parallel",)),
    )(p, p, p, w)


def _conv_bwd(p, w, dy, *, name):
    s = p.shape[0]
    width = w.shape[1]
    nb = width // LANES

    def body(b_ref, c_ref, u_ref, w_ref, dy_ref, db_ref, dc_ref, du_ref, dw_ref):
        bv = b_ref[...].astype(F32)
        cv = c_ref[...].astype(F32)
        uv = u_ref[...].astype(F32)
        dyv = dy_ref[...].astype(F32)
        cu = cv * uv
        row = lax.broadcasted_iota(jnp.int32, cu.shape, 0)
        wv = w_ref[...]
        cu1 = _shift_down(cu, 1, row)
        cu2 = _shift_down(cu, 2, row)
        conv = wv[2:3] * cu + wv[1:2] * cu1 + wv[0:1] * cu2
        db_ref[...] = (dyv * conv).astype(db_ref.dtype)
        dconv = dyv * bv
        dcu = wv[2:3] * dconv + wv[1:2] * _shift_up(dconv, 1, row, s) + wv[0:1] * _shift_up(dconv, 2, row, s)
        dc_ref[...] = (dcu * uv).astype(dc_ref.dtype)
        du_ref[...] = (dcu * cv).astype(du_ref.dtype)
        dw_ref[0:1, :] = jnp.sum(dconv * cu2, axis=0, keepdims=True)
        dw_ref[1:2, :] = jnp.sum(dconv * cu1, axis=0, keepdims=True)
        dw_ref[2:3, :] = jnp.sum(dconv * cu, axis=0, keepdims=True)

    def col(o):
        return pl.BlockSpec((s, LANES), lambda j: (0, j + o * nb))

    wspec = pl.BlockSpec((CONV_TAPS, LANES), lambda j: (0, j))
    act = jax.ShapeDtypeStruct((s, width), BF16)
    return pl.pallas_call(
        body, name=name, grid=(nb,),
        in_specs=[col(0), col(1), col(2), wspec, col(0)],
        out_specs=[col(0), col(0), col(0), wspec],
        out_shape=[act, act, act, jax.ShapeDtypeStruct((CONV_TAPS, width), F32)],
        compiler_params=_params(dimension_semantics=("parallel",)),
    )(p, p, p, w, dy)


def _swiglu_fwd(g, u, *, name):
    nc, s, f = g.shape
    bm = _blk(s, 512)

    def body(g_ref, u_ref, o_ref):
        gv = g_ref[...].astype(F32)
        o_ref[...] = (gv * jax.nn.sigmoid(gv) * u_ref[...].astype(F32)).astype(o_ref.dtype)

    spec = pl.BlockSpec((None, bm, f), lambda c, i: (c, i, 0))
    return pl.pallas_call(
        body, name=name, grid=(nc, s // bm), in_specs=[spec, spec], out_specs=spec,
        out_shape=jax.ShapeDtypeStruct(g.shape, BF16),
        compiler_params=_params(dimension_semantics=("parallel", "parallel")),
    )(g, u)


def _swiglu_bwd(dact, g, u, *, name):
    nc, s, f = g.shape
    bm = _blk(s, 512)

    def body(d_ref, g_ref, u_ref, dg_ref, du_ref):
        gv = g_ref[...].astype(F32)
        dv = d_ref[...].astype(F32)
        sg = jax.nn.sigmoid(gv)
        silu = gv * sg
        dg_ref[...] = (dv * u_ref[...].astype(F32) * (sg * (1.0 + gv * (1.0 - sg)))).astype(dg_ref.dtype)
        du_ref[...] = (dv * silu).astype(du_ref.dtype)

    spec = pl.BlockSpec((None, bm, f), lambda c, i: (c, i, 0))
    out = jax.ShapeDtypeStruct(g.shape, BF16)
    return pl.pallas_call(
        body, name=name, grid=(nc, s // bm), in_specs=[spec, spec, spec], out_specs=[spec, spec],
        out_shape=[out, out],
        compiler_params=_params(dimension_semantics=("parallel", "parallel")),
    )(dact, g, u)


_NT = (((1,), (1,)), ((), ()))
_TN = (((0,), (0,)), ((), ()))


def _mem_probs(q, k, scale):
    s = lax.dot_general(q, k, _NT, preferred_element_type=F32) * scale
    e = jnp.exp(s - jnp.max(s, axis=-1, keepdims=True))
    return e / jnp.sum(e, axis=-1, keepdims=True)


def _mem_fwd(q, k, v, *, name):
    h, s, d = q.shape
    m = k.shape[1]
    bq = _blk(s, 1024)
    scale = 1.0 / math.sqrt(d)

    def body(q_ref, k_ref, v_ref, o_ref):
        p = _mem_probs(q_ref[...], k_ref[...], scale)
        o_ref[...] = jnp.dot(p.astype(BF16), v_ref[...], preferred_element_type=F32).astype(o_ref.dtype)

    qs = pl.BlockSpec((None, bq, d), lambda hh, i: (hh, i, 0))
    ks = pl.BlockSpec((None, m, d), lambda hh, i: (hh, 0, 0))
    return pl.pallas_call(
        body, name=name, grid=(h, s // bq), in_specs=[qs, ks, ks], out_specs=qs,
        out_shape=jax.ShapeDtypeStruct(q.shape, BF16),
        compiler_params=_params(dimension_semantics=("parallel", "parallel")),
    )(q, k, v)


def _mem_bwd(q, k, v, do, *, name):
    h, s, d = q.shape
    m = k.shape[1]
    bq = _blk(s, 1024)
    scale = 1.0 / math.sqrt(d)

    def body(q_ref, k_ref, v_ref, do_ref, dq_ref, dk_ref, dv_ref):
        @pl.when(pl.program_id(1) == 0)
        def _():
            dk_ref[...] = jnp.zeros_like(dk_ref)
            dv_ref[...] = jnp.zeros_like(dv_ref)

        qv = q_ref[...]
        kv = k_ref[...]
        dov = do_ref[...]
        p = _mem_probs(qv, kv, scale)
        pb = p.astype(BF16)
        dp = lax.dot_general(dov, v_ref[...], _NT, preferred_element_type=F32)
        pf = pb.astype(F32)
        ds = (pf * (dp - jnp.sum(pf * dp, axis=-1, keepdims=True)) * scale).astype(BF16)
        dq_ref[...] = jnp.dot(ds, kv, preferred_element_type=F32).astype(dq_ref.dtype)
        dk_ref[...] += lax.dot_general(ds, qv, _TN, preferred_element_type=F32)
        dv_ref[...] += lax.dot_general(pb, dov, _TN, preferred_element_type=F32)

    qs = pl.BlockSpec((None, bq, d), lambda hh, i: (hh, i, 0))
    ks = pl.BlockSpec((None, m, d), lambda hh, i: (hh, 0, 0))
    kvout = jax.ShapeDtypeStruct(k.shape, F32)
    return pl.pallas_call(
        body, name=name, grid=(h, s // bq), in_specs=[qs, ks, ks, qs], out_specs=[qs, ks, ks],
        out_shape=[jax.ShapeDtypeStruct(q.shape, BF16), kvout, kvout],
        compiler_params=_params(dimension_semantics=("parallel", "arbitrary")),
    )(q, k, v, do)


SB_TILE = 256
SB_STRIP = 32
SB_HEAD_GROUP = 4
SB_BWD_HEAD_GROUP = 2


def _sb_scale(d):
    scale = 1.0 / math.sqrt(d)
    assert math.frexp(scale)[0] == 0.5, "the scale is folded into bf16 q, exact only for a power of two"
    return scale


def _sb_strip_mask(r, t):
    rr = r + lax.broadcasted_iota(jnp.int32, (SB_STRIP, t), 0)
    return lax.broadcasted_iota(jnp.int32, (SB_STRIP, t), 1) < rr


def _neg_abs(z):
    bits = lax.bitcast_convert_type(z, jnp.uint32) | jnp.uint32(0x80000000)
    return lax.bitcast_convert_type(bits, F32)


def _store_split(split_scr, rows, val, t):
    hi = val.astype(BF16)
    split_scr[rows, 0:t] = hi
    split_scr[rows, t:2 * t] = (val - hi.astype(F32)).astype(BF16)


def _sb_logs_phase(z_scr, nsplit_scr, beta_scr, t, diag):
    for r in range(0, t, SB_STRIP):
        rows = pl.ds(r, SB_STRIP)
        z = z_scr[rows, :]
        e = jnp.exp(_neg_abs(z))
        nlog = jnp.maximum(z, 0.0) + jnp.log(1.0 + e)
        if beta_scr is not None:
            inv = pl.reciprocal(1.0 + e, approx=True)
            beta_scr[rows, :] = jnp.where(z >= 0.0, inv, e * inv)
        if diag:
            nlog = jnp.where(_sb_strip_mask(r, t), nlog, 0.0)
        _store_split(nsplit_scr, rows, nlog, t)


def _sb_probs(z_scr, tin_scr, rsum_scr, rows, r, t, diag):
    rs = rsum_scr[rows, :]
    a = jnp.exp(z_scr[rows, :] - tin_scr[rows, :] - rs)
    if diag:
        a = jnp.where(_sb_strip_mask(r, t), a, 0.0)
    rsum_scr[rows, :] = rs + tin_scr[rows, 0:1]
    return a


def _sb_triangle(tri_scr, t):
    row = lax.broadcasted_iota(jnp.int32, (t, t), 0)
    col = lax.broadcasted_iota(jnp.int32, (t, t), 1)
    tri = (row >= col).astype(BF16)
    tri_scr[0:t, :] = tri
    tri_scr[t:2 * t, :] = tri


def _sb_fwd(q, k, v, *, name):
    h, s, d = q.shape
    t = _blk(s, SB_TILE)
    scale = _sb_scale(d)

    def body(q_ref, k_ref, v_ref, o_ref, o32_ref, qs_scr, tri_scr, z_scr, nsplit_scr, tin_scr, a_scr, rsum_scr):
        qi = pl.program_id(1)
        qs_scr[...] = q_ref[...] * scale
        _sb_triangle(tri_scr, t)
        rsum_scr[...] = jnp.zeros_like(rsum_scr)
        o32_ref[...] = jnp.zeros_like(o32_ref)

        def tile(kb, diag):
            keys = pl.ds(pl.multiple_of(kb * t, t), t)
            for g in range(grp):
                z_scr[g] = lax.dot_general(qs_scr[g], k_ref[g, keys, :], _NT, preferred_element_type=F32)
            for g in range(grp):
                _sb_logs_phase(z_scr.at[g], nsplit_scr.at[g], None, t, diag)
                tin_scr[g] = jnp.dot(nsplit_scr[g], tri_scr[...], preferred_element_type=F32)
            for g in range(grp):
                for r in range(0, t, SB_STRIP):
                    rows = pl.ds(r, SB_STRIP)
                    a = _sb_probs(z_scr.at[g], tin_scr.at[g], rsum_scr.at[g], rows, r, t, diag)
                    a_scr[g, rows, :] = a.astype(BF16)
                o32_ref[g] += jnp.dot(a_scr[g], v_ref[g, keys, :], preferred_element_type=F32)

        tile(qi, True)

        def walk(n, carry):
            tile(qi - 1 - n, False)
            return carry

        lax.fori_loop(0, qi, walk, 0)
        o_ref[...] = o32_ref[...].astype(o_ref.dtype)

    grp = SB_HEAD_GROUP
    assert h % grp == 0
    qs = pl.BlockSpec((grp, t, d), lambda hh, i: (hh, i, 0))
    ks = pl.BlockSpec((grp, s, d), lambda hh, i: (hh, 0, 0))
    tile_f32 = pltpu.VMEM((grp, t, t), F32)
    col_f32 = pltpu.VMEM((grp, t, 1), F32)
    return pl.pallas_call(
        body, name=name, grid=(h // grp, s // t), in_specs=[qs, ks, ks], out_specs=[qs, qs],
        out_shape=[jax.ShapeDtypeStruct(q.shape, BF16), jax.ShapeDtypeStruct(q.shape, F32)],
        scratch_shapes=[pltpu.VMEM((grp, t, d), BF16), pltpu.VMEM((2 * t, t), BF16), tile_f32,
                        pltpu.VMEM((grp, t, 2 * t), BF16), tile_f32, pltpu.VMEM((grp, t, t), BF16), col_f32],
        compiler_params=_params(dimension_semantics=("parallel", "parallel")),
    )(q, k, v)


def _sb_bwd(q, k, v, o32, do, dk0, dv0, *, name):
    h, s, d = q.shape
    t = _blk(s, SB_TILE)
    scale = _sb_scale(d)
    has_init = dk0 is not None
    n_in = 7 if has_init else 5

    def body(*refs):
        if has_init:
            q_ref, k_ref, v_ref, o_ref, do_ref, dk0_ref, dv0_ref, dq_ref, dk_ref, dv_ref = refs[:n_in + 3]
        else:
            q_ref, k_ref, v_ref, o_ref, do_ref, dq_ref, dk_ref, dv_ref = refs[:n_in + 3]
        (qs_scr, tri_scr, z_scr, beta_scr, nsplit_scr, tin_scr, da_scr, a_scr, g_scr, gsplit_scr, gin_scr, dz_scr,
         dq_scr, rsum_scr, gsum_scr, dsum_scr) = refs[n_in + 3:]
        qi = pl.program_id(1)

        @pl.when(qi == 0)
        def _():
            if has_init:
                dk_ref[...] = dk0_ref[...]
                dv_ref[...] = dv0_ref[...]
            else:
                dk_ref[...] = jnp.zeros_like(dk_ref)
                dv_ref[...] = jnp.zeros_like(dv_ref)

        qs_scr[...] = q_ref[...] * scale
        _sb_triangle(tri_scr, t)
        dsum_scr[...] = jnp.sum(o_ref[...] * do_ref[...].astype(F32), axis=2, keepdims=True)
        rsum_scr[...] = jnp.zeros_like(rsum_scr)
        gsum_scr[...] = jnp.zeros_like(gsum_scr)
        dq_scr[...] = jnp.zeros_like(dq_scr)

        def tile(kb, diag):
            keys = pl.ds(pl.multiple_of(kb * t, t), t)
            for g in range(grp):
                z_scr[g] = lax.dot_general(qs_scr[g], k_ref[g, keys, :], _NT, preferred_element_type=F32)
                da_scr[g] = lax.dot_general(do_ref[g], v_ref[g, keys, :], _NT, preferred_element_type=F32)
            for g in range(grp):
                _sb_logs_phase(z_scr.at[g], nsplit_scr.at[g], beta_scr.at[g], t, diag)
                tin_scr[g] = jnp.dot(nsplit_scr[g], tri_scr[...], preferred_element_type=F32)
            for g in range(grp):
                for r in range(0, t, SB_STRIP):
                    rows = pl.ds(r, SB_STRIP)
                    ab = _sb_probs(z_scr.at[g], tin_scr.at[g], rsum_scr.at[g], rows, r, t, diag).astype(BF16)
                    a_scr[g, rows, :] = ab
                    gv = ab.astype(F32) * da_scr[g, rows, :]
                    g_scr[g, rows, :] = gv
                    _store_split(gsplit_scr.at[g], rows, gv, t)
                gin_scr[g] = jnp.dot(gsplit_scr[g], tri_scr[...], preferred_element_type=F32)
                dv_ref[g, keys, :] += lax.dot_general(a_scr[g], do_ref[g], _TN, preferred_element_type=F32)
            for g in range(grp):
                for r in range(0, t, SB_STRIP):
                    rows = pl.ds(r, SB_STRIP)
                    gs = gsum_scr[g, rows, :]
                    gv = g_scr[g, rows, :]
                    dz = gv - beta_scr[g, rows, :] * ((gv - gin_scr[g, rows, :]) + (dsum_scr[g, rows, :] - gs))
                    if diag:
                        dz = jnp.where(_sb_strip_mask(r, t), dz, 0.0)
                    dz_scr[g, rows, :] = dz.astype(BF16)
                    gsum_scr[g, rows, :] = gs + gin_scr[g, rows, 0:1]
                dq_scr[g] += jnp.dot(dz_scr[g], k_ref[g, keys, :], preferred_element_type=F32)
                dk_ref[g, keys, :] += lax.dot_general(dz_scr[g], qs_scr[g], _TN, preferred_element_type=F32)

        tile(qi, True)

        def walk(n, carry):
            tile(qi - 1 - n, False)
            return carry

        lax.fori_loop(0, qi, walk, 0)
        dq_ref[...] = (dq_scr[...] * scale).astype(dq_ref.dtype)

    grp = SB_BWD_HEAD_GROUP
    assert h % grp == 0
    qs = pl.BlockSpec((grp, t, d), lambda hh, i: (hh, i, 0))
    ks = pl.BlockSpec((grp, s, d), lambda hh, i: (hh, 0, 0))
    in_specs = [qs, ks, ks, qs, qs] + ([ks, ks] if has_init else [])
    operands = [q, k, v, o32, do] + ([dk0, dv0] if has_init else [])
    acc = jax.ShapeDtypeStruct(q.shape, F32)
    tile_f32 = pltpu.VMEM((grp, t, t), F32)
    tile_bf16 = pltpu.VMEM((grp, t, t), BF16)
    split = pltpu.VMEM((grp, t, 2 * t), BF16)
    col_f32 = pltpu.VMEM((grp, t, 1), F32)
    return pl.pallas_call(
        body, name=name, grid=(h // grp, s // t), in_specs=in_specs, out_specs=[qs, ks, ks],
        out_shape=[jax.ShapeDtypeStruct(q.shape, BF16), acc, acc],
        scratch_shapes=[pltpu.VMEM((grp, t, d), BF16), pltpu.VMEM((2 * t, t), BF16), tile_f32, tile_f32, split,
                        tile_f32, tile_f32, tile_bf16, tile_f32, split, tile_f32, tile_bf16,
                        pltpu.VMEM((grp, t, d), F32), col_f32, col_f32, col_f32],
        compiler_params=_params(dimension_semantics=("parallel", "arbitrary")),
    )(*operands)


def _position():
    x, y, c = lax.axis_index("x"), lax.axis_index("y"), lax.axis_index("c")
    return x, y, c, [(1 - x, y), (x, 1 - y), (1 - x, 1 - y)]


_ANY = pl.BlockSpec(memory_space=pl.ANY)
N_PEER_CHIPS = N_CHIPS - 1


def _all_gather_chips(shards, *, name):
    n = len(shards)

    def body(*refs):
        ins, outs = refs[:n], refs[n:2 * n]
        send_sems, recv_sems, local_sems = refs[2 * n:]
        x, y, c, peers = _position()
        me = 2 * x + y
        copies = []
        for a in range(n):
            copies.append(pltpu.make_async_copy(ins[a], outs[a].at[me], local_sems.at[a]))
            for j, (px, py) in enumerate(peers):
                copies.append(pltpu.make_async_remote_copy(
                    src_ref=ins[a], dst_ref=outs[a].at[me],
                    send_sem=send_sems.at[a * N_PEER_CHIPS + j], recv_sem=recv_sems.at[a * N_PEER_CHIPS + j],
                    device_id=(px, py, c), device_id_type=MESH))
        for cp in copies:
            cp.start()
        for cp in copies:
            cp.wait()

    return pl.pallas_call(
        body, name=name, in_specs=[_ANY] * n, out_specs=[_ANY] * n,
        out_shape=[jax.ShapeDtypeStruct((N_CHIPS, *s.shape), s.dtype) for s in shards],
        scratch_shapes=[pltpu.SemaphoreType.DMA((n * N_PEER_CHIPS,)), pltpu.SemaphoreType.DMA((n * N_PEER_CHIPS,)),
                        pltpu.SemaphoreType.DMA((n,))],
        compiler_params=pltpu.CompilerParams(has_side_effects=True),
    )(*shards)


def _scatter_to_chips(parts, *, name):
    n = len(parts)

    def body(*refs):
        ins, outs = refs[:n], refs[n:2 * n]
        send_sems, recv_sems, local_sems = refs[2 * n:]
        x, y, c, peers = _position()
        me = 2 * x + y
        copies = []
        for a in range(n):
            copies.append(pltpu.make_async_copy(ins[a].at[me], outs[a].at[me], local_sems.at[a]))
            for j, (px, py) in enumerate(peers):
                copies.append(pltpu.make_async_remote_copy(
                    src_ref=ins[a].at[2 * px + py], dst_ref=outs[a].at[me],
                    send_sem=send_sems.at[a * N_PEER_CHIPS + j], recv_sem=recv_sems.at[a * N_PEER_CHIPS + j],
                    device_id=(px, py, c), device_id_type=MESH))
        for cp in copies:
            cp.start()
        for cp in copies:
            cp.wait()

    return pl.pallas_call(
        body, name=name, in_specs=[_ANY] * n, out_specs=[_ANY] * n,
        out_shape=[jax.ShapeDtypeStruct(p.shape, p.dtype) for p in parts],
        scratch_shapes=[pltpu.SemaphoreType.DMA((n * N_PEER_CHIPS,)), pltpu.SemaphoreType.DMA((n * N_PEER_CHIPS,)),
                        pltpu.SemaphoreType.DMA((n,))],
        compiler_params=pltpu.CompilerParams(has_side_effects=True),
    )(*parts)


def _sibling_exchange(arrs, *, name):
    n = len(arrs)

    def body(*refs):
        ins, outs = refs[:n], refs[n:2 * n]
        send_sems, recv_sems = refs[2 * n:]
        x, y, c, _ = _position()
        copies = [pltpu.make_async_remote_copy(
            src_ref=ins[a], dst_ref=outs[a], send_sem=send_sems.at[a], recv_sem=recv_sems.at[a],
            device_id=(x, y, 1 - c), device_id_type=MESH) for a in range(n)]
        for cp in copies:
            cp.start()
        for cp in copies:
            cp.wait()

    return pl.pallas_call(
        body, name=name, in_specs=[_ANY] * n, out_specs=[_ANY] * n,
        out_shape=[jax.ShapeDtypeStruct(a.shape, a.dtype) for a in arrs],
        scratch_shapes=[pltpu.SemaphoreType.DMA((n,)), pltpu.SemaphoreType.DMA((n,))],
        compiler_params=pltpu.CompilerParams(has_side_effects=True),
    )(*arrs)


def _small_all_reduce(v, *, name):
    r, cdim = v.shape

    def body(v_ref, o_ref, slots, send_sems, recv_sems):
        x, y, c, _ = _position()
        me = 4 * x + 2 * y + c
        slots[me] = v_ref[...]
        copies = []
        for j in range(1, N_DEVICES):
            peer = (x ^ ((j >> 2) & 1), y ^ ((j >> 1) & 1), c ^ (j & 1))
            copies.append(pltpu.make_async_remote_copy(
                src_ref=v_ref, dst_ref=slots.at[me], send_sem=send_sems.at[j - 1], recv_sem=recv_sems.at[j - 1],
                device_id=peer, device_id_type=MESH))
        for cp in copies:
            cp.start()
        for cp in copies:
            cp.wait()
        acc = slots[0]
        for dev in range(1, N_DEVICES):
            acc = acc + slots[dev]
        o_ref[...] = acc

    vm = pl.BlockSpec(memory_space=pltpu.VMEM)
    return pl.pallas_call(
        body, name=name, in_specs=[vm], out_specs=vm,
        out_shape=jax.ShapeDtypeStruct(v.shape, F32),
        scratch_shapes=[pltpu.VMEM((N_DEVICES, r, cdim), F32),
                        pltpu.SemaphoreType.DMA((N_DEVICES - 1,)), pltpu.SemaphoreType.DMA((N_DEVICES - 1,))],
        compiler_params=pltpu.CompilerParams(has_side_effects=True),
    )(v)


def _sum_chips(parts, *, name):
    _, r, c = parts.shape
    bm = _blk(r, 256)

    def body(p_ref, o_ref):
        acc = p_ref[0].astype(F32)
        for kk in range(1, N_CHIPS):
            acc = acc + p_ref[kk].astype(F32)
        o_ref[...] = acc

    return pl.pallas_call(
        body, name=name, grid=(r // bm,),
        in_specs=[pl.BlockSpec((N_CHIPS, bm, c), lambda i: (0, i, 0))],
        out_specs=pl.BlockSpec((bm, c), lambda i: (i, 0)),
        out_shape=jax.ShapeDtypeStruct((r, c), F32),
        compiler_params=_params(dimension_semantics=("parallel",)),
    )(parts)


def _adamw(w, m, v, g_a, g_b, *, name):
    r, c = w.shape
    bm = _blk(r, 256)
    two = g_b is not None
    bc1 = 1.0 - ADAM_B1 ** ADAM_STEP
    bc2 = 1.0 - ADAM_B2 ** ADAM_STEP

    def body(*refs):
        if two:
            w_ref, m_ref, v_ref, ga_ref, gb_ref, g_ref, d_ref, nm_ref, nv_ref = refs
            g = ga_ref[...] + gb_ref[...]
        else:
            w_ref, m_ref, v_ref, ga_ref, g_ref, d_ref, nm_ref, nv_ref = refs
            g = ga_ref[...]
        nm = ADAM_B1 * m_ref[...] + (1.0 - ADAM_B1) * g
        nv = ADAM_B2 * v_ref[...] + (1.0 - ADAM_B2) * (g * g)
        g_ref[...] = g
        nm_ref[...] = nm
        nv_ref[...] = nv
        d_ref[...] = -ADAM_LR * ((nm / bc1) / (jnp.sqrt(nv / bc2) + ADAM_EPS) + ADAM_WD * w_ref[...])

    spec = pl.BlockSpec((bm, c), lambda i: (i, 0))
    out = jax.ShapeDtypeStruct((r, c), F32)
    operands = [w, m, v, g_a] + ([g_b] if two else [])
    return pl.pallas_call(
        body, name=name, grid=(r // bm,),
        in_specs=[spec] * len(operands), out_specs=[spec] * 4, out_shape=[out] * 4,
        compiler_params=_params(dimension_semantics=("parallel",)),
    )(*operands)


def _heads(a):
    s, w = a.shape
    return a.reshape(s, w // HEAD_DIM, HEAD_DIM).transpose(1, 0, 2)


def _unheads(a):
    h, s, d = a.shape
    return a.transpose(1, 0, 2).reshape(s, h * d)


def _pack_small(mix, ffn, kvn, memn, fin, conv):
    d = mix.shape[-1]
    flat = conv.reshape(-1)
    rows_conv = SMALL_ROWS - 11
    flat = jnp.pad(flat, (0, rows_conv * d - flat.shape[0]))
    return jnp.concatenate([mix, ffn, kvn.reshape(1, d), memn.reshape(1, d), fin.reshape(1, d),
                            flat.reshape(rows_conv, d)], axis=0)


def _unpack_small(buf, conv_shape):
    n = math.prod(conv_shape)
    return (buf[0:4], buf[4:8], buf[8], buf[9], buf[10], buf[11:].reshape(-1)[:n].reshape(conv_shape))


def kernel(x, mem, mix_norm, a_in, conv_w, b_in, kv_norm, w_kv_shared, w_mem_kv, w_o, ffn_norm, w_gate, w_up, w_down, mem_norm, final_norm, loss_target, m_mix_norm, m_a_in, m_conv_w, m_b_in, m_kv_norm, m_w_kv_shared, m_w_mem_kv, m_w_o, m_ffn_norm, m_w_gate, m_w_up, m_w_down, m_mem_norm, m_final_norm, v_mix_norm, v_a_in, v_conv_w, v_b_in, v_kv_norm, v_w_kv_shared, v_w_mem_kv, v_w_o, v_ffn_norm, v_w_gate, v_w_up, v_w_down, v_mem_norm, v_final_norm):
    s, d = x.shape[1], x.shape[2]
    n_mem = mem.shape[1]
    depth = mix_norm.shape[0]
    n_a = a_in.shape[0]
    main_w = conv_w.shape[2] * N_CHIPS
    mem_w = w_mem_kv.shape[2] // 2
    ffn_c = w_gate.shape[2]
    kv_c = w_kv_shared.shape[1]
    a_c = a_in.shape[2]
    chip = 2 * lax.axis_index("x") + lax.axis_index("y")

    x0 = x[0]
    mem0 = mem[0]
    tgt = loss_target[0]
    bs = _blk(s, 1024)

    w_kv3 = w_kv_shared[None]
    gathered = _all_gather_chips(
        [a_in.astype(BF16), b_in.astype(BF16), w_kv3.astype(BF16), w_mem_kv.astype(BF16), w_o.astype(BF16),
         w_gate.astype(BF16), w_up.astype(BF16), w_down.astype(BF16), conv_w], name="gather_weights")
    wa, wb, wkv, wm, wo, wg, wu, wd, conv_parts = gathered
    conv_full = jnp.concatenate([conv_parts[kk] for kk in range(N_CHIPS)], axis=-1)

    def gain(vec):
        return vec.reshape(1, d)

    mem_n = _rms_fwd(mem0, gain(mem_norm), name="mem_norm_fwd")
    saved = []
    k_sh = v_sh = hk = x_kv = None
    xc = x0
    for i in range(depth):
        st = {"x_in": xc}
        h = _rms_fwd(xc, gain(mix_norm[i]), name="mix_norm_fwd")
        mkv = _mm(Op(mem_n), Op(wm, 'r', i), name="mem_kv_proj", bm=n_mem, bn=2 * mem_w, bk=d,
                  out_dtype=BF16)
        mem_k, mem_v = _heads(mkv[:, :mem_w]), _heads(mkv[:, mem_w:])
        if i < n_a:
            p = _mm(Op(h), Op(wa, 'c', i), name="a_in_proj", bm=bs, bn=a_c, bk=d, out_dtype=BF16)
            y_main = _conv_fwd(p, conv_full[i], name="conv_fwd")
            q_mem = _heads(p[:, 3 * main_w:])
        else:
            p = _mm(Op(h), Op(wb, 'r', i - n_a), name="b_in_proj", bm=bs, bn=d, bk=d, out_dtype=BF16)
            q_sb = _heads(p[:, :main_w])
            o_sb, o_sb32 = _sb_fwd(q_sb, k_sh, v_sh, name="sb_fwd")
            y_main = _unheads(o_sb)
            q_mem = _heads(p[:, main_w:])
            st.update(q_sb=q_sb, o_sb32=o_sb32)
        y_mem = _mem_fwd(q_mem, mem_k, mem_v, name="mem_attn_fwd")
        y = jnp.concatenate([y_main, _unheads(y_mem)], axis=-1)
        x_mid = _mm(Op(y), Op(wo, 'r', i), name="w_o_proj", bm=bs, bn=d, bk=d, out_dtype=F32,
                    res=Op(xc))
        h2 = _rms_fwd(x_mid, gain(ffn_norm[i]), name="ffn_norm_fwd")
        gate = _mm(Op(h2), Op(wg, 'c', i), name="w_gate_proj", bm=bs, bn=ffn_c, bk=d, out_dtype=BF16,
                   out_chunk='c')
        up = _mm(Op(h2), Op(wu, 'c', i), name="w_up_proj", bm=bs, bn=ffn_c, bk=d, out_dtype=BF16,
                 out_chunk='c')
        act = _swiglu_fwd(gate, up, name="swiglu_fwd")
        xc = _mm(Op(act, 'c'), Op(wd, 'r', i), name="w_down_proj", bm=bs, bn=d, bk=ffn_c, out_dtype=F32,
                 res=Op(x_mid))
        st.update(h=h, p=p, mem_k=mem_k, mem_v=mem_v, q_mem=q_mem, y=y, x_mid=x_mid, h2=h2, gate=gate, up=up,
                  act=act)
        saved.append(st)
        if i == n_a - 1:
            x_kv = xc
            hk = _rms_fwd(xc, gain(kv_norm), name="kv_norm_fwd")
            kv = _mm(Op(hk), Op(wkv, 'c', 0), name="kv_proj", bm=bs, bn=kv_c, bk=d, out_dtype=BF16)
            k_sh, v_sh = _heads(kv[:, :main_w]), _heads(kv[:, main_w:])

    dx, dg_final, loss_part = _final_loss(xc, gain(final_norm), tgt, name="final_norm_loss")
    loss = lax.psum(loss_part[0, 0], ("x", "y", "c"))

    g_a, g_b, g_m, g_o, g_g, g_u, g_d = ([None] * n_a, [None] * (depth - n_a), [None] * depth, [None] * depth,
                                         [None] * depth, [None] * depth, [None] * depth)
    dg_mix, dg_ffn, dconv = [None] * depth, [None] * depth, [None] * n_a
    dk_sh = dv_sh = None
    dmem_n = None
    g_kv = dg_kv = None
    for i in reversed(range(depth)):
        st = saved[i]
        dact = _mm(Op(dx), Op(wd, 'r', i), name="w_down_dact", tb=True, bm=bs, bn=ffn_c, bk=d, out_dtype=BF16,
                   out_chunk='c')
        g_d[i] = _mm(Op(st["act"], 'c'), Op(dx), name="w_down_grad", ta=True, bm=ffn_c, bn=d, bk=bs,
                     out_dtype=BF16, out_chunk='r')
        dgate, dup = _swiglu_bwd(dact, st["gate"], st["up"], name="swiglu_bwd")
        g_g[i] = _mm(Op(st["h2"]), Op(dgate, 'c'), name="w_gate_grad", ta=True, bm=d, bn=ffn_c, bk=bs,
                     out_dtype=BF16, out_chunk='c')
        g_u[i] = _mm(Op(st["h2"]), Op(dup, 'c'), name="w_up_grad", ta=True, bm=d, bn=ffn_c, bk=bs,
                     out_dtype=BF16, out_chunk='c')
        dh2 = _mm(Op(dgate, 'c'), Op(wg, 'c', i), name="w_gate_dh", tb=True, bm=bs, bn=d, bk=ffn_c,
                  out_dtype=F32)
        dh2 = _mm(Op(dup, 'c'), Op(wu, 'c', i), name="w_up_dh", tb=True, bm=bs, bn=d, bk=ffn_c, out_dtype=F32,
                  res=Op(dh2))
        dx_mid, dg_ffn[i] = _rms_bwd(st["x_mid"], gain(ffn_norm[i]), dh2, dx, name="ffn_norm_bwd")
        dy = _mm(Op(dx_mid), Op(wo, 'r', i), name="w_o_dy", tb=True, bm=bs, bn=d, bk=d,
                 out_dtype=BF16)
        g_o[i] = _mm(Op(st["y"]), Op(dx_mid), name="w_o_grad", ta=True, bm=wo.shape[2], bn=d, bk=bs,
                     out_dtype=BF16, out_chunk='r')
        dq_mem, dmk, dmv = _mem_bwd(st["q_mem"], st["mem_k"], st["mem_v"], _heads(dy[:, main_w:]),
                                    name="mem_attn_bwd")
        dmkv = jnp.concatenate([_unheads(dmk), _unheads(dmv)], axis=-1)
        g_m[i] = _mm(Op(mem_n), Op(dmkv), name="mem_kv_grad", ta=True, bm=wm.shape[2], bn=2 * mem_w, bk=n_mem,
                     out_dtype=BF16, out_chunk='r')
        dmem_n = _mm(Op(dmkv), Op(wm, 'r', i), name="mem_kv_dmem", tb=True, bm=n_mem, bn=d,
                     bk=2 * mem_w, out_dtype=F32, res=None if dmem_n is None else Op(dmem_n))
        if i < n_a:
            db, dc, du, dconv[i] = _conv_bwd(st["p"], conv_full[i], dy[:, :main_w], name="conv_bwd")
            dp = jnp.concatenate([db, dc, du, _unheads(dq_mem)], axis=-1)
            g_a[i] = _mm(Op(st["h"]), Op(dp), name="a_in_grad", ta=True, bm=d, bn=a_c, bk=bs, out_dtype=BF16,
                         out_chunk='c')
            dh = _mm(Op(dp), Op(wa, 'c', i), name="a_in_dh", tb=True, bm=bs, bn=d, bk=a_c, out_dtype=F32)
        else:
            dq_sb, dk_sh, dv_sh = _sb_bwd(st["q_sb"], k_sh, v_sh, st["o_sb32"], _heads(dy[:, :main_w]),
                                          dk_sh, dv_sh, name="sb_bwd")
            dp = jnp.concatenate([_unheads(dq_sb), _unheads(dq_mem)], axis=-1)
            g_b[i - n_a] = _mm(Op(st["h"]), Op(dp), name="b_in_grad", ta=True, bm=wb.shape[2], bn=d, bk=bs,
                               out_dtype=BF16, out_chunk='r')
            dh = _mm(Op(dp), Op(wb, 'r', i - n_a), name="b_in_dh", tb=True, bm=bs, bn=d, bk=d,
                     out_dtype=F32)
        dx, dg_mix[i] = _rms_bwd(st["x_in"], gain(mix_norm[i]), dh, dx_mid, name="mix_norm_bwd")
        if i == n_a:
            dkv = jnp.concatenate([_unheads(dk_sh), _unheads(dv_sh)], axis=-1)
            g_kv = _mm(Op(hk), Op(dkv), name="kv_grad", ta=True, bm=d, bn=kv_c, bk=bs, out_dtype=BF16,
                       out_chunk='c')
            dhk = _mm(Op(dkv), Op(wkv, 'c', 0), name="kv_dh", tb=True, bm=bs, bn=d, bk=kv_c, out_dtype=F32)
            dx, dg_kv = _rms_bwd(x_kv, gain(kv_norm), dhk, dx, name="kv_norm_bwd")
    _, dg_mem = _rms_bwd(mem0, gain(mem_norm), dmem_n, None, name="mem_norm_bwd")

    big = [("a_in", a_in, m_a_in, v_a_in, jnp.stack(g_a, axis=1)),
           ("b_in", b_in, m_b_in, v_b_in, jnp.stack(g_b, axis=1)),
           ("w_kv_shared", w_kv3, m_w_kv_shared[None], v_w_kv_shared[None], g_kv[:, None]),
           ("w_mem_kv", w_mem_kv, m_w_mem_kv, v_w_mem_kv, jnp.stack(g_m, axis=1)),
           ("w_o", w_o, m_w_o, v_w_o, jnp.stack(g_o, axis=1)),
           ("w_gate", w_gate, m_w_gate, v_w_gate, jnp.stack(g_g, axis=1)),
           ("w_up", w_up, m_w_up, v_w_up, jnp.stack(g_u, axis=1)),
           ("w_down", w_down, m_w_down, v_w_down, jnp.stack(g_d, axis=1))]
    landed = _scatter_to_chips([b[4] for b in big], name="scatter_weight_grads")
    core_sums = [_sum_chips(p.reshape(N_CHIPS, -1, p.shape[-1]), name="sum_chip_partials") for p in landed]
    sibling_sums = _sibling_exchange(core_sums, name="exchange_core_sums")
    results = {}
    for (wname, w, mm_, vv_, _), own, sib in zip(big, core_sums, sibling_sums):
        flat = lambda t: t.reshape(-1, t.shape[-1])
        outs = _adamw(flat(w), flat(mm_), flat(vv_), own, sib, name="adamw")
        results[wname] = [o.reshape(w.shape[1:] if wname == "w_kv_shared" else w.shape) for o in outs]

    small_g = _pack_small(jnp.concatenate(dg_mix, axis=0), jnp.concatenate(dg_ffn, axis=0), dg_kv, dg_mem,
                          dg_final, jnp.stack(dconv, axis=0))
    small_g = _small_all_reduce(small_g, name="all_reduce_small_grads")
    conv_shape_full = (n_a, CONV_TAPS, main_w)
    gs = list(_unpack_small(small_g, conv_shape_full))
    gs[5] = lax.dynamic_slice_in_dim(gs[5], chip * conv_w.shape[2], conv_w.shape[2], axis=2)
    small_outs = _adamw(_pack_small(mix_norm, ffn_norm, kv_norm, mem_norm, final_norm, conv_w),
                        _pack_small(m_mix_norm, m_ffn_norm, m_kv_norm, m_mem_norm, m_final_norm, m_conv_w),
                        _pack_small(v_mix_norm, v_ffn_norm, v_kv_norm, v_mem_norm, v_final_norm, v_conv_w),
                        _pack_small(*gs), None, name="adamw_small")
    small_names = ["mix_norm", "ffn_norm", "kv_norm", "mem_norm", "final_norm", "conv_w"]
    for kind, buf in enumerate(small_outs):
        for wname, val in zip(small_names, _unpack_small(buf, conv_w.shape)):
            results.setdefault(wname, [None] * 4)[kind] = val

    order = ["mix_norm", "a_in", "conv_w", "b_in", "kv_norm", "w_kv_shared", "w_mem_kv", "w_o", "ffn_norm",
             "w_gate", "w_up", "w_down", "mem_norm", "final_norm"]
    return (loss, dx[None], *[results[nm][0] for nm in order], *[results[nm][1] for nm in order],
            *[results[nm][2] for nm in order], *[results[nm][3] for nm in order])
```

```python
import math
from typing import NamedTuple, Optional

import jax
import jax.numpy as jnp
from jax import lax
from jax.experimental import pallas as pl
from jax.experimental.pallas import tpu as pltpu

F32 = jnp.float32
BF16 = jnp.bfloat16
MESH = pl.DeviceIdType.MESH

N_CHIPS = 4
N_DEVICES = 8
HEAD_DIM = 64
CONV_TAPS = 3
NORM_EPS = 1e-6
V7X_VMEM_BYTES = 64 * 1024 * 1024
VMEM_LIMIT = V7X_VMEM_BYTES - 8 * 1024 * 1024
LANES = 128
SMALL_ROWS = 16

ADAM_LR = 0.001
ADAM_B1 = 0.9
ADAM_B2 = 0.999
ADAM_EPS = 1e-08
ADAM_WD = 0.01
ADAM_STEP = 10


def _params(**kw):
    return pltpu.CompilerParams(vmem_limit_bytes=VMEM_LIMIT, **kw)


def _blk(n, pref):
    b = min(n, pref)
    assert n % b == 0, (n, pref)
    return b


class Op(NamedTuple):
    arr: jax.Array
    chunk: Optional[str] = None
    layer: Optional[int] = None


def _op_spec(op_chunk, op_layer, shape2, br, bc, pick):
    r, c = shape2
    lead = () if op_layer is None else (op_layer,)
    none = (None,) * len(lead)
    if op_chunk is None:
        def imap(i, j, k):
            rb, cb = pick(i, j, k)
            return (*lead, rb, cb)
        return pl.BlockSpec((*none, br, bc), imap)
    if op_chunk == 'r' and br == N_CHIPS * r:
        def imap(i, j, k):
            rb, cb = pick(i, j, k)
            return (0, *lead, 0, cb)
        return pl.BlockSpec((N_CHIPS, *none, r, bc), imap)
    if op_chunk == 'r':
        n = r // br
        assert r % br == 0

        def imap(i, j, k):
            rb, cb = pick(i, j, k)
            return (rb // n, *lead, rb % n, cb)
        return pl.BlockSpec((None, *none, br, bc), imap)
    n = c // bc
    assert c % bc == 0

    def imap(i, j, k):
        rb, cb = pick(i, j, k)
        return (cb // n, *lead, rb, cb % n)
    return pl.BlockSpec((None, *none, br, bc), imap)


def _mm(a, b, *, name, ta=False, tb=False, bm, bn, bk, out_dtype, out_chunk=None, res=None):
    def dims(op):
        r, c = op.arr.shape[-2:]
        return (r * N_CHIPS if op.chunk == 'r' else r, c * N_CHIPS if op.chunk == 'c' else c)

    ar, ac = dims(a)
    br_, bc_ = dims(b)
    m, ka = (ac, ar) if ta else (ar, ac)
    kb, n = (bc_, br_) if tb else (br_, bc_)
    assert ka == kb, (name, ka, kb)
    assert m % bm == 0 and n % bn == 0 and ka % bk == 0, (name, m, n, ka, bm, bn, bk)
    nk = ka // bk

    if ta:
        a_spec = _op_spec(a.chunk, a.layer, a.arr.shape[-2:], bk, bm, lambda i, j, k: (k, i))
    else:
        a_spec = _op_spec(a.chunk, a.layer, a.arr.shape[-2:], bm, bk, lambda i, j, k: (i, k))
    if tb:
        b_spec = _op_spec(b.chunk, b.layer, b.arr.shape[-2:], bn, bk, lambda i, j, k: (j, k))
    else:
        b_spec = _op_spec(b.chunk, b.layer, b.arr.shape[-2:], bk, bn, lambda i, j, k: (k, j))

    if out_chunk == 'r':
        out_shape2 = (m // N_CHIPS, n)
    elif out_chunk == 'c':
        out_shape2 = (m, n // N_CHIPS)
    else:
        out_shape2 = (m, n)
    o_spec = _op_spec(out_chunk, None, out_shape2, bm, bn, lambda i, j, k: (i, j))
    out_full = out_shape2 if out_chunk is None else (N_CHIPS, *out_shape2)

    contract = (((0 if ta else 1,), (1 if tb else 0,)), ((), ()))
    has_res = res is not None

    def block2(ref):
        v = ref[...]
        return v.reshape(-1, v.shape[-1]).astype(BF16)

    def body(*refs):
        r_ref = refs[2] if has_res else None
        a_ref, b_ref = refs[:2]
        o_ref = refs[3 if has_res else 2]
        prod = lax.dot_general(block2(a_ref), block2(b_ref), contract, preferred_element_type=F32)
        if nk == 1:
            if has_res:
                prod = prod + r_ref[...].astype(F32)
            o_ref[...] = prod.astype(o_ref.dtype)
            return
        acc_ref = refs[-1]
        k = pl.program_id(2)

        @pl.when(k == 0)
        def _():
            acc_ref[...] = prod

        @pl.when(k > 0)
        def _():
            acc_ref[...] += prod

        @pl.when(k == nk - 1)
        def _():
            acc = acc_ref[...]
            if has_res:
                acc = acc + r_ref[...].astype(F32)
            o_ref[...] = acc.astype(o_ref.dtype)

    in_specs = [a_spec, b_spec]
    operands = [a.arr, b.arr]
    if has_res:
        in_specs.append(_op_spec(res.chunk, res.layer, res.arr.shape[-2:], bm, bn, lambda i, j, k: (i, j)))
        operands.append(res.arr)
    return pl.pallas_call(
        body, name=name, grid=(m // bm, n // bn, nk),
        in_specs=in_specs, out_specs=o_spec,
        out_shape=jax.ShapeDtypeStruct(out_full, out_dtype),
        scratch_shapes=[pltpu.VMEM((bm, bn), F32)] if nk > 1 else [],
        compiler_params=_params(dimension_semantics=("parallel", "parallel", "arbitrary")),
    )(*operands)


def _rms_fwd(x, g, *, name, dep=None):
    r, d = x.shape
    bm = _blk(r, 512)

    def body(x_ref, g_ref, *rest):
        o_ref = rest[-1]
        xv = x_ref[...]
        rstd = lax.rsqrt(jnp.mean(xv * xv, axis=-1, keepdims=True) + NORM_EPS)
        o_ref[...] = ((xv * rstd) * g_ref[...]).astype(o_ref.dtype)

    deps = [] if dep is None else [dep]
    return pl.pallas_call(
        body, name=name, grid=(r // bm,),
        in_specs=[pl.BlockSpec((bm, d), lambda i: (i, 0)), pl.BlockSpec((1, d), lambda i: (0, 0))]
        + [pl.BlockSpec(memory_space=pl.ANY)] * len(deps),
        out_specs=pl.BlockSpec((bm, d), lambda i: (i, 0)),
        out_shape=jax.ShapeDtypeStruct((r, d), BF16),
        compiler_params=_params(dimension_semantics=("parallel",)),
    )(x, g, *deps)


def _rms_bwd(x, g, dh, dres, *, name, dep=None):
    r, d = x.shape
    bm = _blk(r, 512)
    has_res = dres is not None
    deps = [] if dep is None else [dep]

    def body(*refs):
        if has_res:
            x_ref, g_ref, dh_ref, dres_ref = refs[:4]
        else:
            x_ref, g_ref, dh_ref = refs[:3]
        dx_ref, dg_ref = refs[-2:]

        @pl.when(pl.program_id(0) == 0)
        def _():
            dg_ref[...] = jnp.zeros_like(dg_ref)

        xv = x_ref[...]
        rstd = lax.rsqrt(jnp.mean(xv * xv, axis=-1, keepdims=True) + NORM_EPS)
        xh = xv * rstd
        dhv = dh_ref[...].astype(F32)
        dg_ref[...] += jnp.sum(dhv * xh, axis=0, keepdims=True)
        dxh = dhv * g_ref[...]
        dx = rstd * (dxh - xh * jnp.mean(dxh * xh, axis=-1, keepdims=True))
        if has_res:
            dx = dres_ref[...] + dx
        dx_ref[...] = dx

    row = pl.BlockSpec((bm, d), lambda i: (i, 0))
    vec = pl.BlockSpec((1, d), lambda i: (0, 0))
    in_specs = [row, vec, row] + ([row] if has_res else []) + [pl.BlockSpec(memory_space=pl.ANY)] * len(deps)
    operands = [x, g, dh] + ([dres] if has_res else []) + deps
    return pl.pallas_call(
        body, name=name, grid=(r // bm,),
        in_specs=in_specs, out_specs=[row, vec],
        out_shape=[jax.ShapeDtypeStruct((r, d), F32), jax.ShapeDtypeStruct((1, d), F32)],
        compiler_params=_params(dimension_semantics=("arbitrary",)),
    )(*operands)


def _final_loss(x, g, tgt, *, name):
    r, d = x.shape
    bm = _blk(r, 512)

    def body(x_ref, g_ref, t_ref, dx_ref, dg_ref, loss_ref):
        @pl.when(pl.program_id(0) == 0)
        def _():
            dg_ref[...] = jnp.zeros_like(dg_ref)
            loss_ref[...] = jnp.zeros_like(loss_ref)

        xv = x_ref[...]
        gv = g_ref[...]
        rstd = lax.rsqrt(jnp.mean(xv * xv, axis=-1, keepdims=True) + NORM_EPS)
        xh = xv * rstd
        diff = xh * gv - t_ref[...]
        loss_ref[...] += jnp.sum(diff * diff) * (0.5 / d)
        dy = diff * (1.0 / d)
        dg_ref[...] += jnp.sum(dy * xh, axis=0, keepdims=True)
        dxh = dy * gv
        dx_ref[...] = rstd * (dxh - xh * jnp.mean(dxh * xh, axis=-1, keepdims=True))

    row = pl.BlockSpec((bm, d), lambda i: (i, 0))
    vec = pl.BlockSpec((1, d), lambda i: (0, 0))
    return pl.pallas_call(
        body, name=name, grid=(r // bm,),
        in_specs=[row, vec, row],
        out_specs=[row, vec, pl.BlockSpec((1, LANES), lambda i: (0, 0))],
        out_shape=[jax.ShapeDtypeStruct((r, d), F32), jax.ShapeDtypeStruct((1, d), F32),
                   jax.ShapeDtypeStruct((1, LANES), F32)],
        compiler_params=_params(dimension_semantics=("arbitrary",)),
    )(x, g, tgt)


def _shift_down(v, k, row):
    return jnp.where(row >= k, pltpu.roll(v, k, 0), 0.0)


def _shift_up(v, k, row, s):
    return jnp.where(row < s - k, pltpu.roll(v, s - k, 0), 0.0)


def _conv_fwd(p, w, *, name):
    s = p.shape[0]
    width = w.shape[1]
    nb = width // LANES

    def body(b_ref, c_ref, u_ref, w_ref, y_ref):
        cu = c_ref[...].astype(F32) * u_ref[...].astype(F32)
        row = lax.broadcasted_iota(jnp.int32, cu.shape, 0)
        wv = w_ref[...]
        conv = wv[2:3] * cu + wv[1:2] * _shift_down(cu, 1, row) + wv[0:1] * _shift_down(cu, 2, row)
        y_ref[...] = (b_ref[...].astype(F32) * conv).astype(y_ref.dtype)

    def col(o):
        return pl.BlockSpec((s, LANES), lambda j: (0, j + o * nb))

    return pl.pallas_call(
        body, name=name, grid=(nb,),
        in_specs=[col(0), col(1), col(2), pl.BlockSpec((CONV_TAPS, LANES), lambda j: (0, j))],
        out_specs=col(0),
        out_shape=jax.ShapeDtypeStruct((s, width), BF16),
        compiler_params=_params(dimension_semantics=("parallel",)),
    )(p, p, p, w)


def _conv_bwd(p, w, dy, *, name):
    s = p.shape[0]
    width = w.shape[1]
    nb = width // LANES

    def body(b_ref, c_ref, u_ref, w_ref, dy_ref, db_ref, dc_ref, du_ref, dw_ref):
        bv = b_ref[...].astype(F32)
        cv = c_ref[...].astype(F32)
        uv = u_ref[...].astype(F32)
        dyv = dy_ref[...].astype(F32)
        cu = cv * uv
        row = lax.broadcasted_iota(jnp.int32, cu.shape, 0)
        wv = w_ref[...]
        cu1 = _shift_down(cu, 1, row)
        cu2 = _shift_down(cu, 2, row)
        conv = wv[2:3] * cu + wv[1:2] * cu1 + wv[0:1] * cu2
        db_ref[...] = (dyv * conv).astype(db_ref.dtype)
        dconv = dyv * bv
        dcu = wv[2:3] * dconv + wv[1:2] * _shift_up(dconv, 1, row, s) + wv[0:1] * _shift_up(dconv, 2, row, s)
        dc_ref[...] = (dcu * uv).astype(dc_ref.dtype)
        du_ref[...] = (dcu * cv).astype(du_ref.dtype)
        dw_ref[0:1, :] = jnp.sum(dconv * cu2, axis=0, keepdims=True)
        dw_ref[1:2, :] = jnp.sum(dconv * cu1, axis=0, keepdims=True)
        dw_ref[2:3, :] = jnp.sum(dconv * cu, axis=0, keepdims=True)

    def col(o):
        return pl.BlockSpec((s, LANES), lambda j: (0, j + o * nb))

    wspec = pl.BlockSpec((CONV_TAPS, LANES), lambda j: (0, j))
    act = jax.ShapeDtypeStruct((s, width), BF16)
    return pl.pallas_call(
        body, name=name, grid=(nb,),
        in_specs=[col(0), col(1), col(2), wspec, col(0)],
        out_specs=[col(0), col(0), col(0), wspec],
        out_shape=[act, act, act, jax.ShapeDtypeStruct((CONV_TAPS, width), F32)],
        compiler_params=_params(dimension_semantics=("parallel",)),
    )(p, p, p, w, dy)


def _swiglu_fwd(g, u, *, name):
    nc, s, f = g.shape
    bm = _blk(s, 512)

    def body(g_ref, u_ref, o_ref):
        gv = g_ref[...].astype(F32)
        o_ref[...] = (gv * jax.nn.sigmoid(gv) * u_ref[...].astype(F32)).astype(o_ref.dtype)

    spec = pl.BlockSpec((None, bm, f), lambda c, i: (c, i, 0))
    return pl.pallas_call(
        body, name=name, grid=(nc, s // bm), in_specs=[spec, spec], out_specs=spec,
        out_shape=jax.ShapeDtypeStruct(g.shape, BF16),
        compiler_params=_params(dimension_semantics=("parallel", "parallel")),
    )(g, u)


def _swiglu_bwd(dact, g, u, *, name):
    nc, s, f = g.shape
    bm = _blk(s, 512)

    def body(d_ref, g_ref, u_ref, dg_ref, du_ref):
        gv = g_ref[...].astype(F32)
        dv = d_ref[...].astype(F32)
        sg = jax.nn.sigmoid(gv)
        silu = gv * sg
        dg_ref[...] = (dv * u_ref[...].astype(F32) * (sg * (1.0 + gv * (1.0 - sg)))).astype(dg_ref.dtype)
        du_ref[...] = (dv * silu).astype(du_ref.dtype)

    spec = pl.BlockSpec((None, bm, f), lambda c, i: (c, i, 0))
    out = jax.ShapeDtypeStruct(g.shape, BF16)
    return pl.pallas_call(
        body, name=name, grid=(nc, s // bm), in_specs=[spec, spec, spec], out_specs=[spec, spec],
        out_shape=[out, out],
        compiler_params=_params(dimension_semantics=("parallel", "parallel")),
    )(dact, g, u)


_NT = (((1,), (1,)), ((), ()))
_TN = (((0,), (0,)), ((), ()))


def _mem_probs(q, k, scale):
    s = lax.dot_general(q, k, _NT, preferred_element_type=F32) * scale
    e = jnp.exp(s - jnp.max(s, axis=-1, keepdims=True))
    return e / jnp.sum(e, axis=-1, keepdims=True)


def _mem_fwd(q, k, v, *, name):
    h, s, d = q.shape
    m = k.shape[1]
    bq = _blk(s, 1024)
    scale = 1.0 / math.sqrt(d)

    def body(q_ref, k_ref, v_ref, o_ref):
        p = _mem_probs(q_ref[...], k_ref[...], scale)
        o_ref[...] = jnp.dot(p.astype(BF16), v_ref[...], preferred_element_type=F32).astype(o_ref.dtype)

    qs = pl.BlockSpec((None, bq, d), lambda hh, i: (hh, i, 0))
    ks = pl.BlockSpec((None, m, d), lambda hh, i: (hh, 0, 0))
    return pl.pallas_call(
        body, name=name, grid=(h, s // bq), in_specs=[qs, ks, ks], out_specs=qs,
        out_shape=jax.ShapeDtypeStruct(q.shape, BF16),
        compiler_params=_params(dimension_semantics=("parallel", "parallel")),
    )(q, k, v)


def _mem_bwd(q, k, v, do, *, name):
    h, s, d = q.shape
    m = k.shape[1]
    bq = _blk(s, 1024)
    scale = 1.0 / math.sqrt(d)

    def body(q_ref, k_ref, v_ref, do_ref, dq_ref, dk_ref, dv_ref):
        @pl.when(pl.program_id(1) == 0)
        def _():
            dk_ref[...] = jnp.zeros_like(dk_ref)
            dv_ref[...] = jnp.zeros_like(dv_ref)

        qv = q_ref[...]
        kv = k_ref[...]
        dov = do_ref[...]
        p = _mem_probs(qv, kv, scale)
        pb = p.astype(BF16)
        dp = lax.dot_general(dov, v_ref[...], _NT, preferred_element_type=F32)
        pf = pb.astype(F32)
        ds = (pf * (dp - jnp.sum(pf * dp, axis=-1, keepdims=True)) * scale).astype(BF16)
        dq_ref[...] = jnp.dot(ds, kv, preferred_element_type=F32).astype(dq_ref.dtype)
        dk_ref[...] += lax.dot_general(ds, qv, _TN, preferred_element_type=F32)
        dv_ref[...] += lax.dot_general(pb, dov, _TN, preferred_element_type=F32)

    qs = pl.BlockSpec((None, bq, d), lambda hh, i: (hh, i, 0))
    ks = pl.BlockSpec((None, m, d), lambda hh, i: (hh, 0, 0))
    kvout = jax.ShapeDtypeStruct(k.shape, F32)
    return pl.pallas_call(
        body, name=name, grid=(h, s // bq), in_specs=[qs, ks, ks, qs], out_specs=[qs, ks, ks],
        out_shape=[jax.ShapeDtypeStruct(q.shape, BF16), kvout, kvout],
        compiler_params=_params(dimension_semantics=("parallel", "arbitrary")),
    )(q, k, v, do)


SB_TILE = 256
SB_STRIP = 32
SB_HEAD_GROUP = 4
SB_BWD_HEAD_GROUP = 2


def _sb_scale(d):
    scale = 1.0 / math.sqrt(d)
    assert math.frexp(scale)[0] == 0.5, "the scale is folded into bf16 q, exact only for a power of two"
    return scale


def _sb_strip_mask(r, t):
    rr = r + lax.broadcasted_iota(jnp.int32, (SB_STRIP, t), 0)
    return lax.broadcasted_iota(jnp.int32, (SB_STRIP, t), 1) < rr


def _neg_abs(z):
    bits = lax.bitcast_convert_type(z, jnp.uint32) | jnp.uint32(0x80000000)
    return lax.bitcast_convert_type(bits, F32)


def _store_split(split_scr, rows, val, t):
    hi = val.astype(BF16)
    split_scr[rows, 0:t] = hi
    split_scr[rows, t:2 * t] = (val - hi.astype(F32)).astype(BF16)


def _sb_logs_phase(z_scr, nsplit_scr, beta_scr, t, diag):
    for r in range(0, t, SB_STRIP):
        rows = pl.ds(r, SB_STRIP)
        z = z_scr[rows, :]
        e = jnp.exp(_neg_abs(z))
        nlog = jnp.maximum(z, 0.0) + jnp.log(1.0 + e)
        if beta_scr is not None:
            inv = pl.reciprocal(1.0 + e, approx=True)
            beta_scr[rows, :] = jnp.where(z >= 0.0, inv, e * inv)
        if diag:
            nlog = jnp.where(_sb_strip_mask(r, t), nlog, 0.0)
        _store_split(nsplit_scr, rows, nlog, t)


def _sb_probs(z_scr, tin_scr, rsum_scr, rows, r, t, diag):
    rs = rsum_scr[rows, :]
    a = jnp.exp(z_scr[rows, :] - tin_scr[rows, :] - rs)
    if diag:
        a = jnp.where(_sb_strip_mask(r, t), a, 0.0)
    rsum_scr[rows, :] = rs + tin_scr[rows, 0:1]
    return a


def _sb_triangle(tri_scr, t):
    row = lax.broadcasted_iota(jnp.int32, (t, t), 0)
    col = lax.broadcasted_iota(jnp.int32, (t, t), 1)
    tri = (row >= col).astype(BF16)
    tri_scr[0:t, :] = tri
    tri_scr[t:2 * t, :] = tri


def _sb_fwd(q, k, v, *, name):
    h, s, d = q.shape
    t = _blk(s, SB_TILE)
    scale = _sb_scale(d)

    def body(q_ref, k_ref, v_ref, o_ref, o32_ref, qs_scr, tri_scr, z_scr, nsplit_scr, tin_scr, a_scr, rsum_scr):
        qi = pl.program_id(1)
        qs_scr[...] = q_ref[...] * scale
        _sb_triangle(tri_scr, t)
        rsum_scr[...] = jnp.zeros_like(rsum_scr)
        o32_ref[...] = jnp.zeros_like(o32_ref)

        def tile(kb, diag):
            keys = pl.ds(pl.multiple_of(kb * t, t), t)
            for g in range(grp):
                z_scr[g] = lax.dot_general(qs_scr[g], k_ref[g, keys, :], _NT, preferred_element_type=F32)
            for g in range(grp):
                _sb_logs_phase(z_scr.at[g], nsplit_scr.at[g], None, t, diag)
                tin_scr[g] = jnp.dot(nsplit_scr[g], tri_scr[...], preferred_element_type=F32)
            for g in range(grp):
                for r in range(0, t, SB_STRIP):
                    rows = pl.ds(r, SB_STRIP)
                    a = _sb_probs(z_scr.at[g], tin_scr.at[g], rsum_scr.at[g], rows, r, t, diag)
                    a_scr[g, rows, :] = a.astype(BF16)
                o32_ref[g] += jnp.dot(a_scr[g], v_ref[g, keys, :], preferred_element_type=F32)

        tile(qi, True)

        def walk(n, carry):
            tile(qi - 1 - n, False)
            return carry

        lax.fori_loop(0, qi, walk, 0)
        o_ref[...] = o32_ref[...].astype(o_ref.dtype)

    grp = SB_HEAD_GROUP
    assert h % grp == 0
    qs = pl.BlockSpec((grp, t, d), lambda hh, i: (hh, i, 0))
    ks = pl.BlockSpec((grp, s, d), lambda hh, i: (hh, 0, 0))
    tile_f32 = pltpu.VMEM((grp, t, t), F32)
    col_f32 = pltpu.VMEM((grp, t, 1), F32)
    return pl.pallas_call(
        body, name=name, grid=(h // grp, s // t), in_specs=[qs, ks, ks], out_specs=[qs, qs],
        out_shape=[jax.ShapeDtypeStruct(q.shape, BF16), jax.ShapeDtypeStruct(q.shape, F32)],
        scratch_shapes=[pltpu.VMEM((grp, t, d), BF16), pltpu.VMEM((2 * t, t), BF16), tile_f32,
                        pltpu.VMEM((grp, t, 2 * t), BF16), tile_f32, pltpu.VMEM((grp, t, t), BF16), col_f32],
        compiler_params=_params(dimension_semantics=("parallel", "parallel")),
    )(q, k, v)


def _sb_bwd(q, k, v, o32, do, dk0, dv0, *, name):
    h, s, d = q.shape
    t = _blk(s, SB_TILE)
    scale = _sb_scale(d)
    has_init = dk0 is not None
    n_in = 7 if has_init else 5

    def body(*refs):
        if has_init:
            q_ref, k_ref, v_ref, o_ref, do_ref, dk0_ref, dv0_ref, dq_ref, dk_ref, dv_ref = refs[:n_in + 3]
        else:
            q_ref, k_ref, v_ref, o_ref, do_ref, dq_ref, dk_ref, dv_ref = refs[:n_in + 3]
        (qs_scr, tri_scr, z_scr, beta_scr, nsplit_scr, tin_scr, da_scr, a_scr, g_scr, gsplit_scr, gin_scr, dz_scr,
         dq_scr, rsum_scr, gsum_scr, dsum_scr) = refs[n_in + 3:]
        qi = pl.program_id(1)

        @pl.when(qi == 0)
        def _():
            if has_init:
                dk_ref[...] = dk0_ref[...]
                dv_ref[...] = dv0_ref[...]
            else:
                dk_ref[...] = jnp.zeros_like(dk_ref)
                dv_ref[...] = jnp.zeros_like(dv_ref)

        qs_scr[...] = q_ref[...] * scale
        _sb_triangle(tri_scr, t)
        dsum_scr[...] = jnp.sum(o_ref[...] * do_ref[...].astype(F32), axis=2, keepdims=True)
        rsum_scr[...] = jnp.zeros_like(rsum_scr)
        gsum_scr[...] = jnp.zeros_like(gsum_scr)
        dq_scr[...] = jnp.zeros_like(dq_scr)

        def tile(kb, diag):
            keys = pl.ds(pl.multiple_of(kb * t, t), t)
            for g in range(grp):
                z_scr[g] = lax.dot_general(qs_scr[g], k_ref[g, keys, :], _NT, preferred_element_type=F32)
                da_scr[g] = lax.dot_general(do_ref[g], v_ref[g, keys, :], _NT, preferred_element_type=F32)
            for g in range(grp):
                _sb_logs_phase(z_scr.at[g], nsplit_scr.at[g], beta_scr.at[g], t, diag)
                tin_scr[g] = jnp.dot(nsplit_scr[g], tri_scr[...], preferred_element_type=F32)
            for g in range(grp):
                for r in range(0, t, SB_STRIP):
                    rows = pl.ds(r, SB_STRIP)
                    ab = _sb_probs(z_scr.at[g], tin_scr.at[g], rsum_scr.at[g], rows, r, t, diag).astype(BF16)
                    a_scr[g, rows, :] = ab
                    gv = ab.astype(F32) * da_scr[g, rows, :]
                    g_scr[g, rows, :] = gv
                    _store_split(gsplit_scr.at[g], rows, gv, t)
                gin_scr[g] = jnp.dot(gsplit_scr[g], tri_scr[...], preferred_element_type=F32)
                dv_ref[g, keys, :] += lax.dot_general(a_scr[g], do_ref[g], _TN, preferred_element_type=F32)
            for g in range(grp):
                for r in range(0, t, SB_STRIP):
                    rows = pl.ds(r, SB_STRIP)
                    gs = gsum_scr[g, rows, :]
                    gv = g_scr[g, rows, :]
                    dz = gv - beta_scr[g, rows, :] * ((gv - gin_scr[g, rows, :]) + (dsum_scr[g, rows, :] - gs))
                    if diag:
                        dz = jnp.where(_sb_strip_mask(r, t), dz, 0.0)
                    dz_scr[g, rows, :] = dz.astype(BF16)
                    gsum_scr[g, rows, :] = gs + gin_scr[g, rows, 0:1]
                dq_scr[g] += jnp.dot(dz_scr[g], k_ref[g, keys, :], preferred_element_type=F32)
                dk_ref[g, keys, :] += lax.dot_general(dz_scr[g], qs_scr[g], _TN, preferred_element_type=F32)

        tile(qi, True)

        def walk(n, carry):
            tile(qi - 1 - n, False)
            return carry

        lax.fori_loop(0, qi, walk, 0)
        dq_ref[...] = (dq_scr[...] * scale).astype(dq_ref.dtype)

    grp = SB_BWD_HEAD_GROUP
    assert h % grp == 0
    qs = pl.BlockSpec((grp, t, d), lambda hh, i: (hh, i, 0))
    ks = pl.BlockSpec((grp, s, d), lambda hh, i: (hh, 0, 0))
    in_specs = [qs, ks, ks, qs, qs] + ([ks, ks] if has_init else [])
    operands = [q, k, v, o32, do] + ([dk0, dv0] if has_init else [])
    acc = jax.ShapeDtypeStruct(q.shape, F32)
    tile_f32 = pltpu.VMEM((grp, t, t), F32)
    tile_bf16 = pltpu.VMEM((grp, t, t), BF16)
    split = pltpu.VMEM((grp, t, 2 * t), BF16)
    col_f32 = pltpu.VMEM((grp, t, 1), F32)
    return pl.pallas_call(
        body, name=name, grid=(h // grp, s // t), in_specs=in_specs, out_specs=[qs, ks, ks],
        out_shape=[jax.ShapeDtypeStruct(q.shape, BF16), acc, acc],
        scratch_shapes=[pltpu.VMEM((grp, t, d), BF16), pltpu.VMEM((2 * t, t), BF16), tile_f32, tile_f32, split,
                        tile_f32, tile_f32, tile_bf16, tile_f32, split, tile_f32, tile_bf16,
                        pltpu.VMEM((grp, t, d), F32), col_f32, col_f32, col_f32],
        compiler_params=_params(dimension_semantics=("parallel", "arbitrary")),
    )(*operands)


def _position():
    x, y, c = lax.axis_index("x"), lax.axis_index("y"), lax.axis_index("c")
    return x, y, c, [(1 - x, y), (x, 1 - y), (1 - x, 1 - y)]


_ANY = pl.BlockSpec(memory_space=pl.ANY)
N_PEER_CHIPS = N_CHIPS - 1


def _all_gather_chips(shards, *, name):
    n = len(shards)

    def body(*refs):
        ins, outs = refs[:n], refs[n:2 * n]
        send_sems, recv_sems, local_sems = refs[2 * n:]
        x, y, c, peers = _position()
        me = 2 * x + y
        copies = []
        for a in range(n):
            copies.append(pltpu.make_async_copy(ins[a], outs[a].at[me], local_sems.at[a]))
            for j, (px, py) in enumerate(peers):
                copies.append(pltpu.make_async_remote_copy(
                    src_ref=ins[a], dst_ref=outs[a].at[me],
                    send_sem=send_sems.at[a * N_PEER_CHIPS + j], recv_sem=recv_sems.at[a * N_PEER_CHIPS + j],
                    device_id=(px, py, c), device_id_type=MESH))
        for cp in copies:
            cp.start()
        for cp in copies:
            cp.wait()

    return pl.pallas_call(
        body, name=name, in_specs=[_ANY] * n, out_specs=[_ANY] * n,
        out_shape=[jax.ShapeDtypeStruct((N_CHIPS, *s.shape), s.dtype) for s in shards],
        scratch_shapes=[pltpu.SemaphoreType.DMA((n * N_PEER_CHIPS,)), pltpu.SemaphoreType.DMA((n * N_PEER_CHIPS,)),
                        pltpu.SemaphoreType.DMA((n,))],
        compiler_params=pltpu.CompilerParams(has_side_effects=True),
    )(*shards)


def _place_own(packs, *, name):
    n = len(packs)

    def body(*refs):
        ins, outs, sems = refs[:n], refs[n:2 * n], refs[2 * n]
        x, y, _, _ = _position()
        copies = [pltpu.make_async_copy(ins[a], outs[a].at[2 * x + y], sems.at[a]) for a in range(n)]
        for cp in copies:
            cp.start()
        for cp in copies:
            cp.wait()

    return pl.pallas_call(
        body, name=name, in_specs=[_ANY] * n, out_specs=[_ANY] * n,
        out_shape=[jax.ShapeDtypeStruct((N_CHIPS, *p.shape), p.dtype) for p in packs],
        scratch_shapes=[pltpu.SemaphoreType.DMA((n,))],
    )(*packs)


_HBM = pl.BlockSpec(memory_space=pltpu.HBM)
_SEM = pl.BlockSpec(memory_space=pltpu.SEMAPHORE)
_DATAFLOW = pltpu.SideEffectType.DATAFLOW_SIDE_EFFECTING
_TOKEN = jax.ShapeDtypeStruct((8, LANES), F32)


def _in_hbm(arr):
    return pltpu.with_memory_space_constraint(arr, pltpu.HBM)


def _gather_start(packs, lands, *, name):
    n = len(packs)

    def body(*refs):
        src, land = refs[:n], refs[n:2 * n]
        send_sems, recv_sems = refs[2 * n:2 * n + 2]
        token = refs[-1]
        x, y, c, peers = _position()
        for i in range(n):
            for j, (px, py) in enumerate(peers):
                pltpu.make_async_remote_copy(
                    src_ref=src[i], dst_ref=land[i].at[2 * x + y],
                    send_sem=send_sems.at[N_PEER_CHIPS * i + j], recv_sem=recv_sems.at[N_PEER_CHIPS * i + j],
                    device_id=(px, py, c), device_id_type=MESH).start()
        token[...] = jnp.zeros_like(token)

    thru = [pltpu.HBM(a.shape, a.dtype) for a in (*packs, *lands)]
    outs = pl.pallas_call(
        body, name=name,
        out_shape=(pltpu.SemaphoreType.DMA((N_PEER_CHIPS * n,)), pltpu.SemaphoreType.DMA((N_PEER_CHIPS * n,)), *thru,
                   _TOKEN),
        in_specs=[_HBM] * (2 * n), out_specs=(_SEM, _SEM, *[_HBM] * (2 * n), pl.BlockSpec(memory_space=pltpu.VMEM)),
        input_output_aliases={k: 2 + k for k in range(2 * n)},
        compiler_params=pltpu.CompilerParams(has_side_effects=_DATAFLOW),
    )(*[_in_hbm(a) for a in (*packs, *lands)])
    return outs[0], outs[1], list(outs[2:2 + n]), list(outs[2 + n:2 + 2 * n]), outs[-1]


def _gather_wait(first, packs, lands, send_sems, recv_sems, after, *, name):
    n = len(packs)

    def body(*refs):
        src, land = refs[:n], refs[n:2 * n]
        send_sems, recv_sems = refs[2 * n:2 * n + 2]
        _, _, c, peers = _position()
        for a in range(n):
            for j, (px, py) in enumerate(peers):
                sem = N_PEER_CHIPS * (first + a) + j
                cp = pltpu.make_async_remote_copy(
                    src_ref=src[a], dst_ref=land[a].at[2 * px + py], send_sem=send_sems.at[sem],
                    recv_sem=recv_sems.at[sem], device_id=(px, py, c), device_id_type=MESH)
                cp.wait_send()
                cp.wait_recv()

    outs = pl.pallas_call(
        body, name=name, out_shape=[pltpu.HBM(a.shape, a.dtype) for a in (*packs, *lands)],
        in_specs=[_HBM] * (2 * n) + [_SEM, _SEM, _ANY], out_specs=[_HBM] * (2 * n),
        input_output_aliases={k: k for k in range(2 * n)},
        compiler_params=pltpu.CompilerParams(has_side_effects=_DATAFLOW),
    )(*packs, *lands, send_sems, recv_sems, after)
    return list(outs[n:])


def _scatter_start(gpacks, lands, *, name):
    n = len(gpacks)

    def body(*refs):
        src, land = refs[:n], refs[n:2 * n]
        send_sems, recv_sems = refs[2 * n:2 * n + 2]
        token = refs[-1]
        _, _, c, peers = _position()
        for a in range(n):
            for j, (px, py) in enumerate(peers):
                pltpu.make_async_remote_copy(
                    src_ref=src[a].at[2 * px + py], dst_ref=land[a].at[j], send_sem=send_sems.at[N_PEER_CHIPS * a + j],
                    recv_sem=recv_sems.at[N_PEER_CHIPS * a + j], device_id=(px, py, c), device_id_type=MESH).start()
        token[...] = jnp.zeros_like(token)

    thru = [pltpu.HBM(a.shape, a.dtype) for a in (*gpacks, *lands)]
    outs = pl.pallas_call(
        body, name=name,
        out_shape=(pltpu.SemaphoreType.DMA((N_PEER_CHIPS * n,)), pltpu.SemaphoreType.DMA((N_PEER_CHIPS * n,)), *thru,
                   _TOKEN),
        in_specs=[_HBM] * (2 * n), out_specs=(_SEM, _SEM, *[_HBM] * (2 * n), pl.BlockSpec(memory_space=pltpu.VMEM)),
        input_output_aliases={k: 2 + k for k in range(2 * n)},
        compiler_params=pltpu.CompilerParams(has_side_effects=_DATAFLOW),
    )(*[_in_hbm(a) for a in (*gpacks, *lands)])
    return outs[0], outs[1], list(outs[2:2 + n]), list(outs[2 + n:2 + 2 * n]), outs[-1]


def _scatter_wait(gpacks, lands, send_sems, recv_sems, after, *, name):
    n = len(gpacks)

    def body(*refs):
        src, land = refs[:n], refs[n:2 * n]
        send_sems, recv_sems = refs[2 * n:2 * n + 2]
        _, _, c, peers = _position()
        for a in range(n):
            for j, (px, py) in enumerate(peers):
                cp = pltpu.make_async_remote_copy(
                    src_ref=src[a].at[2 * px + py], dst_ref=land[a].at[j], send_sem=send_sems.at[N_PEER_CHIPS * a + j],
                    recv_sem=recv_sems.at[N_PEER_CHIPS * a + j], device_id=(px, py, c),
                    device_id_type=MESH)
                cp.wait_send()
                cp.wait_recv()

    outs = pl.pallas_call(
        body, name=name, out_shape=[pltpu.HBM(a.shape, a.dtype) for a in (*gpacks, *lands)],
        in_specs=[_HBM] * (2 * n) + [_SEM, _SEM, _ANY], out_specs=[_HBM] * (2 * n),
        input_output_aliases={k: k for k in range(2 * n)},
        compiler_params=pltpu.CompilerParams(has_side_effects=_DATAFLOW),
    )(*gpacks, *lands, send_sems, recv_sems, after)
    return list(outs[:n]), list(outs[n:])


def _sibling_exchange(arrs, *, name):
    n = len(arrs)

    def body(*refs):
        ins, outs = refs[:n], refs[n:2 * n]
        send_sems, recv_sems = refs[2 * n:]
        x, y, c, _ = _position()
        copies = [pltpu.make_async_remote_copy(
            src_ref=ins[a], dst_ref=outs[a], send_sem=send_sems.at[a], recv_sem=recv_sems.at[a],
            device_id=(x, y, 1 - c), device_id_type=MESH) for a in range(n)]
        for cp in copies:
            cp.start()
        for cp in copies:
            cp.wait()

    return pl.pallas_call(
        body, name=name, in_specs=[_ANY] * n, out_specs=[_ANY] * n,
        out_shape=[jax.ShapeDtypeStruct(a.shape, a.dtype) for a in arrs],
        scratch_shapes=[pltpu.SemaphoreType.DMA((n,)), pltpu.SemaphoreType.DMA((n,))],
        compiler_params=pltpu.CompilerParams(has_side_effects=True),
    )(*arrs)


def _small_all_reduce(v, *, name):
    r, cdim = v.shape

    def body(v_ref, o_ref, slots, send_sems, recv_sems):
        x, y, c, _ = _position()
        me = 4 * x + 2 * y + c
        slots[me] = v_ref[...]
        copies = []
        for j in range(1, N_DEVICES):
            peer = (x ^ ((j >> 2) & 1), y ^ ((j >> 1) & 1), c ^ (j & 1))
            copies.append(pltpu.make_async_remote_copy(
                src_ref=v_ref, dst_ref=slots.at[me], send_sem=send_sems.at[j - 1], recv_sem=recv_sems.at[j - 1],
                device_id=peer, device_id_type=MESH))
        for cp in copies:
            cp.start()
        for cp in copies:
            cp.wait()
        acc = slots[0]
        for dev in range(1, N_DEVICES):
            acc = acc + slots[dev]
        o_ref[...] = acc

    vm = pl.BlockSpec(memory_space=pltpu.VMEM)
    return pl.pallas_call(
        body, name=name, in_specs=[vm], out_specs=vm,
        out_shape=jax.ShapeDtypeStruct(v.shape, F32),
        scratch_shapes=[pltpu.VMEM((N_DEVICES, r, cdim), F32),
                        pltpu.SemaphoreType.DMA((N_DEVICES - 1,)), pltpu.SemaphoreType.DMA((N_DEVICES - 1,))],
        compiler_params=pltpu.CompilerParams(has_side_effects=True),
    )(v)


def _row_block(r, cap, mult):
    best = max(b for b in range(mult, cap + 1, mult) if r % b == 0)
    return best


def _sum_partials(chip_idx, own4, landed, *, name):
    _, r, c = landed.shape
    bm = _row_block(r, 512, 16)

    def body(chip_ref, own_ref, land_ref, o_ref):
        acc = own_ref[...].astype(F32)
        for j in range(N_PEER_CHIPS):
            acc = acc + land_ref[j].astype(F32)
        o_ref[...] = acc

    return pl.pallas_call(
        body, name=name,
        grid_spec=pltpu.PrefetchScalarGridSpec(
            num_scalar_prefetch=1, grid=(r // bm,),
            in_specs=[pl.BlockSpec((None, bm, c), lambda i, chip: (chip[0], i, 0)),
                      pl.BlockSpec((N_PEER_CHIPS, bm, c), lambda i, chip: (0, i, 0))],
            out_specs=pl.BlockSpec((bm, c), lambda i, chip: (i, 0))),
        out_shape=jax.ShapeDtypeStruct((r, c), F32),
        compiler_params=_params(dimension_semantics=("parallel",)),
    )(chip_idx, own4, landed)


def _adamw(w, m, v, g_a, g_b, *, name):
    r, c = w.shape
    bm = _blk(r, 256)
    two = g_b is not None
    bc1 = 1.0 - ADAM_B1 ** ADAM_STEP
    bc2 = 1.0 - ADAM_B2 ** ADAM_STEP

    def body(*refs):
        if two:
            w_ref, m_ref, v_ref, ga_ref, gb_ref, g_ref, d_ref, nm_ref, nv_ref = refs
            g = ga_ref[...] + gb_ref[...]
        else:
            w_ref, m_ref, v_ref, ga_ref, g_ref, d_ref, nm_ref, nv_ref = refs
            g = ga_ref[...]
        nm = ADAM_B1 * m_ref[...] + (1.0 - ADAM_B1) * g
        nv = ADAM_B2 * v_ref[...] + (1.0 - ADAM_B2) * (g * g)
        g_ref[...] = g
        nm_ref[...] = nm
        nv_ref[...] = nv
        d_ref[...] = -ADAM_LR * ((nm / bc1) / (jnp.sqrt(nv / bc2) + ADAM_EPS) + ADAM_WD * w_ref[...])

    spec = pl.BlockSpec((bm, c), lambda i: (i, 0))
    out = jax.ShapeDtypeStruct((r, c), F32)
    operands = [w, m, v, g_a] + ([g_b] if two else [])
    return pl.pallas_call(
        body, name=name, grid=(r // bm,),
        in_specs=[spec] * len(operands), out_specs=[spec] * 4, out_shape=[out] * 4,
        compiler_params=_params(dimension_semantics=("parallel",)),
    )(*operands)


def _heads(a):
    s, w = a.shape
    return a.reshape(s, w // HEAD_DIM, HEAD_DIM).transpose(1, 0, 2)


def _unheads(a):
    h, s, d = a.shape
    return a.transpose(1, 0, 2).reshape(s, h * d)


def _width_groups(shapes):
    groups = {}
    for idx, (_, c) in enumerate(shapes):
        groups.setdefault(c, []).append(idx)
    return list(groups.values())


def _pack(arrs, lead):
    groups = _width_groups([a.shape[-2:] for a in arrs])
    return [jnp.concatenate([arrs[k] for k in grp], axis=lead) for grp in groups]


def _unpack(bufs, shapes, lead):
    outs = [None] * len(shapes)
    for buf, grp in zip(bufs, _width_groups(shapes)):
        off = 0
        for k in grp:
            outs[k] = lax.slice_in_dim(buf, off, off + shapes[k][0], axis=lead)
            off += shapes[k][0]
    return outs


class LayerWeights(NamedTuple):
    w_in: jax.Array
    wm: jax.Array
    wo: jax.Array
    wg: jax.Array
    wu: jax.Array
    wd: jax.Array
    wkv: Optional[jax.Array]


def _pack_small(mix, ffn, kvn, memn, fin, conv):
    d = mix.shape[-1]
    flat = conv.reshape(-1)
    rows_conv = SMALL_ROWS - 11
    flat = jnp.pad(flat, (0, rows_conv * d - flat.shape[0]))
    return jnp.concatenate([mix, ffn, kvn.reshape(1, d), memn.reshape(1, d), fin.reshape(1, d),
                            flat.reshape(rows_conv, d)], axis=0)


def _unpack_small(buf, conv_shape):
    n = math.prod(conv_shape)
    return (buf[0:4], buf[4:8], buf[8], buf[9], buf[10], buf[11:].reshape(-1)[:n].reshape(conv_shape))


def kernel(x, mem, mix_norm, a_in, conv_w, b_in, kv_norm, w_kv_shared, w_mem_kv, w_o, ffn_norm, w_gate, w_up, w_down, mem_norm, final_norm, loss_target, m_mix_norm, m_a_in, m_conv_w, m_b_in, m_kv_norm, m_w_kv_shared, m_w_mem_kv, m_w_o, m_ffn_norm, m_w_gate, m_w_up, m_w_down, m_mem_norm, m_final_norm, v_mix_norm, v_a_in, v_conv_w, v_b_in, v_kv_norm, v_w_kv_shared, v_w_mem_kv, v_w_o, v_ffn_norm, v_w_gate, v_w_up, v_w_down, v_mem_norm, v_final_norm):
    s, d = x.shape[1], x.shape[2]
    n_mem = mem.shape[1]
    depth = mix_norm.shape[0]
    n_a = a_in.shape[0]
    main_w = conv_w.shape[2] * N_CHIPS
    mem_w = w_mem_kv.shape[2] // 2
    ffn_c = w_gate.shape[2]
    kv_c = w_kv_shared.shape[1]
    a_c = a_in.shape[2]
    chip = 2 * lax.axis_index("x") + lax.axis_index("y")

    x0 = x[0]
    mem0 = mem[0]
    tgt = loss_target[0]
    bs = _blk(s, 1024)

    def layer_shards(i, a_or_b, others, kv):
        ws = [a_or_b[0][i] if i < n_a else a_or_b[1][i - n_a]] + [w[i] for w in others]
        return ws + ([kv] if i == n_a - 1 else [])

    shards = [layer_shards(i, (a_in, b_in), (w_mem_kv, w_o, w_gate, w_up, w_down), w_kv_shared) for i in range(depth)]
    shapes = [[w.shape for w in ws] for ws in shards]
    layer_packs = [_pack([w.astype(BF16) for w in ws], 0) for ws in shards]
    first = [sum(len(p) for p in layer_packs[:i]) for i in range(depth + 1)]
    packs = [p for ps in layer_packs for p in ps]
    lands = _place_own(packs, name="place_own_shards")
    send_sems, recv_sems, packs, lands, started = _gather_start(packs, lands, name="gather_weights_start")
    conv_parts, = _all_gather_chips([conv_w], name="gather_conv_weights")
    conv_full = jnp.concatenate([conv_parts[kk] for kk in range(N_CHIPS)], axis=-1)

    def gain(vec):
        return vec.reshape(1, d)

    mem_n = _rms_fwd(mem0, gain(mem_norm), name="mem_norm_fwd", dep=started)
    saved = []
    lw = []
    k_sh = v_sh = hk = x_kv = None
    xc = x0
    for i in range(depth):
        mine = slice(first[i], first[i + 1])
        landed = _gather_wait(first[i], packs[mine], lands[mine], send_sems, recv_sems, xc,
                              name=f"gather_weights_wait_l{i}")
        parts = _unpack(landed, shapes[i], 1)
        lw.append(LayerWeights(*parts[:6], parts[6] if i == n_a - 1 else None))
        st = {"x_in": xc}
        h = _rms_fwd(xc, gain(mix_norm[i]), name="mix_norm_fwd")
        mkv = _mm(Op(mem_n), Op(lw[i].wm, 'r'), name="mem_kv_proj", bm=n_mem, bn=2 * mem_w, bk=d,
                  out_dtype=BF16)
        mem_k, mem_v = _heads(mkv[:, :mem_w]), _heads(mkv[:, mem_w:])
        if i < n_a:
            p = _mm(Op(h), Op(lw[i].w_in, 'c'), name="a_in_proj", bm=bs, bn=a_c, bk=d, out_dtype=BF16)
            y_main = _conv_fwd(p, conv_full[i], name="conv_fwd")
            q_mem = _heads(p[:, 3 * main_w:])
        else:
            p = _mm(Op(h), Op(lw[i].w_in, 'r'), name="b_in_proj", bm=bs, bn=d, bk=d, out_dtype=BF16)
            q_sb = _heads(p[:, :main_w])
            o_sb, o_sb32 = _sb_fwd(q_sb, k_sh, v_sh, name="sb_fwd")
            y_main = _unheads(o_sb)
            q_mem = _heads(p[:, main_w:])
            st.update(q_sb=q_sb, o_sb32=o_sb32)
        y_mem = _mem_fwd(q_mem, mem_k, mem_v, name="mem_attn_fwd")
        y = jnp.concatenate([y_main, _unheads(y_mem)], axis=-1)
        x_mid = _mm(Op(y), Op(lw[i].wo, 'r'), name="w_o_proj", bm=bs, bn=d, bk=d, out_dtype=F32,
                    res=Op(xc))
        h2 = _rms_fwd(x_mid, gain(ffn_norm[i]), name="ffn_norm_fwd")
        gate = _mm(Op(h2), Op(lw[i].wg, 'c'), name="w_gate_proj", bm=bs, bn=ffn_c, bk=d, out_dtype=BF16,
                   out_chunk='c')
        up = _mm(Op(h2), Op(lw[i].wu, 'c'), name="w_up_proj", bm=bs, bn=ffn_c, bk=d, out_dtype=BF16,
                 out_chunk='c')
        act = _swiglu_fwd(gate, up, name="swiglu_fwd")
        xc = _mm(Op(act, 'c'), Op(lw[i].wd, 'r'), name="w_down_proj", bm=bs, bn=d, bk=ffn_c, out_dtype=F32,
                 res=Op(x_mid))
        st.update(h=h, p=p, mem_k=mem_k, mem_v=mem_v, q_mem=q_mem, y=y, x_mid=x_mid, h2=h2, gate=gate, up=up,
                  act=act)
        saved.append(st)
        if i == n_a - 1:
            x_kv = xc
            hk = _rms_fwd(xc, gain(kv_norm), name="kv_norm_fwd")
            kv = _mm(Op(hk), Op(lw[n_a - 1].wkv, 'c'), name="kv_proj", bm=bs, bn=kv_c, bk=d, out_dtype=BF16)
            k_sh, v_sh = _heads(kv[:, :main_w]), _heads(kv[:, main_w:])

    dx, dg_final, loss_part = _final_loss(xc, gain(final_norm), tgt, name="final_norm_loss")
    loss = lax.psum(loss_part[0, 0], ("x", "y", "c"))

    g_a, g_b, g_m, g_o, g_g, g_u, g_d = ([None] * n_a, [None] * (depth - n_a), [None] * depth, [None] * depth,
                                         [None] * depth, [None] * depth, [None] * depth)
    dg_mix, dg_ffn, dconv = [None] * depth, [None] * depth, [None] * n_a
    in_flight = [None] * depth
    dk_sh = dv_sh = None
    dmem_n = None
    g_kv = dg_kv = None
    for i in reversed(range(depth)):
        st = saved[i]
        dact = _mm(Op(dx), Op(lw[i].wd, 'r'), name="w_down_dact", tb=True, bm=bs, bn=ffn_c, bk=d, out_dtype=BF16,
                   out_chunk='c')
        g_d[i] = _mm(Op(st["act"], 'c'), Op(dx), name="w_down_grad", ta=True, bm=ffn_c, bn=d, bk=bs,
                     out_dtype=BF16, out_chunk='r')
        dgate, dup = _swiglu_bwd(dact, st["gate"], st["up"], name="swiglu_bwd")
        g_g[i] = _mm(Op(st["h2"]), Op(dgate, 'c'), name="w_gate_grad", ta=True, bm=d, bn=ffn_c, bk=bs,
                     out_dtype=BF16, out_chunk='c')
        g_u[i] = _mm(Op(st["h2"]), Op(dup, 'c'), name="w_up_grad", ta=True, bm=d, bn=ffn_c, bk=bs,
                     out_dtype=BF16, out_chunk='c')
        dh2 = _mm(Op(dgate, 'c'), Op(lw[i].wg, 'c'), name="w_gate_dh", tb=True, bm=bs, bn=d, bk=ffn_c,
                  out_dtype=F32)
        dh2 = _mm(Op(dup, 'c'), Op(lw[i].wu, 'c'), name="w_up_dh", tb=True, bm=bs, bn=d, bk=ffn_c, out_dtype=F32,
                  res=Op(dh2))
        dx_mid, dg_ffn[i] = _rms_bwd(st["x_mid"], gain(ffn_norm[i]), dh2, dx, name="ffn_norm_bwd")
        dy = _mm(Op(dx_mid), Op(lw[i].wo, 'r'), name="w_o_dy", tb=True, bm=bs, bn=d, bk=d,
                 out_dtype=BF16)
        g_o[i] = _mm(Op(st["y"]), Op(dx_mid), name="w_o_grad", ta=True, bm=d // N_CHIPS, bn=d, bk=bs,
                     out_dtype=BF16, out_chunk='r')
        dq_mem, dmk, dmv = _mem_bwd(st["q_mem"], st["mem_k"], st["mem_v"], _heads(dy[:, main_w:]),
                                    name="mem_attn_bwd")
        dmkv = jnp.concatenate([_unheads(dmk), _unheads(dmv)], axis=-1)
        g_m[i] = _mm(Op(mem_n), Op(dmkv), name="mem_kv_grad", ta=True, bm=d // N_CHIPS, bn=2 * mem_w, bk=n_mem,
                     out_dtype=BF16, out_chunk='r')
        dmem_n = _mm(Op(dmkv), Op(lw[i].wm, 'r'), name="mem_kv_dmem", tb=True, bm=n_mem, bn=d,
                     bk=2 * mem_w, out_dtype=F32, res=None if dmem_n is None else Op(dmem_n))
        if i < n_a:
            db, dc, du, dconv[i] = _conv_bwd(st["p"], conv_full[i], dy[:, :main_w], name="conv_bwd")
            dp = jnp.concatenate([db, dc, du, _unheads(dq_mem)], axis=-1)
            g_a[i] = _mm(Op(st["h"]), Op(dp), name="a_in_grad", ta=True, bm=d, bn=a_c, bk=bs, out_dtype=BF16,
                         out_chunk='c')
            dh = _mm(Op(dp), Op(lw[i].w_in, 'c'), name="a_in_dh", tb=True, bm=bs, bn=d, bk=a_c, out_dtype=F32)
        else:
            dq_sb, dk_sh, dv_sh = _sb_bwd(st["q_sb"], k_sh, v_sh, st["o_sb32"], _heads(dy[:, :main_w]),
                                          dk_sh, dv_sh, name="sb_bwd")
            dp = jnp.concatenate([_unheads(dq_sb), _unheads(dq_mem)], axis=-1)
            g_b[i - n_a] = _mm(Op(st["h"]), Op(dp), name="b_in_grad", ta=True, bm=d // N_CHIPS, bn=d, bk=bs,
                               out_dtype=BF16, out_chunk='r')
            dh = _mm(Op(dp), Op(lw[i].w_in, 'r'), name="b_in_dh", tb=True, bm=bs, bn=d, bk=d,
                     out_dtype=F32)
        grads = [g_a[i] if i < n_a else g_b[i - n_a], g_m[i], g_o[i], g_g[i], g_u[i], g_d[i]]
        grads += [g_kv] if i == n_a - 1 else []
        gpacks = _pack(grads, 1)
        in_flight[i] = _scatter_start(gpacks, [lax.empty((N_PEER_CHIPS, *g.shape[1:]), BF16) for g in gpacks],
                                      name=f"scatter_grads_start_l{i}")
        dx, dg_mix[i] = _rms_bwd(st["x_in"], gain(mix_norm[i]), dh, dx_mid, name="mix_norm_bwd",
                                 dep=in_flight[i][4])
        if i == n_a:
            dkv = jnp.concatenate([_unheads(dk_sh), _unheads(dv_sh)], axis=-1)
            g_kv = _mm(Op(hk), Op(dkv), name="kv_grad", ta=True, bm=d, bn=kv_c, bk=bs, out_dtype=BF16,
                       out_chunk='c')
            dhk = _mm(Op(dkv), Op(lw[n_a - 1].wkv, 'c'), name="kv_dh", tb=True, bm=bs, bn=d, bk=kv_c, out_dtype=F32)
            dx, dg_kv = _rms_bwd(x_kv, gain(kv_norm), dhk, dx, name="kv_norm_bwd")
    _, dg_mem = _rms_bwd(mem0, gain(mem_norm), dmem_n, None, name="mem_norm_bwd")

    chip_idx = chip.astype(jnp.int32).reshape(1)
    core_sums = []
    for i in range(depth):
        ssem, rsem, gthru, lthru, _ = in_flight[i]
        gthru, lthru = _scatter_wait(gthru, lthru, ssem, rsem, dx, name=f"scatter_grads_wait_l{i}")
        core_sums += [_sum_partials(chip_idx, g, l, name="sum_chip_partials") for g, l in zip(gthru, lthru)]
    sibling_sums = _sibling_exchange(core_sums, name="exchange_core_sums")
    own_parts = [_unpack(core_sums[first[i]:first[i + 1]], shapes[i], 0) for i in range(depth)]
    sib_parts = [_unpack(sibling_sums[first[i]:first[i + 1]], shapes[i], 0) for i in range(depth)]

    def stacked(parts, pos, layers):
        return jnp.concatenate([parts[i][pos] for i in layers], axis=0)

    a_layers, b_layers, all_layers = range(n_a), range(n_a, depth), range(depth)
    big = [("a_in", a_in, m_a_in, v_a_in, 0, a_layers), ("b_in", b_in, m_b_in, v_b_in, 0, b_layers),
           ("w_kv_shared", w_kv_shared, m_w_kv_shared, v_w_kv_shared, 6, [n_a - 1]),
           ("w_mem_kv", w_mem_kv, m_w_mem_kv, v_w_mem_kv, 1, all_layers), ("w_o", w_o, m_w_o, v_w_o, 2, all_layers),
           ("w_gate", w_gate, m_w_gate, v_w_gate, 3, all_layers), ("w_up", w_up, m_w_up, v_w_up, 4, all_layers),
           ("w_down", w_down, m_w_down, v_w_down, 5, all_layers)]
    results = {}
    for wname, w, mm_, vv_, pos, layers in big:
        flat = lambda t: t.reshape(-1, t.shape[-1])
        outs = _adamw(flat(w), flat(mm_), flat(vv_), stacked(own_parts, pos, layers), stacked(sib_parts, pos, layers),
                      name="adamw")
        results[wname] = [o.reshape(w.shape) for o in outs]

    small_g = _pack_small(jnp.concatenate(dg_mix, axis=0), jnp.concatenate(dg_ffn, axis=0), dg_kv, dg_mem,
                          dg_final, jnp.stack(dconv, axis=0))
    small_g = _small_all_reduce(small_g, name="all_reduce_small_grads")
    conv_shape_full = (n_a, CONV_TAPS, main_w)
    gs = list(_unpack_small(small_g, conv_shape_full))
    gs[5] = lax.dynamic_slice_in_dim(gs[5], chip * conv_w.shape[2], conv_w.shape[2], axis=2)
    small_outs = _adamw(_pack_small(mix_norm, ffn_norm, kv_norm, mem_norm, final_norm, conv_w),
                        _pack_small(m_mix_norm, m_ffn_norm, m_kv_norm, m_mem_norm, m_final_norm, m_conv_w),
                        _pack_small(v_mix_norm, v_ffn_norm, v_kv_norm, v_mem_norm, v_final_norm, v_conv_w),
                        _pack_small(*gs), None, name="adamw_small")
    small_names = ["mix_norm", "ffn_norm", "kv_norm", "mem_norm", "final_norm", "conv_w"]
    for kind, buf in enumerate(small_outs):
        for wname, val in zip(small_names, _unpack_small(buf, conv_w.shape)):
            results.setdefault(wname, [None] * 4)[kind] = val

    order = ["mix_norm", "a_in", "conv_w", "b_in", "kv_norm", "w_kv_shared", "w_mem_kv", "w_o", "ffn_norm",
             "w_gate", "w_up", "w_down", "mem_norm", "final_norm"]
    return (loss, dx[None], *[results[nm][0] for nm in order], *[results[nm][1] for nm in order],
            *[results[nm][2] for nm in order], *[results[nm][3] for nm in order])
```

```python
import math
from typing import NamedTuple, Optional

import jax
import jax.numpy as jnp
from jax import lax
from jax.experimental import pallas as pl
from jax.experimental.pallas import tpu as pltpu

F32 = jnp.float32
BF16 = jnp.bfloat16
MESH = pl.DeviceIdType.MESH

N_CHIPS = 4
N_DEVICES = 8
HEAD_DIM = 64
CONV_TAPS = 3
NORM_EPS = 1e-6
V7X_VMEM_BYTES = 64 * 1024 * 1024
VMEM_LIMIT = V7X_VMEM_BYTES - 8 * 1024 * 1024
LANES = 128
SMALL_ROWS = 16

ADAM_LR = 0.001
ADAM_B1 = 0.9
ADAM_B2 = 0.999
ADAM_EPS = 1e-08
ADAM_WD = 0.01
ADAM_STEP = 10


def _params(**kw):
    return pltpu.CompilerParams(vmem_limit_bytes=VMEM_LIMIT, **kw)


def _blk(n, pref):
    b = min(n, pref)
    assert n % b == 0, (n, pref)
    return b


class Op(NamedTuple):
    arr: jax.Array
    chunk: Optional[str] = None
    layer: Optional[int] = None


def _op_spec(op_chunk, op_layer, shape2, br, bc, pick):
    r, c = shape2
    lead = () if op_layer is None else (op_layer,)
    none = (None,) * len(lead)
    if op_chunk is None:
        def imap(i, j, k):
            rb, cb = pick(i, j, k)
            return (*lead, rb, cb)
        return pl.BlockSpec((*none, br, bc), imap)
    if op_chunk == 'r' and br == N_CHIPS * r:
        def imap(i, j, k):
            rb, cb = pick(i, j, k)
            return (0, *lead, 0, cb)
        return pl.BlockSpec((N_CHIPS, *none, r, bc), imap)
    if op_chunk == 'r':
        n = r // br
        assert r % br == 0

        def imap(i, j, k):
            rb, cb = pick(i, j, k)
            return (rb // n, *lead, rb % n, cb)
        return pl.BlockSpec((None, *none, br, bc), imap)
    n = c // bc
    assert c % bc == 0

    def imap(i, j, k):
        rb, cb = pick(i, j, k)
        return (cb // n, *lead, rb, cb % n)
    return pl.BlockSpec((None, *none, br, bc), imap)


def _mm(a, b, *, name, ta=False, tb=False, bm, bn, bk, out_dtype, out_chunk=None, res=None):
    def dims(op):
        r, c = op.arr.shape[-2:]
        return (r * N_CHIPS if op.chunk == 'r' else r, c * N_CHIPS if op.chunk == 'c' else c)

    ar, ac = dims(a)
    br_, bc_ = dims(b)
    m, ka = (ac, ar) if ta else (ar, ac)
    kb, n = (bc_, br_) if tb else (br_, bc_)
    assert ka == kb, (name, ka, kb)
    assert m % bm == 0 and n % bn == 0 and ka % bk == 0, (name, m, n, ka, bm, bn, bk)
    nk = ka // bk

    if ta:
        a_spec = _op_spec(a.chunk, a.layer, a.arr.shape[-2:], bk, bm, lambda i, j, k: (k, i))
    else:
        a_spec = _op_spec(a.chunk, a.layer, a.arr.shape[-2:], bm, bk, lambda i, j, k: (i, k))
    if tb:
        b_spec = _op_spec(b.chunk, b.layer, b.arr.shape[-2:], bn, bk, lambda i, j, k: (j, k))
    else:
        b_spec = _op_spec(b.chunk, b.layer, b.arr.shape[-2:], bk, bn, lambda i, j, k: (k, j))

    if out_chunk == 'r':
        out_shape2 = (m // N_CHIPS, n)
    elif out_chunk == 'c':
        out_shape2 = (m, n // N_CHIPS)
    else:
        out_shape2 = (m, n)
    o_spec = _op_spec(out_chunk, None, out_shape2, bm, bn, lambda i, j, k: (i, j))
    out_full = out_shape2 if out_chunk is None else (N_CHIPS, *out_shape2)

    contract = (((0 if ta else 1,), (1 if tb else 0,)), ((), ()))
    has_res = res is not None

    def block2(ref):
        v = ref[...]
        return v.reshape(-1, v.shape[-1]).astype(BF16)

    def body(*refs):
        r_ref = refs[2] if has_res else None
        a_ref, b_ref = refs[:2]
        o_ref = refs[3 if has_res else 2]
        prod = lax.dot_general(block2(a_ref), block2(b_ref), contract, preferred_element_type=F32)
        if nk == 1:
            if has_res:
                prod = prod + r_ref[...].astype(F32)
            o_ref[...] = prod.astype(o_ref.dtype)
            return
        acc_ref = refs[-1]
        k = pl.program_id(2)

        @pl.when(k == 0)
        def _():
            acc_ref[...] = prod

        @pl.when(k > 0)
        def _():
            acc_ref[...] += prod

        @pl.when(k == nk - 1)
        def _():
            acc = acc_ref[...]
            if has_res:
                acc = acc + r_ref[...].astype(F32)
            o_ref[...] = acc.astype(o_ref.dtype)

    in_specs = [a_spec, b_spec]
    operands = [a.arr, b.arr]
    if has_res:
        in_specs.append(_op_spec(res.chunk, res.layer, res.arr.shape[-2:], bm, bn, lambda i, j, k: (i, j)))
        operands.append(res.arr)
    return pl.pallas_call(
        body, name=name, grid=(m // bm, n // bn, nk),
        in_specs=in_specs, out_specs=o_spec,
        out_shape=jax.ShapeDtypeStruct(out_full, out_dtype),
        scratch_shapes=[pltpu.VMEM((bm, bn), F32)] if nk > 1 else [],
        compiler_params=_params(dimension_semantics=("parallel", "parallel", "arbitrary")),
    )(*operands)


def _rms_fwd(x, g, *, name, dep=None):
    r, d = x.shape
    bm = _blk(r, 512)

    def body(x_ref, g_ref, *rest):
        o_ref = rest[-1]
        xv = x_ref[...]
        rstd = lax.rsqrt(jnp.mean(xv * xv, axis=-1, keepdims=True) + NORM_EPS)
        o_ref[...] = ((xv * rstd) * g_ref[...]).astype(o_ref.dtype)

    deps = [] if dep is None else [dep]
    return pl.pallas_call(
        body, name=name, grid=(r // bm,),
        in_specs=[pl.BlockSpec((bm, d), lambda i: (i, 0)), pl.BlockSpec((1, d), lambda i: (0, 0))]
        + [pl.BlockSpec(memory_space=pl.ANY)] * len(deps),
        out_specs=pl.BlockSpec((bm, d), lambda i: (i, 0)),
        out_shape=jax.ShapeDtypeStruct((r, d), BF16),
        compiler_params=_params(dimension_semantics=("parallel",)),
    )(x, g, *deps)


def _rms_bwd(x, g, dh, dres, *, name, dep=None):
    r, d = x.shape
    bm = _blk(r, 512)
    has_res = dres is not None
    deps = [] if dep is None else [dep]

    def body(*refs):
        if has_res:
            x_ref, g_ref, dh_ref, dres_ref = refs[:4]
        else:
            x_ref, g_ref, dh_ref = refs[:3]
        dx_ref, dg_ref = refs[-2:]

        @pl.when(pl.program_id(0) == 0)
        def _():
            dg_ref[...] = jnp.zeros_like(dg_ref)

        xv = x_ref[...]
        rstd = lax.rsqrt(jnp.mean(xv * xv, axis=-1, keepdims=True) + NORM_EPS)
        xh = xv * rstd
        dhv = dh_ref[...].astype(F32)
        dg_ref[...] += jnp.sum(dhv * xh, axis=0, keepdims=True)
        dxh = dhv * g_ref[...]
        dx = rstd * (dxh - xh * jnp.mean(dxh * xh, axis=-1, keepdims=True))
        if has_res:
            dx = dres_ref[...] + dx
        dx_ref[...] = dx

    row = pl.BlockSpec((bm, d), lambda i: (i, 0))
    vec = pl.BlockSpec((1, d), lambda i: (0, 0))
    in_specs = [row, vec, row] + ([row] if has_res else []) + [pl.BlockSpec(memory_space=pl.ANY)] * len(deps)
    operands = [x, g, dh] + ([dres] if has_res else []) + deps
    return pl.pallas_call(
        body, name=name, grid=(r // bm,),
        in_specs=in_specs, out_specs=[row, vec],
        out_shape=[jax.ShapeDtypeStruct((r, d), F32), jax.ShapeDtypeStruct((1, d), F32)],
        compiler_params=_params(dimension_semantics=("arbitrary",)),
    )(*operands)


def _final_loss(x, g, tgt, *, name):
    r, d = x.shape
    bm = _blk(r, 512)

    def body(x_ref, g_ref, t_ref, dx_ref, dg_ref, loss_ref):
        @pl.when(pl.program_id(0) == 0)
        def _():
            dg_ref[...] = jnp.zeros_like(dg_ref)
            loss_ref[...] = jnp.zeros_like(loss_ref)

        xv = x_ref[...]
        gv = g_ref[...]
        rstd = lax.rsqrt(jnp.mean(xv * xv, axis=-1, keepdims=True) + NORM_EPS)
        xh = xv * rstd
        diff = xh * gv - t_ref[...]
        loss_ref[...] += jnp.sum(diff * diff) * (0.5 / d)
        dy = diff * (1.0 / d)
        dg_ref[...] += jnp.sum(dy * xh, axis=0, keepdims=True)
        dxh = dy * gv
        dx_ref[...] = rstd * (dxh - xh * jnp.mean(dxh * xh, axis=-1, keepdims=True))

    row = pl.BlockSpec((bm, d), lambda i: (i, 0))
    vec = pl.BlockSpec((1, d), lambda i: (0, 0))
    return pl.pallas_call(
        body, name=name, grid=(r // bm,),
        in_specs=[row, vec, row],
        out_specs=[row, vec, pl.BlockSpec((1, LANES), lambda i: (0, 0))],
        out_shape=[jax.ShapeDtypeStruct((r, d), F32), jax.ShapeDtypeStruct((1, d), F32),
                   jax.ShapeDtypeStruct((1, LANES), F32)],
        compiler_params=_params(dimension_semantics=("arbitrary",)),
    )(x, g, tgt)


def _shift_down(v, k, row):
    return jnp.where(row >= k, pltpu.roll(v, k, 0), 0.0)


def _shift_up(v, k, row, s):
    return jnp.where(row < s - k, pltpu.roll(v, s - k, 0), 0.0)


def _conv_fwd(p, w, *, name):
    s = p.shape[0]
    width = w.shape[1]
    nb = width // LANES

    def body(b_ref, c_ref, u_ref, w_ref, y_ref):
        cu = c_ref[...].astype(F32) * u_ref[...].astype(F32)
        row = lax.broadcasted_iota(jnp.int32, cu.shape, 0)
        wv = w_ref[...]
        conv = wv[2:3] * cu + wv[1:2] * _shift_down(cu, 1, row) + wv[0:1] * _shift_down(cu, 2, row)
        y_ref[...] = (b_ref[...].astype(F32) * conv).astype(y_ref.dtype)

    def col(o):
        return pl.BlockSpec((s, LANES), lambda j: (0, j + o * nb))

    return pl.pallas_call(
        body, name=name, grid=(nb,),
        in_specs=[col(0), col(1), col(2), pl.BlockSpec((CONV_TAPS, LANES), lambda j: (0, j))],
        out_specs=col(0),
        out_shape=jax.ShapeDtypeStruct((s, width), BF16),
        compiler_params=_params(dimension_semantics=("parallel",)),
    )(p, p, p, w)


def _conv_bwd(p, w, dy, *, name):
    s = p.shape[0]
    width = w.shape[1]
    nb = width // LANES

    def body(b_ref, c_ref, u_ref, w_ref, dy_ref, db_ref, dc_ref, du_ref, dw_ref):
        bv = b_ref[...].astype(F32)
        cv = c_ref[...].astype(F32)
        uv = u_ref[...].astype(F32)
        dyv = dy_ref[...].astype(F32)
        cu = cv * uv
        row = lax.broadcasted_iota(jnp.int32, cu.shape, 0)
        wv = w_ref[...]
        cu1 = _shift_down(cu, 1, row)
        cu2 = _shift_down(cu, 2, row)
        conv = wv[2:3] * cu + wv[1:2] * cu1 + wv[0:1] * cu2
        db_ref[...] = (dyv * conv).astype(db_ref.dtype)
        dconv = dyv * bv
        dcu = wv[2:3] * dconv + wv[1:2] * _shift_up(dconv, 1, row, s) + wv[0:1] * _shift_up(dconv, 2, row, s)
        dc_ref[...] = (dcu * uv).astype(dc_ref.dtype)
        du_ref[...] = (dcu * cv).astype(du_ref.dtype)
        dw_ref[0:1, :] = jnp.sum(dconv * cu2, axis=0, keepdims=True)
        dw_ref[1:2, :] = jnp.sum(dconv * cu1, axis=0, keepdims=True)
        dw_ref[2:3, :] = jnp.sum(dconv * cu, axis=0, keepdims=True)

    def col(o):
        return pl.BlockSpec((s, LANES), lambda j: (0, j + o * nb))

    wspec = pl.BlockSpec((CONV_TAPS, LANES), lambda j: (0, j))
    act = jax.ShapeDtypeStruct((s, width), BF16)
    return pl.pallas_call(
        body, name=name, grid=(nb,),
        in_specs=[col(0), col(1), col(2), wspec, col(0)],
        out_specs=[col(0), col(0), col(0), wspec],
        out_shape=[act, act, act, jax.ShapeDtypeStruct((CONV_TAPS, width), F32)],
        compiler_params=_params(dimension_semantics=("parallel",)),
    )(p, p, p, w, dy)


def _swiglu_fwd(g, u, *, name):
    nc, s, f = g.shape
    bm = _blk(s, 512)

    def body(g_ref, u_ref, o_ref):
        gv = g_ref[...].astype(F32)
        o_ref[...] = (gv * jax.nn.sigmoid(gv) * u_ref[...].astype(F32)).astype(o_ref.dtype)

    spec = pl.BlockSpec((None, bm, f), lambda c, i: (c, i, 0))
    return pl.pallas_call(
        body, name=name, grid=(nc, s // bm), in_specs=[spec, spec], out_specs=spec,
        out_shape=jax.ShapeDtypeStruct(g.shape, BF16),
        compiler_params=_params(dimension_semantics=("parallel", "parallel")),
    )(g, u)


def _swiglu_bwd(dact, g, u, *, name):
    nc, s, f = g.shape
    bm = _blk(s, 512)

    def body(d_ref, g_ref, u_ref, dg_ref, du_ref):
        gv = g_ref[...].astype(F32)
        dv = d_ref[...].astype(F32)
        sg = jax.nn.sigmoid(gv)
        silu = gv * sg
        dg_ref[...] = (dv * u_ref[...].astype(F32) * (sg * (1.0 + gv * (1.0 - sg)))).astype(dg_ref.dtype)
        du_ref[...] = (dv * silu).astype(du_ref.dtype)

    spec = pl.BlockSpec((None, bm, f), lambda c, i: (c, i, 0))
    out = jax.ShapeDtypeStruct(g.shape, BF16)
    return pl.pallas_call(
        body, name=name, grid=(nc, s // bm), in_specs=[spec, spec, spec], out_specs=[spec, spec],
        out_shape=[out, out],
        compiler_params=_params(dimension_semantics=("parallel", "parallel")),
    )(dact, g, u)


_NT = (((1,), (1,)), ((), ()))
_TN = (((0,), (0,)), ((), ()))


def _mem_probs(q, k, scale):
    s = lax.dot_general(q, k, _NT, preferred_element_type=F32) * scale
    e = jnp.exp(s - jnp.max(s, axis=-1, keepdims=True))
    return e / jnp.sum(e, axis=-1, keepdims=True)


def _mem_fwd(q, k, v, *, name):
    h, s, d = q.shape
    m = k.shape[1]
    bq = _blk(s, 1024)
    scale = 1.0 / math.sqrt(d)

    def body(q_ref, k_ref, v_ref, o_ref):
        p = _mem_probs(q_ref[...], k_ref[...], scale)
        o_ref[...] = jnp.dot(p.astype(BF16), v_ref[...], preferred_element_type=F32).astype(o_ref.dtype)

    qs = pl.BlockSpec((None, bq, d), lambda hh, i: (hh, i, 0))
    ks = pl.BlockSpec((None, m, d), lambda hh, i: (hh, 0, 0))
    return pl.pallas_call(
        body, name=name, grid=(h, s // bq), in_specs=[qs, ks, ks], out_specs=qs,
        out_shape=jax.ShapeDtypeStruct(q.shape, BF16),
        compiler_params=_params(dimension_semantics=("parallel", "parallel")),
    )(q, k, v)


def _mem_bwd(q, k, v, do, *, name):
    h, s, d = q.shape
    m = k.shape[1]
    bq = _blk(s, 1024)
    scale = 1.0 / math.sqrt(d)

    def body(q_ref, k_ref, v_ref, do_ref, dq_ref, dk_ref, dv_ref):
        @pl.when(pl.program_id(1) == 0)
        def _():
            dk_ref[...] = jnp.zeros_like(dk_ref)
            dv_ref[...] = jnp.zeros_like(dv_ref)

        qv = q_ref[...]
        kv = k_ref[...]
        dov = do_ref[...]
        p = _mem_probs(qv, kv, scale)
        pb = p.astype(BF16)
        dp = lax.dot_general(dov, v_ref[...], _NT, preferred_element_type=F32)
        pf = pb.astype(F32)
        ds = (pf * (dp - jnp.sum(pf * dp, axis=-1, keepdims=True)) * scale).astype(BF16)
        dq_ref[...] = jnp.dot(ds, kv, preferred_element_type=F32).astype(dq_ref.dtype)
        dk_ref[...] += lax.dot_general(ds, qv, _TN, preferred_element_type=F32)
        dv_ref[...] += lax.dot_general(pb, dov, _TN, preferred_element_type=F32)

    qs = pl.BlockSpec((None, bq, d), lambda hh, i: (hh, i, 0))
    ks = pl.BlockSpec((None, m, d), lambda hh, i: (hh, 0, 0))
    kvout = jax.ShapeDtypeStruct(k.shape, F32)
    return pl.pallas_call(
        body, name=name, grid=(h, s // bq), in_specs=[qs, ks, ks, qs], out_specs=[qs, ks, ks],
        out_shape=[jax.ShapeDtypeStruct(q.shape, BF16), kvout, kvout],
        compiler_params=_params(dimension_semantics=("parallel", "arbitrary")),
    )(q, k, v, do)


SB_TILE = 256
SB_STRIP = 32
SB_HEAD_GROUP = 4
SB_BWD_HEAD_GROUP = 2


def _sb_scale(d):
    scale = 1.0 / math.sqrt(d)
    assert math.frexp(scale)[0] == 0.5, "the scale is folded into bf16 q, exact only for a power of two"
    return scale


def _sb_strip_mask(r, t):
    rr = r + lax.broadcasted_iota(jnp.int32, (SB_STRIP, t), 0)
    return lax.broadcasted_iota(jnp.int32, (SB_STRIP, t), 1) < rr


def _neg_abs(z):
    bits = lax.bitcast_convert_type(z, jnp.uint32) | jnp.uint32(0x80000000)
    return lax.bitcast_convert_type(bits, F32)


def _store_split(split_scr, rows, val, t):
    hi = val.astype(BF16)
    split_scr[rows, 0:t] = hi
    split_scr[rows, t:2 * t] = (val - hi.astype(F32)).astype(BF16)


def _sb_logs_phase(z_scr, nsplit_scr, beta_scr, t, diag):
    for r in range(0, t, SB_STRIP):
        rows = pl.ds(r, SB_STRIP)
        z = z_scr[rows, :]
        e = jnp.exp(_neg_abs(z))
        nlog = jnp.maximum(z, 0.0) + jnp.log(1.0 + e)
        if beta_scr is not None:
            inv = pl.reciprocal(1.0 + e, approx=True)
            beta_scr[rows, :] = jnp.where(z >= 0.0, inv, e * inv)
        if diag:
            nlog = jnp.where(_sb_strip_mask(r, t), nlog, 0.0)
        _store_split(nsplit_scr, rows, nlog, t)


def _sb_probs(z_scr, tin_scr, rsum_scr, rows, r, t, diag):
    rs = rsum_scr[rows, :]
    a = jnp.exp(z_scr[rows, :] - tin_scr[rows, :] - rs)
    if diag:
        a = jnp.where(_sb_strip_mask(r, t), a, 0.0)
    rsum_scr[rows, :] = rs + tin_scr[rows, 0:1]
    return a


def _sb_triangle(tri_scr, t):
    row = lax.broadcasted_iota(jnp.int32, (t, t), 0)
    col = lax.broadcasted_iota(jnp.int32, (t, t), 1)
    tri = (row >= col).astype(BF16)
    tri_scr[0:t, :] = tri
    tri_scr[t:2 * t, :] = tri


def _sb_fwd(q, k, v, *, name):
    h, s, d = q.shape
    t = _blk(s, SB_TILE)
    scale = _sb_scale(d)

    def body(q_ref, k_ref, v_ref, o_ref, o32_ref, qs_scr, tri_scr, z_scr, nsplit_scr, tin_scr, a_scr, rsum_scr):
        qi = pl.program_id(1)
        qs_scr[...] = q_ref[...] * scale
        _sb_triangle(tri_scr, t)
        rsum_scr[...] = jnp.zeros_like(rsum_scr)
        o32_ref[...] = jnp.zeros_like(o32_ref)

        def tile(kb, diag):
            keys = pl.ds(pl.multiple_of(kb * t, t), t)
            for g in range(grp):
                z_scr[g] = lax.dot_general(qs_scr[g], k_ref[g, keys, :], _NT, preferred_element_type=F32)
            for g in range(grp):
                _sb_logs_phase(z_scr.at[g], nsplit_scr.at[g], None, t, diag)
                tin_scr[g] = jnp.dot(nsplit_scr[g], tri_scr[...], preferred_element_type=F32)
            for g in range(grp):
                for r in range(0, t, SB_STRIP):
                    rows = pl.ds(r, SB_STRIP)
                    a = _sb_probs(z_scr.at[g], tin_scr.at[g], rsum_scr.at[g], rows, r, t, diag)
                    a_scr[g, rows, :] = a.astype(BF16)
                o32_ref[g] += jnp.dot(a_scr[g], v_ref[g, keys, :], preferred_element_type=F32)

        tile(qi, True)

        def walk(n, carry):
            tile(qi - 1 - n, False)
            return carry

        lax.fori_loop(0, qi, walk, 0)
        o_ref[...] = o32_ref[...].astype(o_ref.dtype)

    grp = SB_HEAD_GROUP
    assert h % grp == 0
    qs = pl.BlockSpec((grp, t, d), lambda hh, i: (hh, i, 0))
    ks = pl.BlockSpec((grp, s, d), lambda hh, i: (hh, 0, 0))
    tile_f32 = pltpu.VMEM((grp, t, t), F32)
    col_f32 = pltpu.VMEM((grp, t, 1), F32)
    return pl.pallas_call(
        body, name=name, grid=(h // grp, s // t), in_specs=[qs, ks, ks], out_specs=[qs, qs],
        out_shape=[jax.ShapeDtypeStruct(q.shape, BF16), jax.ShapeDtypeStruct(q.shape, F32)],
        scratch_shapes=[pltpu.VMEM((grp, t, d), BF16), pltpu.VMEM((2 * t, t), BF16), tile_f32,
                        pltpu.VMEM((grp, t, 2 * t), BF16), tile_f32, pltpu.VMEM((grp, t, t), BF16), col_f32],
        compiler_params=_params(dimension_semantics=("parallel", "parallel")),
    )(q, k, v)


def _sb_bwd(q, k, v, o32, do, dk0, dv0, *, name):
    h, s, d = q.shape
    t = _blk(s, SB_TILE)
    scale = _sb_scale(d)
    has_init = dk0 is not None
    n_in = 7 if has_init else 5

    def body(*refs):
        if has_init:
            q_ref, k_ref, v_ref, o_ref, do_ref, dk0_ref, dv0_ref, dq_ref, dk_ref, dv_ref = refs[:n_in + 3]
        else:
            q_ref, k_ref, v_ref, o_ref, do_ref, dq_ref, dk_ref, dv_ref = refs[:n_in + 3]
        (qs_scr, tri_scr, z_scr, beta_scr, nsplit_scr, tin_scr, da_scr, a_scr, g_scr, gsplit_scr, gin_scr, dz_scr,
         dq_scr, rsum_scr, gsum_scr, dsum_scr) = refs[n_in + 3:]
        qi = pl.program_id(1)

        @pl.when(qi == 0)
        def _():
            if has_init:
                dk_ref[...] = dk0_ref[...]
                dv_ref[...] = dv0_ref[...]
            else:
                dk_ref[...] = jnp.zeros_like(dk_ref)
                dv_ref[...] = jnp.zeros_like(dv_ref)

        qs_scr[...] = q_ref[...] * scale
        _sb_triangle(tri_scr, t)
        dsum_scr[...] = jnp.sum(o_ref[...] * do_ref[...].astype(F32), axis=2, keepdims=True)
        rsum_scr[...] = jnp.zeros_like(rsum_scr)
        gsum_scr[...] = jnp.zeros_like(gsum_scr)
        dq_scr[...] = jnp.zeros_like(dq_scr)

        def tile(kb, diag):
            keys = pl.ds(pl.multiple_of(kb * t, t), t)
            for g in range(grp):
                z_scr[g] = lax.dot_general(qs_scr[g], k_ref[g, keys, :], _NT, preferred_element_type=F32)
                da_scr[g] = lax.dot_general(do_ref[g], v_ref[g, keys, :], _NT, preferred_element_type=F32)
            for g in range(grp):
                _sb_logs_phase(z_scr.at[g], nsplit_scr.at[g], beta_scr.at[g], t, diag)
                tin_scr[g] = jnp.dot(nsplit_scr[g], tri_scr[...], preferred_element_type=F32)
            for g in range(grp):
                for r in range(0, t, SB_STRIP):
                    rows = pl.ds(r, SB_STRIP)
                    ab = _sb_probs(z_scr.at[g], tin_scr.at[g], rsum_scr.at[g], rows, r, t, diag).astype(BF16)
                    a_scr[g, rows, :] = ab
                    gv = ab.astype(F32) * da_scr[g, rows, :]
                    g_scr[g, rows, :] = gv
                    _store_split(gsplit_scr.at[g], rows, gv, t)
                gin_scr[g] = jnp.dot(gsplit_scr[g], tri_scr[...], preferred_element_type=F32)
                dv_ref[g, keys, :] += lax.dot_general(a_scr[g], do_ref[g], _TN, preferred_element_type=F32)
            for g in range(grp):
                for r in range(0, t, SB_STRIP):
                    rows = pl.ds(r, SB_STRIP)
                    gs = gsum_scr[g, rows, :]
                    gv = g_scr[g, rows, :]
                    dz = gv - beta_scr[g, rows, :] * ((gv - gin_scr[g, rows, :]) + (dsum_scr[g, rows, :] - gs))
                    if diag:
                        dz = jnp.where(_sb_strip_mask(r, t), dz, 0.0)
                    dz_scr[g, rows, :] = dz.astype(BF16)
                    gsum_scr[g, rows, :] = gs + gin_scr[g, rows, 0:1]
                dq_scr[g] += jnp.dot(dz_scr[g], k_ref[g, keys, :], preferred_element_type=F32)
                dk_ref[g, keys, :] += lax.dot_general(dz_scr[g], qs_scr[g], _TN, preferred_element_type=F32)

        tile(qi, True)

        def walk(n, carry):
            tile(qi - 1 - n, False)
            return carry

        lax.fori_loop(0, qi, walk, 0)
        dq_ref[...] = (dq_scr[...] * scale).astype(dq_ref.dtype)

    grp = SB_BWD_HEAD_GROUP
    assert h % grp == 0
    qs = pl.BlockSpec((grp, t, d), lambda hh, i: (hh, i, 0))
    ks = pl.BlockSpec((grp, s, d), lambda hh, i: (hh, 0, 0))
    in_specs = [qs, ks, ks, qs, qs] + ([ks, ks] if has_init else [])
    operands = [q, k, v, o32, do] + ([dk0, dv0] if has_init else [])
    acc = jax.ShapeDtypeStruct(q.shape, F32)
    tile_f32 = pltpu.VMEM((grp, t, t), F32)
    tile_bf16 = pltpu.VMEM((grp, t, t), BF16)
    split = pltpu.VMEM((grp, t, 2 * t), BF16)
    col_f32 = pltpu.VMEM((grp, t, 1), F32)
    return pl.pallas_call(
        body, name=name, grid=(h // grp, s // t), in_specs=in_specs, out_specs=[qs, ks, ks],
        out_shape=[jax.ShapeDtypeStruct(q.shape, BF16), acc, acc],
        scratch_shapes=[pltpu.VMEM((grp, t, d), BF16), pltpu.VMEM((2 * t, t), BF16), tile_f32, tile_f32, split,
                        tile_f32, tile_f32, tile_bf16, tile_f32, split, tile_f32, tile_bf16,
                        pltpu.VMEM((grp, t, d), F32), col_f32, col_f32, col_f32],
        compiler_params=_params(dimension_semantics=("parallel", "arbitrary")),
    )(*operands)


def _position():
    x, y, c = lax.axis_index("x"), lax.axis_index("y"), lax.axis_index("c")
    return x, y, c, [(1 - x, y), (x, 1 - y), (1 - x, 1 - y)]


_ANY = pl.BlockSpec(memory_space=pl.ANY)
N_PEER_CHIPS = N_CHIPS - 1


def _all_gather_chips(shards, *, name):
    n = len(shards)

    def body(*refs):
        ins, outs = refs[:n], refs[n:2 * n]
        send_sems, recv_sems, local_sems = refs[2 * n:]
        x, y, c, peers = _position()
        me = 2 * x + y
        copies = []
        for a in range(n):
            copies.append(pltpu.make_async_copy(ins[a], outs[a].at[me], local_sems.at[a]))
            for j, (px, py) in enumerate(peers):
                copies.append(pltpu.make_async_remote_copy(
                    src_ref=ins[a], dst_ref=outs[a].at[me],
                    send_sem=send_sems.at[a * N_PEER_CHIPS + j], recv_sem=recv_sems.at[a * N_PEER_CHIPS + j],
                    device_id=(px, py, c), device_id_type=MESH))
        for cp in copies:
            cp.start()
        for cp in copies:
            cp.wait()

    return pl.pallas_call(
        body, name=name, in_specs=[_ANY] * n, out_specs=[_ANY] * n,
        out_shape=[jax.ShapeDtypeStruct((N_CHIPS, *s.shape), s.dtype) for s in shards],
        scratch_shapes=[pltpu.SemaphoreType.DMA((n * N_PEER_CHIPS,)), pltpu.SemaphoreType.DMA((n * N_PEER_CHIPS,)),
                        pltpu.SemaphoreType.DMA((n,))],
        compiler_params=pltpu.CompilerParams(has_side_effects=True),
    )(*shards)


_HBM = pl.BlockSpec(memory_space=pltpu.HBM)
_SEM = pl.BlockSpec(memory_space=pltpu.SEMAPHORE)
_DATAFLOW = pltpu.SideEffectType.DATAFLOW_SIDE_EFFECTING
_TOKEN = jax.ShapeDtypeStruct((8, LANES), F32)


def _in_hbm(arr):
    return pltpu.with_memory_space_constraint(arr, pltpu.HBM)


def _gather_start(packs, lands, after, *, name):
    n = len(packs)

    def body(*refs):
        src, land = refs[:n], refs[n:2 * n]
        send_sems, recv_sems = refs[2 * n + 1:2 * n + 3]
        token = refs[-1]
        x, y, c, peers = _position()
        for i in range(n):
            for j, (px, py) in enumerate(peers):
                pltpu.make_async_remote_copy(
                    src_ref=src[i], dst_ref=land[i].at[2 * x + y],
                    send_sem=send_sems.at[N_PEER_CHIPS * i + j], recv_sem=recv_sems.at[N_PEER_CHIPS * i + j],
                    device_id=(px, py, c), device_id_type=MESH).start()
        token[...] = jnp.zeros_like(token)

    thru = [pltpu.HBM(a.shape, a.dtype) for a in (*packs, *lands)]
    outs = pl.pallas_call(
        body, name=name,
        out_shape=(pltpu.SemaphoreType.DMA((N_PEER_CHIPS * n,)), pltpu.SemaphoreType.DMA((N_PEER_CHIPS * n,)), *thru,
                   _TOKEN),
        in_specs=[_HBM] * (2 * n) + [_ANY],
        out_specs=(_SEM, _SEM, *[_HBM] * (2 * n), pl.BlockSpec(memory_space=pltpu.VMEM)),
        input_output_aliases={k: 2 + k for k in range(2 * n)},
        compiler_params=pltpu.CompilerParams(has_side_effects=_DATAFLOW),
    )(*[_in_hbm(a) for a in (*packs, *lands)], after)
    return outs[0], outs[1], list(outs[2:2 + n]), list(outs[2 + n:2 + 2 * n]), outs[-1]


def _gather_wait(first, packs, lands, send_sems, recv_sems, after, *, name):
    n = len(packs)

    def body(*refs):
        src, land = refs[:n], refs[n:2 * n]
        send_sems, recv_sems = refs[2 * n:2 * n + 2]
        _, _, c, peers = _position()
        for a in range(n):
            for j, (px, py) in enumerate(peers):
                sem = N_PEER_CHIPS * (first + a) + j
                cp = pltpu.make_async_remote_copy(
                    src_ref=src[a], dst_ref=land[a].at[2 * px + py], send_sem=send_sems.at[sem],
                    recv_sem=recv_sems.at[sem], device_id=(px, py, c), device_id_type=MESH)
                cp.wait_send()
                cp.wait_recv()

    outs = pl.pallas_call(
        body, name=name, out_shape=[pltpu.HBM(a.shape, a.dtype) for a in (*packs, *lands)],
        in_specs=[_HBM] * (2 * n) + [_SEM, _SEM, _ANY], out_specs=[_HBM] * (2 * n),
        input_output_aliases={k: k for k in range(2 * n)},
        compiler_params=pltpu.CompilerParams(has_side_effects=_DATAFLOW),
    )(*packs, *lands, send_sems, recv_sems, after)
    return list(outs[:n]), list(outs[n:])


def _scatter_start(gpacks, lands, *, name):
    n = len(gpacks)

    def body(*refs):
        src, land = refs[:n], refs[n:2 * n]
        send_sems, recv_sems = refs[2 * n:2 * n + 2]
        token = refs[-1]
        _, _, c, peers = _position()
        for a in range(n):
            for j, (px, py) in enumerate(peers):
                pltpu.make_async_remote_copy(
                    src_ref=src[a].at[2 * px + py], dst_ref=land[a].at[j], send_sem=send_sems.at[N_PEER_CHIPS * a + j],
                    recv_sem=recv_sems.at[N_PEER_CHIPS * a + j], device_id=(px, py, c), device_id_type=MESH).start()
        token[...] = jnp.zeros_like(token)

    thru = [pltpu.HBM(a.shape, a.dtype) for a in (*gpacks, *lands)]
    outs = pl.pallas_call(
        body, name=name,
        out_shape=(pltpu.SemaphoreType.DMA((N_PEER_CHIPS * n,)), pltpu.SemaphoreType.DMA((N_PEER_CHIPS * n,)), *thru,
                   _TOKEN),
        in_specs=[_HBM] * (2 * n), out_specs=(_SEM, _SEM, *[_HBM] * (2 * n), pl.BlockSpec(memory_space=pltpu.VMEM)),
        input_output_aliases={k: 2 + k for k in range(2 * n)},
        compiler_params=pltpu.CompilerParams(has_side_effects=_DATAFLOW),
    )(*[_in_hbm(a) for a in (*gpacks, *lands)])
    return outs[0], outs[1], list(outs[2:2 + n]), list(outs[2 + n:2 + 2 * n]), outs[-1]


def _scatter_wait(gpacks, lands, send_sems, recv_sems, after, *, name):
    n = len(gpacks)

    def body(*refs):
        src, land = refs[:n], refs[n:2 * n]
        send_sems, recv_sems = refs[2 * n:2 * n + 2]
        _, _, c, peers = _position()
        for a in range(n):
            for j, (px, py) in enumerate(peers):
                cp = pltpu.make_async_remote_copy(
                    src_ref=src[a].at[2 * px + py], dst_ref=land[a].at[j], send_sem=send_sems.at[N_PEER_CHIPS * a + j],
                    recv_sem=recv_sems.at[N_PEER_CHIPS * a + j], device_id=(px, py, c),
                    device_id_type=MESH)
                cp.wait_send()
                cp.wait_recv()

    outs = pl.pallas_call(
        body, name=name, out_shape=[pltpu.HBM(a.shape, a.dtype) for a in (*gpacks, *lands)],
        in_specs=[_HBM] * (2 * n) + [_SEM, _SEM, _ANY], out_specs=[_HBM] * (2 * n),
        input_output_aliases={k: k for k in range(2 * n)},
        compiler_params=pltpu.CompilerParams(has_side_effects=_DATAFLOW),
    )(*gpacks, *lands, send_sems, recv_sems, after)
    return list(outs[:n]), list(outs[n:])


def _sibling_exchange(arrs, *, name):
    n = len(arrs)

    def body(*refs):
        ins, outs = refs[:n], refs[n:2 * n]
        send_sems, recv_sems = refs[2 * n:]
        x, y, c, _ = _position()
        copies = [pltpu.make_async_remote_copy(
            src_ref=ins[a], dst_ref=outs[a], send_sem=send_sems.at[a], recv_sem=recv_sems.at[a],
            device_id=(x, y, 1 - c), device_id_type=MESH) for a in range(n)]
        for cp in copies:
            cp.start()
        for cp in copies:
            cp.wait()

    return pl.pallas_call(
        body, name=name, in_specs=[_ANY] * n, out_specs=[_ANY] * n,
        out_shape=[jax.ShapeDtypeStruct(a.shape, a.dtype) for a in arrs],
        scratch_shapes=[pltpu.SemaphoreType.DMA((n,)), pltpu.SemaphoreType.DMA((n,))],
        compiler_params=pltpu.CompilerParams(has_side_effects=True),
    )(*arrs)


def _small_all_reduce(v, *, name):
    r, cdim = v.shape

    def body(v_ref, o_ref, slots, send_sems, recv_sems):
        x, y, c, _ = _position()
        me = 4 * x + 2 * y + c
        slots[me] = v_ref[...]
        copies = []
        for j in range(1, N_DEVICES):
            peer = (x ^ ((j >> 2) & 1), y ^ ((j >> 1) & 1), c ^ (j & 1))
            copies.append(pltpu.make_async_remote_copy(
                src_ref=v_ref, dst_ref=slots.at[me], send_sem=send_sems.at[j - 1], recv_sem=recv_sems.at[j - 1],
                device_id=peer, device_id_type=MESH))
        for cp in copies:
            cp.start()
        for cp in copies:
            cp.wait()
        acc = slots[0]
        for dev in range(1, N_DEVICES):
            acc = acc + slots[dev]
        o_ref[...] = acc

    vm = pl.BlockSpec(memory_space=pltpu.VMEM)
    return pl.pallas_call(
        body, name=name, in_specs=[vm], out_specs=vm,
        out_shape=jax.ShapeDtypeStruct(v.shape, F32),
        scratch_shapes=[pltpu.VMEM((N_DEVICES, r, cdim), F32),
                        pltpu.SemaphoreType.DMA((N_DEVICES - 1,)), pltpu.SemaphoreType.DMA((N_DEVICES - 1,))],
        compiler_params=pltpu.CompilerParams(has_side_effects=True),
    )(v)


def _row_block(r, cap, mult):
    best = max(b for b in range(mult, cap + 1, mult) if r % b == 0)
    return best


def _sum_partials(chip_idx, own4, landed, *, name):
    _, r, c = landed.shape
    bm = _row_block(r, 512, 16)

    def body(chip_ref, own_ref, land_ref, o_ref):
        acc = own_ref[...].astype(F32)
        for j in range(N_PEER_CHIPS):
            acc = acc + land_ref[j].astype(F32)
        o_ref[...] = acc

    return pl.pallas_call(
        body, name=name,
        grid_spec=pltpu.PrefetchScalarGridSpec(
            num_scalar_prefetch=1, grid=(r // bm,),
            in_specs=[pl.BlockSpec((None, bm, c), lambda i, chip: (chip[0], i, 0)),
                      pl.BlockSpec((N_PEER_CHIPS, bm, c), lambda i, chip: (0, i, 0))],
            out_specs=pl.BlockSpec((bm, c), lambda i, chip: (i, 0))),
        out_shape=jax.ShapeDtypeStruct((r, c), F32),
        compiler_params=_params(dimension_semantics=("parallel",)),
    )(chip_idx, own4, landed)


def _adamw(w, m, v, g_a, g_b, *, name):
    r, c = w.shape
    bm = _blk(r, 256)
    two = g_b is not None
    bc1 = 1.0 - ADAM_B1 ** ADAM_STEP
    bc2 = 1.0 - ADAM_B2 ** ADAM_STEP

    def body(*refs):
        if two:
            w_ref, m_ref, v_ref, ga_ref, gb_ref, g_ref, d_ref, nm_ref, nv_ref = refs
            g = ga_ref[...] + gb_ref[...]
        else:
            w_ref, m_ref, v_ref, ga_ref, g_ref, d_ref, nm_ref, nv_ref = refs
            g = ga_ref[...]
        nm = ADAM_B1 * m_ref[...] + (1.0 - ADAM_B1) * g
        nv = ADAM_B2 * v_ref[...] + (1.0 - ADAM_B2) * (g * g)
        g_ref[...] = g
        nm_ref[...] = nm
        nv_ref[...] = nv
        d_ref[...] = -ADAM_LR * ((nm / bc1) / (jnp.sqrt(nv / bc2) + ADAM_EPS) + ADAM_WD * w_ref[...])

    spec = pl.BlockSpec((bm, c), lambda i: (i, 0))
    out = jax.ShapeDtypeStruct((r, c), F32)
    operands = [w, m, v, g_a] + ([g_b] if two else [])
    return pl.pallas_call(
        body, name=name, grid=(r // bm,),
        in_specs=[spec] * len(operands), out_specs=[spec] * 4, out_shape=[out] * 4,
        compiler_params=_params(dimension_semantics=("parallel",)),
    )(*operands)


def _heads(a):
    s, w = a.shape
    return a.reshape(s, w // HEAD_DIM, HEAD_DIM).transpose(1, 0, 2)


def _unheads(a):
    h, s, d = a.shape
    return a.transpose(1, 0, 2).reshape(s, h * d)


def _width_groups(shapes):
    groups = {}
    for idx, (_, c) in enumerate(shapes):
        groups.setdefault(c, []).append(idx)
    return list(groups.values())


def _pack(arrs, lead):
    groups = _width_groups([a.shape[-2:] for a in arrs])
    return [jnp.concatenate([arrs[k] for k in grp], axis=lead) for grp in groups]


def _unpack(bufs, shapes, lead):
    outs = [None] * len(shapes)
    for buf, grp in zip(bufs, _width_groups(shapes)):
        off = 0
        for k in grp:
            outs[k] = lax.slice_in_dim(buf, off, off + shapes[k][0], axis=lead)
            off += shapes[k][0]
    return outs


class LayerWeights:
    FIELDS = ("w_in", "wm", "wo", "wg", "wu", "wd", "wkv")
    STAGE = (0, 0, 1, 2, 2, 2, 2)

    def __init__(self):
        for f in self.FIELDS:
            setattr(self, f, None)


def _pack_small(mix, ffn, kvn, memn, fin, conv):
    d = mix.shape[-1]
    flat = conv.reshape(-1)
    rows_conv = SMALL_ROWS - 11
    flat = jnp.pad(flat, (0, rows_conv * d - flat.shape[0]))
    return jnp.concatenate([mix, ffn, kvn.reshape(1, d), memn.reshape(1, d), fin.reshape(1, d),
                            flat.reshape(rows_conv, d)], axis=0)


def _unpack_small(buf, conv_shape):
    n = math.prod(conv_shape)
    return (buf[0:4], buf[4:8], buf[8], buf[9], buf[10], buf[11:].reshape(-1)[:n].reshape(conv_shape))


def kernel(x, mem, mix_norm, a_in, conv_w, b_in, kv_norm, w_kv_shared, w_mem_kv, w_o, ffn_norm, w_gate, w_up, w_down, mem_norm, final_norm, loss_target, m_mix_norm, m_a_in, m_conv_w, m_b_in, m_kv_norm, m_w_kv_shared, m_w_mem_kv, m_w_o, m_ffn_norm, m_w_gate, m_w_up, m_w_down, m_mem_norm, m_final_norm, v_mix_norm, v_a_in, v_conv_w, v_b_in, v_kv_norm, v_w_kv_shared, v_w_mem_kv, v_w_o, v_ffn_norm, v_w_gate, v_w_up, v_w_down, v_mem_norm, v_final_norm):
    s, d = x.shape[1], x.shape[2]
    n_mem = mem.shape[1]
    depth = mix_norm.shape[0]
    n_a = a_in.shape[0]
    main_w = conv_w.shape[2] * N_CHIPS
    mem_w = w_mem_kv.shape[2] // 2
    ffn_c = w_gate.shape[2]
    kv_c = w_kv_shared.shape[1]
    a_c = a_in.shape[2]
    chip = 2 * lax.axis_index("x") + lax.axis_index("y")

    x0 = x[0]
    mem0 = mem[0]
    tgt = loss_target[0]
    bs = _blk(s, 1024)

    def layer_shards(i, a_or_b, others, kv):
        ws = [a_or_b[0][i] if i < n_a else a_or_b[1][i - n_a]] + [w[i] for w in others]
        return ws + ([kv] if i == n_a - 1 else [])

    shards = [layer_shards(i, (a_in, b_in), (w_mem_kv, w_o, w_gate, w_up, w_down), w_kv_shared) for i in range(depth)]
    shapes = [[w.shape for w in ws] for ws in shards]
    layer_packs = [_pack([w.astype(BF16) for w in ws], 0) for ws in shards]
    first = [sum(len(p) for p in layer_packs[:i]) for i in range(depth + 1)]
    packs = [p for ps in layer_packs for p in ps]
    conv_parts, = _all_gather_chips([conv_w], name="gather_conv_weights")
    conv_full = jnp.concatenate([conv_parts[kk] for kk in range(N_CHIPS)], axis=-1)
    lands = [lax.empty((N_CHIPS, *p.shape), BF16) for p in packs]
    send_sems, recv_sems, packs, lands, started = _gather_start(packs, lands, conv_parts,
                                                                name="gather_weights_start")

    def gain(vec):
        return vec.reshape(1, d)

    lw = [LayerWeights() for _ in range(depth)]

    def fetch(i, stage, after):
        groups = _width_groups(shapes[i])
        sel = [gi for gi, grp in enumerate(groups) if min(LayerWeights.STAGE[k] for k in grp) == stage]
        if not sel:
            return
        lo, hi = first[i] + sel[0], first[i] + sel[-1] + 1
        assert hi - lo == len(sel)
        own, landed = _gather_wait(lo, packs[lo:hi], lands[lo:hi], send_sems, recv_sems, after,
                                   name=f"gather_weights_wait_l{i}_s{stage}")
        for gi, mine, buf in zip(sel, own, landed):
            buf = lax.dynamic_update_slice_in_dim(buf, mine[None], chip, axis=0)
            off = 0
            for k in groups[gi]:
                rows = shapes[i][k][0]
                setattr(lw[i], LayerWeights.FIELDS[k], lax.slice_in_dim(buf, off, off + rows, axis=1))
                off += rows

    mem_n = _rms_fwd(mem0, gain(mem_norm), name="mem_norm_fwd", dep=started)
    saved = []
    k_sh = v_sh = hk = x_kv = None
    xc = x0
    for i in range(depth):
        st = {"x_in": xc}
        h = _rms_fwd(xc, gain(mix_norm[i]), name="mix_norm_fwd")
        fetch(i, 0, h)
        mkv = _mm(Op(mem_n), Op(lw[i].wm, 'r'), name="mem_kv_proj", bm=n_mem, bn=2 * mem_w, bk=d,
                  out_dtype=BF16)
        mem_k, mem_v = _heads(mkv[:, :mem_w]), _heads(mkv[:, mem_w:])
        if i < n_a:
            p = _mm(Op(h), Op(lw[i].w_in, 'c'), name="a_in_proj", bm=bs, bn=a_c, bk=d, out_dtype=BF16)
            y_main = _conv_fwd(p, conv_full[i], name="conv_fwd")
            q_mem = _heads(p[:, 3 * main_w:])
        else:
            p = _mm(Op(h), Op(lw[i].w_in, 'r'), name="b_in_proj", bm=bs, bn=d, bk=d, out_dtype=BF16)
            q_sb = _heads(p[:, :main_w])
            o_sb, o_sb32 = _sb_fwd(q_sb, k_sh, v_sh, name="sb_fwd")
            y_main = _unheads(o_sb)
            q_mem = _heads(p[:, main_w:])
            st.update(q_sb=q_sb, o_sb32=o_sb32)
        y_mem = _mem_fwd(q_mem, mem_k, mem_v, name="mem_attn_fwd")
        y = jnp.concatenate([y_main, _unheads(y_mem)], axis=-1)
        fetch(i, 1, y)
        x_mid = _mm(Op(y), Op(lw[i].wo, 'r'), name="w_o_proj", bm=bs, bn=d, bk=d, out_dtype=F32,
                    res=Op(xc))
        h2 = _rms_fwd(x_mid, gain(ffn_norm[i]), name="ffn_norm_fwd")
        fetch(i, 2, h2)
        gate = _mm(Op(h2), Op(lw[i].wg, 'c'), name="w_gate_proj", bm=bs, bn=ffn_c, bk=d, out_dtype=BF16,
                   out_chunk='c')
        up = _mm(Op(h2), Op(lw[i].wu, 'c'), name="w_up_proj", bm=bs, bn=ffn_c, bk=d, out_dtype=BF16,
                 out_chunk='c')
        act = _swiglu_fwd(gate, up, name="swiglu_fwd")
        xc = _mm(Op(act, 'c'), Op(lw[i].wd, 'r'), name="w_down_proj", bm=bs, bn=d, bk=ffn_c, out_dtype=F32,
                 res=Op(x_mid))
        st.update(h=h, p=p, mem_k=mem_k, mem_v=mem_v, q_mem=q_mem, y=y, x_mid=x_mid, h2=h2, gate=gate, up=up,
                  act=act)
        saved.append(st)
        if i == n_a - 1:
            x_kv = xc
            hk = _rms_fwd(xc, gain(kv_norm), name="kv_norm_fwd")
            kv = _mm(Op(hk), Op(lw[n_a - 1].wkv, 'c'), name="kv_proj", bm=bs, bn=kv_c, bk=d, out_dtype=BF16)
            k_sh, v_sh = _heads(kv[:, :main_w]), _heads(kv[:, main_w:])

    dx, dg_final, loss_part = _final_loss(xc, gain(final_norm), tgt, name="final_norm_loss")
    loss = lax.psum(loss_part[0, 0], ("x", "y", "c"))

    g_a, g_b, g_m, g_o, g_g, g_u, g_d = ([None] * n_a, [None] * (depth - n_a), [None] * depth, [None] * depth,
                                         [None] * depth, [None] * depth, [None] * depth)
    dg_mix, dg_ffn, dconv = [None] * depth, [None] * depth, [None] * n_a
    in_flight = [None] * depth
    dk_sh = dv_sh = None
    dmem_n = None
    g_kv = dg_kv = None
    for i in reversed(range(depth)):
        st = saved[i]
        dact = _mm(Op(dx), Op(lw[i].wd, 'r'), name="w_down_dact", tb=True, bm=bs, bn=ffn_c, bk=d, out_dtype=BF16,
                   out_chunk='c')
        g_d[i] = _mm(Op(st["act"], 'c'), Op(dx), name="w_down_grad", ta=True, bm=ffn_c, bn=d, bk=bs,
                     out_dtype=BF16, out_chunk='r')
        dgate, dup = _swiglu_bwd(dact, st["gate"], st["up"], name="swiglu_bwd")
        g_g[i] = _mm(Op(st["h2"]), Op(dgate, 'c'), name="w_gate_grad", ta=True, bm=d, bn=ffn_c, bk=bs,
                     out_dtype=BF16, out_chunk='c')
        g_u[i] = _mm(Op(st["h2"]), Op(dup, 'c'), name="w_up_grad", ta=True, bm=d, bn=ffn_c, bk=bs,
                     out_dtype=BF16, out_chunk='c')
        dh2 = _mm(Op(dgate, 'c'), Op(lw[i].wg, 'c'), name="w_gate_dh", tb=True, bm=bs, bn=d, bk=ffn_c,
                  out_dtype=F32)
        dh2 = _mm(Op(dup, 'c'), Op(lw[i].wu, 'c'), name="w_up_dh", tb=True, bm=bs, bn=d, bk=ffn_c, out_dtype=F32,
                  res=Op(dh2))
        dx_mid, dg_ffn[i] = _rms_bwd(st["x_mid"], gain(ffn_norm[i]), dh2, dx, name="ffn_norm_bwd")
        dy = _mm(Op(dx_mid), Op(lw[i].wo, 'r'), name="w_o_dy", tb=True, bm=bs, bn=d, bk=d,
                 out_dtype=BF16)
        g_o[i] = _mm(Op(st["y"]), Op(dx_mid), name="w_o_grad", ta=True, bm=d // N_CHIPS, bn=d, bk=bs,
                     out_dtype=BF16, out_chunk='r')
        dq_mem, dmk, dmv = _mem_bwd(st["q_mem"], st["mem_k"], st["mem_v"], _heads(dy[:, main_w:]),
                                    name="mem_attn_bwd")
        dmkv = jnp.concatenate([_unheads(dmk), _unheads(dmv)], axis=-1)
        g_m[i] = _mm(Op(mem_n), Op(dmkv), name="mem_kv_grad", ta=True, bm=d // N_CHIPS, bn=2 * mem_w, bk=n_mem,
                     out_dtype=BF16, out_chunk='r')
        dmem_n = _mm(Op(dmkv), Op(lw[i].wm, 'r'), name="mem_kv_dmem", tb=True, bm=n_mem, bn=d,
                     bk=2 * mem_w, out_dtype=F32, res=None if dmem_n is None else Op(dmem_n))
        if i < n_a:
            db, dc, du, dconv[i] = _conv_bwd(st["p"], conv_full[i], dy[:, :main_w], name="conv_bwd")
            dp = jnp.concatenate([db, dc, du, _unheads(dq_mem)], axis=-1)
            g_a[i] = _mm(Op(st["h"]), Op(dp), name="a_in_grad", ta=True, bm=d, bn=a_c, bk=bs, out_dtype=BF16,
                         out_chunk='c')
            dh = _mm(Op(dp), Op(lw[i].w_in, 'c'), name="a_in_dh", tb=True, bm=bs, bn=d, bk=a_c, out_dtype=F32)
        else:
            dq_sb, dk_sh, dv_sh = _sb_bwd(st["q_sb"], k_sh, v_sh, st["o_sb32"], _heads(dy[:, :main_w]),
                                          dk_sh, dv_sh, name="sb_bwd")
            dp = jnp.concatenate([_unheads(dq_sb), _unheads(dq_mem)], axis=-1)
            g_b[i - n_a] = _mm(Op(st["h"]), Op(dp), name="b_in_grad", ta=True, bm=d // N_CHIPS, bn=d, bk=bs,
                               out_dtype=BF16, out_chunk='r')
            dh = _mm(Op(dp), Op(lw[i].w_in, 'r'), name="b_in_dh", tb=True, bm=bs, bn=d, bk=d,
                     out_dtype=F32)
        grads = [g_a[i] if i < n_a else g_b[i - n_a], g_m[i], g_o[i], g_g[i], g_u[i], g_d[i]]
        grads += [g_kv] if i == n_a - 1 else []
        gpacks = _pack(grads, 1)
        in_flight[i] = _scatter_start(gpacks, [lax.empty((N_PEER_CHIPS, *g.shape[1:]), BF16) for g in gpacks],
                                      name=f"scatter_grads_start_l{i}")
        dx, dg_mix[i] = _rms_bwd(st["x_in"], gain(mix_norm[i]), dh, dx_mid, name="mix_norm_bwd",
                                 dep=in_flight[i][4])
        if i == n_a:
            dkv = jnp.concatenate([_unheads(dk_sh), _unheads(dv_sh)], axis=-1)
            g_kv = _mm(Op(hk), Op(dkv), name="kv_grad", ta=True, bm=d, bn=kv_c, bk=bs, out_dtype=BF16,
                       out_chunk='c')
            dhk = _mm(Op(dkv), Op(lw[n_a - 1].wkv, 'c'), name="kv_dh", tb=True, bm=bs, bn=d, bk=kv_c, out_dtype=F32)
            dx, dg_kv = _rms_bwd(x_kv, gain(kv_norm), dhk, dx, name="kv_norm_bwd")
    _, dg_mem = _rms_bwd(mem0, gain(mem_norm), dmem_n, None, name="mem_norm_bwd")

    chip_idx = chip.astype(jnp.int32).reshape(1)
    core_sums = []
    for i in range(depth):
        ssem, rsem, gthru, lthru, _ = in_flight[i]
        gthru, lthru = _scatter_wait(gthru, lthru, ssem, rsem, dx, name=f"scatter_grads_wait_l{i}")
        core_sums += [_sum_partials(chip_idx, g, l, name="sum_chip_partials") for g, l in zip(gthru, lthru)]
    sibling_sums = _sibling_exchange(core_sums, name="exchange_core_sums")
    own_parts = [_unpack(core_sums[first[i]:first[i + 1]], shapes[i], 0) for i in range(depth)]
    sib_parts = [_unpack(sibling_sums[first[i]:first[i + 1]], shapes[i], 0) for i in range(depth)]

    def stacked(parts, pos, layers):
        return jnp.concatenate([parts[i][pos] for i in layers], axis=0)

    a_layers, b_layers, all_layers = range(n_a), range(n_a, depth), range(depth)
    big = [("a_in", a_in, m_a_in, v_a_in, 0, a_layers), ("b_in", b_in, m_b_in, v_b_in, 0, b_layers),
           ("w_kv_shared", w_kv_shared, m_w_kv_shared, v_w_kv_shared, 6, [n_a - 1]),
           ("w_mem_kv", w_mem_kv, m_w_mem_kv, v_w_mem_kv, 1, all_layers), ("w_o", w_o, m_w_o, v_w_o, 2, all_layers),
           ("w_gate", w_gate, m_w_gate, v_w_gate, 3, all_layers), ("w_up", w_up, m_w_up, v_w_up, 4, all_layers),
           ("w_down", w_down, m_w_down, v_w_down, 5, all_layers)]
    results = {}
    for wname, w, mm_, vv_, pos, layers in big:
        flat = lambda t: t.reshape(-1, t.shape[-1])
        outs = _adamw(flat(w), flat(mm_), flat(vv_), stacked(own_parts, pos, layers), stacked(sib_parts, pos, layers),
                      name="adamw")
        results[wname] = [o.reshape(w.shape) for o in outs]

    small_g = _pack_small(jnp.concatenate(dg_mix, axis=0), jnp.concatenate(dg_ffn, axis=0), dg_kv, dg_mem,
                          dg_final, jnp.stack(dconv, axis=0))
    small_g = _small_all_reduce(small_g, name="all_reduce_small_grads")
    conv_shape_full = (n_a, CONV_TAPS, main_w)
    gs = list(_unpack_small(small_g, conv_shape_full))
    gs[5] = lax.dynamic_slice_in_dim(gs[5], chip * conv_w.shape[2], conv_w.shape[2], axis=2)
    small_outs = _adamw(_pack_small(mix_norm, ffn_norm, kv_norm, mem_norm, final_norm, conv_w),
                        _pack_small(m_mix_norm, m_ffn_norm, m_kv_norm, m_mem_norm, m_final_norm, m_conv_w),
                        _pack_small(v_mix_norm, v_ffn_norm, v_kv_norm, v_mem_norm, v_final_norm, v_conv_w),
                        _pack_small(*gs), None, name="adamw_small")
    small_names = ["mix_norm", "ffn_norm", "kv_norm", "mem_norm", "final_norm", "conv_w"]
    for kind, buf in enumerate(small_outs):
        for wname, val in zip(small_names, _unpack_small(buf, conv_w.shape)):
            results.setdefault(wname, [None] * 4)[kind] = val

    order = ["mix_norm", "a_in", "conv_w", "b_in", "kv_norm", "w_kv_shared", "w_mem_kv", "w_o", "ffn_norm",
             "w_gate", "w_up", "w_down", "mem_norm", "final_norm"]
    return (loss, dx[None], *[results[nm][0] for nm in order], *[results[nm][1] for nm in order],
            *[results[nm][2] for nm in order], *[results[nm][3] for nm in order])
```

```python
import math
from typing import NamedTuple, Optional

import jax
import jax.numpy as jnp
from jax import lax
from jax.experimental import pallas as pl
from jax.experimental.pallas import tpu as pltpu

F32 = jnp.float32
BF16 = jnp.bfloat16
MESH = pl.DeviceIdType.MESH

N_CHIPS = 4
N_DEVICES = 8
HEAD_DIM = 64
CONV_TAPS = 3
NORM_EPS = 1e-6
V7X_VMEM_BYTES = 64 * 1024 * 1024
VMEM_LIMIT = V7X_VMEM_BYTES - 8 * 1024 * 1024
LANES = 128
SMALL_ROWS = 16

ADAM_LR = 0.001
ADAM_B1 = 0.9
ADAM_B2 = 0.999
ADAM_EPS = 1e-08
ADAM_WD = 0.01
ADAM_STEP = 10


def _params(**kw):
    return pltpu.CompilerParams(vmem_limit_bytes=VMEM_LIMIT, **kw)


def _blk(n, pref):
    b = min(n, pref)
    assert n % b == 0, (n, pref)
    return b


class Op(NamedTuple):
    arr: jax.Array
    chunk: Optional[str] = None
    layer: Optional[int] = None


def _op_spec(op_chunk, op_layer, shape2, br, bc, pick):
    r, c = shape2
    lead = () if op_layer is None else (op_layer,)
    none = (None,) * len(lead)
    if op_chunk is None:
        def imap(i, j, k):
            rb, cb = pick(i, j, k)
            return (*lead, rb, cb)
        return pl.BlockSpec((*none, br, bc), imap)
    if op_chunk == 'r' and br == N_CHIPS * r:
        def imap(i, j, k):
            rb, cb = pick(i, j, k)
            return (0, *lead, 0, cb)
        return pl.BlockSpec((N_CHIPS, *none, r, bc), imap)
    if op_chunk == 'r':
        n = r // br
        assert r % br == 0

        def imap(i, j, k):
            rb, cb = pick(i, j, k)
            return (rb // n, *lead, rb % n, cb)
        return pl.BlockSpec((None, *none, br, bc), imap)
    n = c // bc
    assert c % bc == 0

    def imap(i, j, k):
        rb, cb = pick(i, j, k)
        return (cb // n, *lead, rb, cb % n)
    return pl.BlockSpec((None, *none, br, bc), imap)


def _mm(a, b, *, name, ta=False, tb=False, bm, bn, bk, out_dtype, out_chunk=None, res=None):
    def dims(op):
        r, c = op.arr.shape[-2:]
        return (r * N_CHIPS if op.chunk == 'r' else r, c * N_CHIPS if op.chunk == 'c' else c)

    ar, ac = dims(a)
    br_, bc_ = dims(b)
    m, ka = (ac, ar) if ta else (ar, ac)
    kb, n = (bc_, br_) if tb else (br_, bc_)
    assert ka == kb, (name, ka, kb)
    assert m % bm == 0 and n % bn == 0 and ka % bk == 0, (name, m, n, ka, bm, bn, bk)
    nk = ka // bk

    if ta:
        a_spec = _op_spec(a.chunk, a.layer, a.arr.shape[-2:], bk, bm, lambda i, j, k: (k, i))
    else:
        a_spec = _op_spec(a.chunk, a.layer, a.arr.shape[-2:], bm, bk, lambda i, j, k: (i, k))
    if tb:
        b_spec = _op_spec(b.chunk, b.layer, b.arr.shape[-2:], bn, bk, lambda i, j, k: (j, k))
    else:
        b_spec = _op_spec(b.chunk, b.layer, b.arr.shape[-2:], bk, bn, lambda i, j, k: (k, j))

    if out_chunk == 'r':
        out_shape2 = (m // N_CHIPS, n)
    elif out_chunk == 'c':
        out_shape2 = (m, n // N_CHIPS)
    else:
        out_shape2 = (m, n)
    o_spec = _op_spec(out_chunk, None, out_shape2, bm, bn, lambda i, j, k: (i, j))
    out_full = out_shape2 if out_chunk is None else (N_CHIPS, *out_shape2)

    contract = (((0 if ta else 1,), (1 if tb else 0,)), ((), ()))
    has_res = res is not None

    def block2(ref):
        v = ref[...]
        return v.reshape(-1, v.shape[-1]).astype(BF16)

    def body(*refs):
        r_ref = refs[2] if has_res else None
        a_ref, b_ref = refs[:2]
        o_ref = refs[3 if has_res else 2]
        prod = lax.dot_general(block2(a_ref), block2(b_ref), contract, preferred_element_type=F32)
        if nk == 1:
            if has_res:
                prod = prod + r_ref[...].astype(F32)
            o_ref[...] = prod.astype(o_ref.dtype)
            return
        acc_ref = refs[-1]
        k = pl.program_id(2)

        @pl.when(k == 0)
        def _():
            acc_ref[...] = prod

        @pl.when(k > 0)
        def _():
            acc_ref[...] += prod

        @pl.when(k == nk - 1)
        def _():
            acc = acc_ref[...]
            if has_res:
                acc = acc + r_ref[...].astype(F32)
            o_ref[...] = acc.astype(o_ref.dtype)

    in_specs = [a_spec, b_spec]
    operands = [a.arr, b.arr]
    if has_res:
        in_specs.append(_op_spec(res.chunk, res.layer, res.arr.shape[-2:], bm, bn, lambda i, j, k: (i, j)))
        operands.append(res.arr)
    return pl.pallas_call(
        body, name=name, grid=(m // bm, n // bn, nk),
        in_specs=in_specs, out_specs=o_spec,
        out_shape=jax.ShapeDtypeStruct(out_full, out_dtype),
        scratch_shapes=[pltpu.VMEM((bm, bn), F32)] if nk > 1 else [],
        compiler_params=_params(dimension_semantics=("parallel", "parallel", "arbitrary")),
    )(*operands)


def _rms_fwd(x, g, *, name, dep=None):
    r, d = x.shape
    bm = _blk(r, 512)

    def body(x_ref, g_ref, *rest):
        o_ref = rest[-1]
        xv = x_ref[...]
        rstd = lax.rsqrt(jnp.mean(xv * xv, axis=-1, keepdims=True) + NORM_EPS)
        o_ref[...] = ((xv * rstd) * g_ref[...]).astype(o_ref.dtype)

    deps = [] if dep is None else [dep]
    return pl.pallas_call(
        body, name=name, grid=(r // bm,),
        in_specs=[pl.BlockSpec((bm, d), lambda i: (i, 0)), pl.BlockSpec((1, d), lambda i: (0, 0))]
        + [pl.BlockSpec(memory_space=pl.ANY)] * len(deps),
        out_specs=pl.BlockSpec((bm, d), lambda i: (i, 0)),
        out_shape=jax.ShapeDtypeStruct((r, d), BF16),
        compiler_params=_params(dimension_semantics=("parallel",)),
    )(x, g, *deps)


def _rms_bwd(x, g, dh, dres, *, name, dep=None):
    r, d = x.shape
    bm = _blk(r, 512)
    has_res = dres is not None
    deps = [] if dep is None else [dep]

    def body(*refs):
        if has_res:
            x_ref, g_ref, dh_ref, dres_ref = refs[:4]
        else:
            x_ref, g_ref, dh_ref = refs[:3]
        dx_ref, dg_ref = refs[-2:]

        @pl.when(pl.program_id(0) == 0)
        def _():
            dg_ref[...] = jnp.zeros_like(dg_ref)

        xv = x_ref[...]
        rstd = lax.rsqrt(jnp.mean(xv * xv, axis=-1, keepdims=True) + NORM_EPS)
        xh = xv * rstd
        dhv = dh_ref[...].astype(F32)
        dg_ref[...] += jnp.sum(dhv * xh, axis=0, keepdims=True)
        dxh = dhv * g_ref[...]
        dx = rstd * (dxh - xh * jnp.mean(dxh * xh, axis=-1, keepdims=True))
        if has_res:
            dx = dres_ref[...] + dx
        dx_ref[...] = dx

    row = pl.BlockSpec((bm, d), lambda i: (i, 0))
    vec = pl.BlockSpec((1, d), lambda i: (0, 0))
    in_specs = [row, vec, row] + ([row] if has_res else []) + [pl.BlockSpec(memory_space=pl.ANY)] * len(deps)
    operands = [x, g, dh] + ([dres] if has_res else []) + deps
    return pl.pallas_call(
        body, name=name, grid=(r // bm,),
        in_specs=in_specs, out_specs=[row, vec],
        out_shape=[jax.ShapeDtypeStruct((r, d), F32), jax.ShapeDtypeStruct((1, d), F32)],
        compiler_params=_params(dimension_semantics=("arbitrary",)),
    )(*operands)


def _final_loss(x, g, tgt, *, name):
    r, d = x.shape
    bm = _blk(r, 512)

    def body(x_ref, g_ref, t_ref, dx_ref, dg_ref, loss_ref):
        @pl.when(pl.program_id(0) == 0)
        def _():
            dg_ref[...] = jnp.zeros_like(dg_ref)
            loss_ref[...] = jnp.zeros_like(loss_ref)

        xv = x_ref[...]
        gv = g_ref[...]
        rstd = lax.rsqrt(jnp.mean(xv * xv, axis=-1, keepdims=True) + NORM_EPS)
        xh = xv * rstd
        diff = xh * gv - t_ref[...]
        loss_ref[...] += jnp.sum(diff * diff) * (0.5 / d)
        dy = diff * (1.0 / d)
        dg_ref[...] += jnp.sum(dy * xh, axis=0, keepdims=True)
        dxh = dy * gv
        dx_ref[...] = rstd * (dxh - xh * jnp.mean(dxh * xh, axis=-1, keepdims=True))

    row = pl.BlockSpec((bm, d), lambda i: (i, 0))
    vec = pl.BlockSpec((1, d), lambda i: (0, 0))
    return pl.pallas_call(
        body, name=name, grid=(r // bm,),
        in_specs=[row, vec, row],
        out_specs=[row, vec, pl.BlockSpec((1, LANES), lambda i: (0, 0))],
        out_shape=[jax.ShapeDtypeStruct((r, d), F32), jax.ShapeDtypeStruct((1, d), F32),
                   jax.ShapeDtypeStruct((1, LANES), F32)],
        compiler_params=_params(dimension_semantics=("arbitrary",)),
    )(x, g, tgt)


def _shift_down(v, k, row):
    return jnp.where(row >= k, pltpu.roll(v, k, 0), 0.0)


def _shift_up(v, k, row, s):
    return jnp.where(row < s - k, pltpu.roll(v, s - k, 0), 0.0)


def _conv_fwd(p, w, *, name):
    s = p.shape[0]
    width = w.shape[1]
    nb = width // LANES

    def body(b_ref, c_ref, u_ref, w_ref, y_ref):
        cu = c_ref[...].astype(F32) * u_ref[...].astype(F32)
        row = lax.broadcasted_iota(jnp.int32, cu.shape, 0)
        wv = w_ref[...]
        conv = wv[2:3] * cu + wv[1:2] * _shift_down(cu, 1, row) + wv[0:1] * _shift_down(cu, 2, row)
        y_ref[...] = (b_ref[...].astype(F32) * conv).astype(y_ref.dtype)

    def col(o):
        return pl.BlockSpec((s, LANES), lambda j: (0, j + o * nb))

    return pl.pallas_call(
        body, name=name, grid=(nb,),
        in_specs=[col(0), col(1), col(2), pl.BlockSpec((CONV_TAPS, LANES), lambda j: (0, j))],
        out_specs=col(0),
        out_shape=jax.ShapeDtypeStruct((s, width), BF16),
        compiler_params=_params(dimension_semantics=("parallel",)),
    )(p, p, p, w)


def _conv_bwd(p, w, dy, *, name):
    s = p.shape[0]
    width = w.shape[1]
    nb = width // LANES

    def body(b_ref, c_ref, u_ref, w_ref, dy_ref, db_ref, dc_ref, du_ref, dw_ref):
        bv = b_ref[...].astype(F32)
        cv = c_ref[...].astype(F32)
        uv = u_ref[...].astype(F32)
        dyv = dy_ref[...].astype(F32)
        cu = cv * uv
        row = lax.broadcasted_iota(jnp.int32, cu.shape, 0)
        wv = w_ref[...]
        cu1 = _shift_down(cu, 1, row)
        cu2 = _shift_down(cu, 2, row)
        conv = wv[2:3] * cu + wv[1:2] * cu1 + wv[0:1] * cu2
        db_ref[...] = (dyv * conv).astype(db_ref.dtype)
        dconv = dyv * bv
        dcu = wv[2:3] * dconv + wv[1:2] * _shift_up(dconv, 1, row, s) + wv[0:1] * _shift_up(dconv, 2, row, s)
        dc_ref[...] = (dcu * uv).astype(dc_ref.dtype)
        du_ref[...] = (dcu * cv).astype(du_ref.dtype)
        dw_ref[0:1, :] = jnp.sum(dconv * cu2, axis=0, keepdims=True)
        dw_ref[1:2, :] = jnp.sum(dconv * cu1, axis=0, keepdims=True)
        dw_ref[2:3, :] = jnp.sum(dconv * cu, axis=0, keepdims=True)

    def col(o):
        return pl.BlockSpec((s, LANES), lambda j: (0, j + o * nb))

    wspec = pl.BlockSpec((CONV_TAPS, LANES), lambda j: (0, j))
    act = jax.ShapeDtypeStruct((s, width), BF16)
    return pl.pallas_call(
        body, name=name, grid=(nb,),
        in_specs=[col(0), col(1), col(2), wspec, col(0)],
        out_specs=[col(0), col(0), col(0), wspec],
        out_shape=[act, act, act, jax.ShapeDtypeStruct((CONV_TAPS, width), F32)],
        compiler_params=_params(dimension_semantics=("parallel",)),
    )(p, p, p, w, dy)


FFN_STRIP = 128


def _ffn_in(h2, wg, wu, *, name):
    s, d = h2.shape
    nc, _, f = wg.shape
    bm = _blk(s, 1024)

    def body(h_ref, wg_ref, wu_ref, g_ref, u_ref, a_ref, g_scr, u_scr):
        hv = h_ref[...]
        g_scr[...] = jnp.dot(hv, wg_ref[...], preferred_element_type=F32)
        u_scr[...] = jnp.dot(hv, wu_ref[...], preferred_element_type=F32)
        for r in range(0, bm, FFN_STRIP):
            rows = pl.ds(r, min(FFN_STRIP, bm))
            gv, uv = g_scr[rows, :], u_scr[rows, :]
            g_ref[rows, :] = gv.astype(BF16)
            u_ref[rows, :] = uv.astype(BF16)
            a_ref[rows, :] = (gv * jax.nn.sigmoid(gv) * uv).astype(BF16)

    wspec = pl.BlockSpec((None, d, f), lambda i, j: (j, 0, 0))
    ospec = pl.BlockSpec((None, bm, f), lambda i, j: (j, i, 0))
    out = jax.ShapeDtypeStruct((nc, s, f), BF16)
    return pl.pallas_call(
        body, name=name, grid=(s // bm, nc),
        in_specs=[pl.BlockSpec((bm, d), lambda i, j: (i, 0)), wspec, wspec], out_specs=[ospec, ospec, ospec],
        out_shape=[out, out, out], scratch_shapes=[pltpu.VMEM((bm, f), F32), pltpu.VMEM((bm, f), F32)],
        compiler_params=_params(dimension_semantics=("parallel", "parallel")),
    )(h2, wg, wu)


def _ffn_dact(dx, wd, gate, up, *, name):
    s, d = dx.shape
    nc, f, _ = wd.shape
    bm = _blk(s, 1024)

    def body(dx_ref, wd_ref, g_ref, u_ref, dg_ref, du_ref, d_scr):
        d_scr[...] = lax.dot_general(dx_ref[...].astype(BF16), wd_ref[...], _NT, preferred_element_type=F32)
        for r in range(0, bm, FFN_STRIP):
            rows = pl.ds(r, min(FFN_STRIP, bm))
            gv = g_ref[rows, :].astype(F32)
            dv = d_scr[rows, :]
            sg = jax.nn.sigmoid(gv)
            dg_ref[rows, :] = (dv * u_ref[rows, :].astype(F32) * (sg * (1.0 + gv * (1.0 - sg)))).astype(BF16)
            du_ref[rows, :] = (dv * (gv * sg)).astype(BF16)

    cspec = pl.BlockSpec((None, bm, f), lambda i, j: (j, i, 0))
    out = jax.ShapeDtypeStruct((nc, s, f), BF16)
    return pl.pallas_call(
        body, name=name, grid=(s // bm, nc),
        in_specs=[pl.BlockSpec((bm, d), lambda i, j: (i, 0)), pl.BlockSpec((None, f, d), lambda i, j: (j, 0, 0)),
                  cspec, cspec],
        out_specs=[cspec, cspec], out_shape=[out, out], scratch_shapes=[pltpu.VMEM((bm, f), F32)],
        compiler_params=_params(dimension_semantics=("parallel", "parallel")),
    )(dx, wd, gate, up)


_NT = (((1,), (1,)), ((), ()))
_TN = (((0,), (0,)), ((), ()))


def _mem_probs(q, k, scale):
    s = lax.dot_general(q, k, _NT, preferred_element_type=F32) * scale
    e = jnp.exp(s - jnp.max(s, axis=-1, keepdims=True))
    return e / jnp.sum(e, axis=-1, keepdims=True)


def _mem_fwd(q, k, v, *, name):
    h, s, d = q.shape
    m = k.shape[1]
    bq = _blk(s, 1024)
    scale = 1.0 / math.sqrt(d)

    def body(q_ref, k_ref, v_ref, o_ref):
        p = _mem_probs(q_ref[...], k_ref[...], scale)
        o_ref[...] = jnp.dot(p.astype(BF16), v_ref[...], preferred_element_type=F32).astype(o_ref.dtype)

    qs = pl.BlockSpec((None, bq, d), lambda hh, i: (hh, i, 0))
    ks = pl.BlockSpec((None, m, d), lambda hh, i: (hh, 0, 0))
    return pl.pallas_call(
        body, name=name, grid=(h, s // bq), in_specs=[qs, ks, ks], out_specs=qs,
        out_shape=jax.ShapeDtypeStruct(q.shape, BF16),
        compiler_params=_params(dimension_semantics=("parallel", "parallel")),
    )(q, k, v)


def _mem_bwd(q, k, v, do, *, name):
    h, s, d = q.shape
    m = k.shape[1]
    bq = _blk(s, 1024)
    scale = 1.0 / math.sqrt(d)

    def body(q_ref, k_ref, v_ref, do_ref, dq_ref, dk_ref, dv_ref):
        @pl.when(pl.program_id(1) == 0)
        def _():
            dk_ref[...] = jnp.zeros_like(dk_ref)
            dv_ref[...] = jnp.zeros_like(dv_ref)

        qv = q_ref[...]
        kv = k_ref[...]
        dov = do_ref[...]
        p = _mem_probs(qv, kv, scale)
        pb = p.astype(BF16)
        dp = lax.dot_general(dov, v_ref[...], _NT, preferred_element_type=F32)
        pf = pb.astype(F32)
        ds = (pf * (dp - jnp.sum(pf * dp, axis=-1, keepdims=True)) * scale).astype(BF16)
        dq_ref[...] = jnp.dot(ds, kv, preferred_element_type=F32).astype(dq_ref.dtype)
        dk_ref[...] += lax.dot_general(ds, qv, _TN, preferred_element_type=F32)
        dv_ref[...] += lax.dot_general(pb, dov, _TN, preferred_element_type=F32)

    qs = pl.BlockSpec((None, bq, d), lambda hh, i: (hh, i, 0))
    ks = pl.BlockSpec((None, m, d), lambda hh, i: (hh, 0, 0))
    kvout = jax.ShapeDtypeStruct(k.shape, F32)
    return pl.pallas_call(
        body, name=name, grid=(h, s // bq), in_specs=[qs, ks, ks, qs], out_specs=[qs, ks, ks],
        out_shape=[jax.ShapeDtypeStruct(q.shape, BF16), kvout, kvout],
        compiler_params=_params(dimension_semantics=("parallel", "arbitrary")),
    )(q, k, v, do)


SB_TILE = 256
SB_STRIP = 32
SB_HEAD_GROUP = 4
SB_BWD_HEAD_GROUP = 4


def _sb_scale(d):
    scale = 1.0 / math.sqrt(d)
    assert math.frexp(scale)[0] == 0.5, "the scale is folded into bf16 q, exact only for a power of two"
    return scale


def _sb_strip_mask(r, t):
    rr = r + lax.broadcasted_iota(jnp.int32, (SB_STRIP, t), 0)
    return lax.broadcasted_iota(jnp.int32, (SB_STRIP, t), 1) < rr


def _neg_abs(z):
    bits = lax.bitcast_convert_type(z, jnp.uint32) | jnp.uint32(0x80000000)
    return lax.bitcast_convert_type(bits, F32)


def _store_split(split_scr, rows, val, t):
    hi = val.astype(BF16)
    split_scr[rows, 0:t] = hi
    split_scr[rows, t:2 * t] = (val - hi.astype(F32)).astype(BF16)


def _sb_logs_phase(z_scr, nsplit_scr, beta_scr, t, diag):
    for r in range(0, t, SB_STRIP):
        rows = pl.ds(r, SB_STRIP)
        z = z_scr[rows, :]
        e = jnp.exp(_neg_abs(z))
        nlog = jnp.maximum(z, 0.0) + jnp.log(1.0 + e)
        if beta_scr is not None:
            inv = pl.reciprocal(1.0 + e, approx=True)
            beta_scr[rows, :] = jnp.where(z >= 0.0, inv, e * inv)
        if diag:
            nlog = jnp.where(_sb_strip_mask(r, t), nlog, 0.0)
        _store_split(nsplit_scr, rows, nlog, t)


def _sb_probs(z_scr, tin_scr, rsum_scr, rows, r, t, diag):
    rs = rsum_scr[rows, :]
    a = jnp.exp(z_scr[rows, :] - tin_scr[rows, :] - rs)
    if diag:
        a = jnp.where(_sb_strip_mask(r, t), a, 0.0)
    rsum_scr[rows, :] = rs + tin_scr[rows, 0:1]
    return a


def _sb_triangle(tri_scr, t):
    row = lax.broadcasted_iota(jnp.int32, (t, t), 0)
    col = lax.broadcasted_iota(jnp.int32, (t, t), 1)
    tri = (row >= col).astype(BF16)
    tri_scr[0:t, :] = tri
    tri_scr[t:2 * t, :] = tri


def _sb_fwd(q, k, v, *, name):
    h, s, d = q.shape
    t = _blk(s, SB_TILE)
    scale = _sb_scale(d)

    def body(q_ref, k_ref, v_ref, o_ref, o32_ref, qs_scr, tri_scr, z_scr, nsplit_scr, tin_scr, a_scr, rsum_scr):
        qi = pl.program_id(1)
        qs_scr[...] = q_ref[...] * scale
        _sb_triangle(tri_scr, t)
        rsum_scr[...] = jnp.zeros_like(rsum_scr)
        o32_ref[...] = jnp.zeros_like(o32_ref)

        def tile(kb, diag):
            keys = pl.ds(pl.multiple_of(kb * t, t), t)
            for g in range(grp):
                z_scr[g] = lax.dot_general(qs_scr[g], k_ref[g, keys, :], _NT, preferred_element_type=F32)
            for g in range(grp):
                _sb_logs_phase(z_scr.at[g], nsplit_scr.at[g], None, t, diag)
                tin_scr[g] = jnp.dot(nsplit_scr[g], tri_scr[...], preferred_element_type=F32)
            for g in range(grp):
                for r in range(0, t, SB_STRIP):
                    rows = pl.ds(r, SB_STRIP)
                    a = _sb_probs(z_scr.at[g], tin_scr.at[g], rsum_scr.at[g], rows, r, t, diag)
                    a_scr[g, rows, :] = a.astype(BF16)
                o32_ref[g] += jnp.dot(a_scr[g], v_ref[g, keys, :], preferred_element_type=F32)

        tile(qi, True)

        def walk(n, carry):
            tile(qi - 1 - n, False)
            return carry

        lax.fori_loop(0, qi, walk, 0)
        o_ref[...] = o32_ref[...].astype(o_ref.dtype)

    grp = SB_HEAD_GROUP
    assert h % grp == 0
    qs = pl.BlockSpec((grp, t, d), lambda hh, i: (hh, i, 0))
    ks = pl.BlockSpec((grp, s, d), lambda hh, i: (hh, 0, 0))
    tile_f32 = pltpu.VMEM((grp, t, t), F32)
    col_f32 = pltpu.VMEM((grp, t, 1), F32)
    return pl.pallas_call(
        body, name=name, grid=(h // grp, s // t), in_specs=[qs, ks, ks], out_specs=[qs, qs],
        out_shape=[jax.ShapeDtypeStruct(q.shape, BF16), jax.ShapeDtypeStruct(q.shape, F32)],
        scratch_shapes=[pltpu.VMEM((grp, t, d), BF16), pltpu.VMEM((2 * t, t), BF16), tile_f32,
                        pltpu.VMEM((grp, t, 2 * t), BF16), tile_f32, pltpu.VMEM((grp, t, t), BF16), col_f32],
        compiler_params=_params(dimension_semantics=("parallel", "parallel")),
    )(q, k, v)


def _sb_bwd(q, k, v, o32, do, dk0, dv0, *, name):
    h, s, d = q.shape
    t = _blk(s, SB_TILE)
    scale = _sb_scale(d)
    has_init = dk0 is not None
    n_in = 7 if has_init else 5

    def body(*refs):
        if has_init:
            q_ref, k_ref, v_ref, o_ref, do_ref, dk0_ref, dv0_ref, dq_ref, dk_ref, dv_ref = refs[:n_in + 3]
        else:
            q_ref, k_ref, v_ref, o_ref, do_ref, dq_ref, dk_ref, dv_ref = refs[:n_in + 3]
        (qs_scr, tri_scr, z_scr, beta_scr, nsplit_scr, tin_scr, da_scr, a_scr, dz_scr,
         dq_scr, rsum_scr, gsum_scr, dsum_scr) = refs[n_in + 3:]
        g_scr, gsplit_scr, gin_scr = da_scr, nsplit_scr, tin_scr
        qi = pl.program_id(1)

        @pl.when(qi == 0)
        def _():
            if has_init:
                dk_ref[...] = dk0_ref[...]
                dv_ref[...] = dv0_ref[...]
            else:
                dk_ref[...] = jnp.zeros_like(dk_ref)
                dv_ref[...] = jnp.zeros_like(dv_ref)

        qs_scr[...] = q_ref[...] * scale
        _sb_triangle(tri_scr, t)
        dsum_scr[...] = jnp.sum(o_ref[...] * do_ref[...].astype(F32), axis=2, keepdims=True)
        rsum_scr[...] = jnp.zeros_like(rsum_scr)
        gsum_scr[...] = jnp.zeros_like(gsum_scr)
        dq_scr[...] = jnp.zeros_like(dq_scr)

        def tile(kb, diag):
            keys = pl.ds(pl.multiple_of(kb * t, t), t)
            for g in range(grp):
                z_scr[g] = lax.dot_general(qs_scr[g], k_ref[g, keys, :], _NT, preferred_element_type=F32)
                da_scr[g] = lax.dot_general(do_ref[g], v_ref[g, keys, :], _NT, preferred_element_type=F32)
            for g in range(grp):
                _sb_logs_phase(z_scr.at[g], nsplit_scr.at[g], beta_scr.at[g], t, diag)
                tin_scr[g] = jnp.dot(nsplit_scr[g], tri_scr[...], preferred_element_type=F32)
            for g in range(grp):
                for r in range(0, t, SB_STRIP):
                    rows = pl.ds(r, SB_STRIP)
                    ab = _sb_probs(z_scr.at[g], tin_scr.at[g], rsum_scr.at[g], rows, r, t, diag).astype(BF16)
                    a_scr[g, rows, :] = ab
                    gv = ab.astype(F32) * da_scr[g, rows, :]
                    g_scr[g, rows, :] = gv
                    _store_split(gsplit_scr.at[g], rows, gv, t)
                gin_scr[g] = jnp.dot(gsplit_scr[g], tri_scr[...], preferred_element_type=F32)
                dv_ref[g, keys, :] += lax.dot_general(a_scr[g], do_ref[g], _TN, preferred_element_type=F32)
            for g in range(grp):
                for r in range(0, t, SB_STRIP):
                    rows = pl.ds(r, SB_STRIP)
                    gs = gsum_scr[g, rows, :]
                    gv = g_scr[g, rows, :]
                    dz = gv - beta_scr[g, rows, :] * ((gv - gin_scr[g, rows, :]) + (dsum_scr[g, rows, :] - gs))
                    if diag:
                        dz = jnp.where(_sb_strip_mask(r, t), dz, 0.0)
                    dz_scr[g, rows, :] = dz.astype(BF16)
                    gsum_scr[g, rows, :] = gs + gin_scr[g, rows, 0:1]
                dq_scr[g] += jnp.dot(dz_scr[g], k_ref[g, keys, :], preferred_element_type=F32)
                dk_ref[g, keys, :] += lax.dot_general(dz_scr[g], qs_scr[g], _TN, preferred_element_type=F32)

        tile(qi, True)

        def walk(n, carry):
            tile(qi - 1 - n, False)
            return carry

        lax.fori_loop(0, qi, walk, 0)
        dq_ref[...] = (dq_scr[...] * scale).astype(dq_ref.dtype)

    grp = SB_BWD_HEAD_GROUP
    assert h % grp == 0
    qs = pl.BlockSpec((grp, t, d), lambda hh, i: (hh, i, 0))
    ks = pl.BlockSpec((grp, s, d), lambda hh, i: (hh, 0, 0), pipeline_mode=pl.Buffered(1))
    in_specs = [qs, ks, ks, qs, qs] + ([ks, ks] if has_init else [])
    operands = [q, k, v, o32, do] + ([dk0, dv0] if has_init else [])
    acc = jax.ShapeDtypeStruct(q.shape, F32)
    tile_f32 = pltpu.VMEM((grp, t, t), F32)
    tile_bf16 = pltpu.VMEM((grp, t, t), BF16)
    split = pltpu.VMEM((grp, t, 2 * t), BF16)
    col_f32 = pltpu.VMEM((grp, t, 1), F32)
    return pl.pallas_call(
        body, name=name, grid=(h // grp, s // t), in_specs=in_specs, out_specs=[qs, ks, ks],
        out_shape=[jax.ShapeDtypeStruct(q.shape, BF16), acc, acc],
        scratch_shapes=[pltpu.VMEM((grp, t, d), BF16), pltpu.VMEM((2 * t, t), BF16), tile_f32, tile_f32, split,
                        tile_f32, tile_f32, tile_bf16, tile_bf16,
                        pltpu.VMEM((grp, t, d), F32), col_f32, col_f32, col_f32],
        compiler_params=_params(dimension_semantics=("parallel", "arbitrary")),
    )(*operands)


def _position():
    x, y, c = lax.axis_index("x"), lax.axis_index("y"), lax.axis_index("c")
    return x, y, c, [(1 - x, y), (x, 1 - y), (1 - x, 1 - y)]


_ANY = pl.BlockSpec(memory_space=pl.ANY)
N_PEER_CHIPS = N_CHIPS - 1


def _all_gather_chips(shards, *, name):
    n = len(shards)

    def body(*refs):
        ins, outs = refs[:n], refs[n:2 * n]
        send_sems, recv_sems, local_sems = refs[2 * n:]
        x, y, c, peers = _position()
        me = 2 * x + y
        copies = []
        for a in range(n):
            copies.append(pltpu.make_async_copy(ins[a], outs[a].at[me], local_sems.at[a]))
            for j, (px, py) in enumerate(peers):
                copies.append(pltpu.make_async_remote_copy(
                    src_ref=ins[a], dst_ref=outs[a].at[me],
                    send_sem=send_sems.at[a * N_PEER_CHIPS + j], recv_sem=recv_sems.at[a * N_PEER_CHIPS + j],
                    device_id=(px, py, c), device_id_type=MESH))
        for cp in copies:
            cp.start()
        for cp in copies:
            cp.wait()

    return pl.pallas_call(
        body, name=name, in_specs=[_ANY] * n, out_specs=[_ANY] * n,
        out_shape=[jax.ShapeDtypeStruct((N_CHIPS, *s.shape), s.dtype) for s in shards],
        scratch_shapes=[pltpu.SemaphoreType.DMA((n * N_PEER_CHIPS,)), pltpu.SemaphoreType.DMA((n * N_PEER_CHIPS,)),
                        pltpu.SemaphoreType.DMA((n,))],
        compiler_params=pltpu.CompilerParams(has_side_effects=True),
    )(*shards)


_HBM = pl.BlockSpec(memory_space=pltpu.HBM)
_SEM = pl.BlockSpec(memory_space=pltpu.SEMAPHORE)
_DATAFLOW = pltpu.SideEffectType.DATAFLOW_SIDE_EFFECTING
_TOKEN = jax.ShapeDtypeStruct((8, LANES), F32)


def _in_hbm(arr):
    return pltpu.with_memory_space_constraint(arr, pltpu.HBM)


def _gather_start(packs, lands, after, *, name):
    n = len(packs)

    def body(*refs):
        src, land = refs[:n], refs[n:2 * n]
        send_sems, recv_sems = refs[2 * n + 1:2 * n + 3]
        token = refs[-1]
        x, y, c, peers = _position()
        for i in range(n):
            for j, (px, py) in enumerate(peers):
                pltpu.make_async_remote_copy(
                    src_ref=src[i], dst_ref=land[i].at[2 * x + y],
                    send_sem=send_sems.at[N_PEER_CHIPS * i + j], recv_sem=recv_sems.at[N_PEER_CHIPS * i + j],
                    device_id=(px, py, c), device_id_type=MESH).start()
        token[...] = jnp.zeros_like(token)

    thru = [pltpu.HBM(a.shape, a.dtype) for a in (*packs, *lands)]
    outs = pl.pallas_call(
        body, name=name,
        out_shape=(pltpu.SemaphoreType.DMA((N_PEER_CHIPS * n,)), pltpu.SemaphoreType.DMA((N_PEER_CHIPS * n,)), *thru,
                   _TOKEN),
        in_specs=[_HBM] * (2 * n) + [_ANY],
        out_specs=(_SEM, _SEM, *[_HBM] * (2 * n), pl.BlockSpec(memory_space=pltpu.VMEM)),
        input_output_aliases={k: 2 + k for k in range(2 * n)},
        compiler_params=pltpu.CompilerParams(has_side_effects=_DATAFLOW),
    )(*[_in_hbm(a) for a in (*packs, *lands)], after)
    return outs[0], outs[1], list(outs[2:2 + n]), list(outs[2 + n:2 + 2 * n]), outs[-1]


def _gather_wait(first, packs, lands, send_sems, recv_sems, after, *, name):
    n = len(packs)

    def body(*refs):
        src, land = refs[:n], refs[n:2 * n]
        send_sems, recv_sems = refs[2 * n:2 * n + 2]
        _, _, c, peers = _position()
        for a in range(n):
            for j, (px, py) in enumerate(peers):
                sem = N_PEER_CHIPS * (first + a) + j
                cp = pltpu.make_async_remote_copy(
                    src_ref=src[a], dst_ref=land[a].at[2 * px + py], send_sem=send_sems.at[sem],
                    recv_sem=recv_sems.at[sem], device_id=(px, py, c), device_id_type=MESH)
                cp.wait_send()
                cp.wait_recv()

    outs = pl.pallas_call(
        body, name=name, out_shape=[pltpu.HBM(a.shape, a.dtype) for a in (*packs, *lands)],
        in_specs=[_HBM] * (2 * n) + [_SEM, _SEM, _ANY], out_specs=[_HBM] * (2 * n),
        input_output_aliases={k: k for k in range(2 * n)},
        compiler_params=pltpu.CompilerParams(has_side_effects=_DATAFLOW),
    )(*packs, *lands, send_sems, recv_sems, after)
    return list(outs[:n]), list(outs[n:])


def _scatter_start(gpacks, lands, *, name):
    n = len(gpacks)

    def body(*refs):
        src, land = refs[:n], refs[n:2 * n]
        send_sems, recv_sems = refs[2 * n:2 * n + 2]
        token = refs[-1]
        _, _, c, peers = _position()
        for a in range(n):
            for j, (px, py) in enumerate(peers):
                pltpu.make_async_remote_copy(
                    src_ref=src[a].at[2 * px + py], dst_ref=land[a].at[j], send_sem=send_sems.at[N_PEER_CHIPS * a + j],
                    recv_sem=recv_sems.at[N_PEER_CHIPS * a + j], device_id=(px, py, c), device_id_type=MESH).start()
        token[...] = jnp.zeros_like(token)

    thru = [pltpu.HBM(a.shape, a.dtype) for a in (*gpacks, *lands)]
    outs = pl.pallas_call(
        body, name=name,
        out_shape=(pltpu.SemaphoreType.DMA((N_PEER_CHIPS * n,)), pltpu.SemaphoreType.DMA((N_PEER_CHIPS * n,)), *thru,
                   _TOKEN),
        in_specs=[_HBM] * (2 * n), out_specs=(_SEM, _SEM, *[_HBM] * (2 * n), pl.BlockSpec(memory_space=pltpu.VMEM)),
        input_output_aliases={k: 2 + k for k in range(2 * n)},
        compiler_params=pltpu.CompilerParams(has_side_effects=_DATAFLOW),
    )(*[_in_hbm(a) for a in (*gpacks, *lands)])
    return outs[0], outs[1], list(outs[2:2 + n]), list(outs[2 + n:2 + 2 * n]), outs[-1]


def _scatter_wait(gpacks, lands, send_sems, recv_sems, after, *, name):
    n = len(gpacks)

    def body(*refs):
        src, land = refs[:n], refs[n:2 * n]
        send_sems, recv_sems = refs[2 * n:2 * n + 2]
        _, _, c, peers = _position()
        for a in range(n):
            for j, (px, py) in enumerate(peers):
                cp = pltpu.make_async_remote_copy(
                    src_ref=src[a].at[2 * px + py], dst_ref=land[a].at[j], send_sem=send_sems.at[N_PEER_CHIPS * a + j],
                    recv_sem=recv_sems.at[N_PEER_CHIPS * a + j], device_id=(px, py, c),
                    device_id_type=MESH)
                cp.wait_send()
                cp.wait_recv()

    outs = pl.pallas_call(
        body, name=name, out_shape=[pltpu.HBM(a.shape, a.dtype) for a in (*gpacks, *lands)],
        in_specs=[_HBM] * (2 * n) + [_SEM, _SEM, _ANY], out_specs=[_HBM] * (2 * n),
        input_output_aliases={k: k for k in range(2 * n)},
        compiler_params=pltpu.CompilerParams(has_side_effects=_DATAFLOW),
    )(*gpacks, *lands, send_sems, recv_sems, after)
    return list(outs[:n]), list(outs[n:])


def _sibling_exchange(arrs, *, name):
    n = len(arrs)

    def body(*refs):
        ins, outs = refs[:n], refs[n:2 * n]
        send_sems, recv_sems = refs[2 * n:]
        x, y, c, _ = _position()
        copies = [pltpu.make_async_remote_copy(
            src_ref=ins[a], dst_ref=outs[a], send_sem=send_sems.at[a], recv_sem=recv_sems.at[a],
            device_id=(x, y, 1 - c), device_id_type=MESH) for a in range(n)]
        for cp in copies:
            cp.start()
        for cp in copies:
            cp.wait()

    return pl.pallas_call(
        body, name=name, in_specs=[_ANY] * n, out_specs=[_ANY] * n,
        out_shape=[jax.ShapeDtypeStruct(a.shape, a.dtype) for a in arrs],
        scratch_shapes=[pltpu.SemaphoreType.DMA((n,)), pltpu.SemaphoreType.DMA((n,))],
        compiler_params=pltpu.CompilerParams(has_side_effects=True),
    )(*arrs)


def _small_all_reduce(v, *, name):
    r, cdim = v.shape

    def body(v_ref, o_ref, slots, send_sems, recv_sems):
        x, y, c, _ = _position()
        me = 4 * x + 2 * y + c
        slots[me] = v_ref[...]
        copies = []
        for j in range(1, N_DEVICES):
            peer = (x ^ ((j >> 2) & 1), y ^ ((j >> 1) & 1), c ^ (j & 1))
            copies.append(pltpu.make_async_remote_copy(
                src_ref=v_ref, dst_ref=slots.at[me], send_sem=send_sems.at[j - 1], recv_sem=recv_sems.at[j - 1],
                device_id=peer, device_id_type=MESH))
        for cp in copies:
            cp.start()
        for cp in copies:
            cp.wait()
        acc = slots[0]
        for dev in range(1, N_DEVICES):
            acc = acc + slots[dev]
        o_ref[...] = acc

    vm = pl.BlockSpec(memory_space=pltpu.VMEM)
    return pl.pallas_call(
        body, name=name, in_specs=[vm], out_specs=vm,
        out_shape=jax.ShapeDtypeStruct(v.shape, F32),
        scratch_shapes=[pltpu.VMEM((N_DEVICES, r, cdim), F32),
                        pltpu.SemaphoreType.DMA((N_DEVICES - 1,)), pltpu.SemaphoreType.DMA((N_DEVICES - 1,))],
        compiler_params=pltpu.CompilerParams(has_side_effects=True),
    )(v)


def _row_block(r, cap, mult):
    best = max(b for b in range(mult, cap + 1, mult) if r % b == 0)
    return best


def _sum_partials(chip_idx, own4, landed, *, name):
    _, r, c = landed.shape
    bm = _row_block(r, 512, 16)

    def body(chip_ref, own_ref, land_ref, o_ref):
        acc = own_ref[...].astype(F32)
        for j in range(N_PEER_CHIPS):
            acc = acc + land_ref[j].astype(F32)
        o_ref[...] = acc

    return pl.pallas_call(
        body, name=name,
        grid_spec=pltpu.PrefetchScalarGridSpec(
            num_scalar_prefetch=1, grid=(r // bm,),
            in_specs=[pl.BlockSpec((None, bm, c), lambda i, chip: (chip[0], i, 0)),
                      pl.BlockSpec((N_PEER_CHIPS, bm, c), lambda i, chip: (0, i, 0))],
            out_specs=pl.BlockSpec((bm, c), lambda i, chip: (i, 0))),
        out_shape=jax.ShapeDtypeStruct((r, c), F32),
        compiler_params=_params(dimension_semantics=("parallel",)),
    )(chip_idx, own4, landed)


def _adamw(w, m, v, g_a, g_b, *, name):
    r, c = w.shape
    bm = _blk(r, 256)
    two = g_b is not None
    bc1 = 1.0 - ADAM_B1 ** ADAM_STEP
    bc2 = 1.0 - ADAM_B2 ** ADAM_STEP

    def body(*refs):
        if two:
            w_ref, m_ref, v_ref, ga_ref, gb_ref, g_ref, d_ref, nm_ref, nv_ref = refs
            g = ga_ref[...] + gb_ref[...]
        else:
            w_ref, m_ref, v_ref, ga_ref, g_ref, d_ref, nm_ref, nv_ref = refs
            g = ga_ref[...]
        nm = ADAM_B1 * m_ref[...] + (1.0 - ADAM_B1) * g
        nv = ADAM_B2 * v_ref[...] + (1.0 - ADAM_B2) * (g * g)
        g_ref[...] = g
        nm_ref[...] = nm
        nv_ref[...] = nv
        d_ref[...] = -ADAM_LR * ((nm / bc1) / (jnp.sqrt(nv / bc2) + ADAM_EPS) + ADAM_WD * w_ref[...])

    spec = pl.BlockSpec((bm, c), lambda i: (i, 0))
    out = jax.ShapeDtypeStruct((r, c), F32)
    operands = [w, m, v, g_a] + ([g_b] if two else [])
    return pl.pallas_call(
        body, name=name, grid=(r // bm,),
        in_specs=[spec] * len(operands), out_specs=[spec] * 4, out_shape=[out] * 4,
        compiler_params=_params(dimension_semantics=("parallel",)),
    )(*operands)


def _heads(a):
    s, w = a.shape
    return a.reshape(s, w // HEAD_DIM, HEAD_DIM).transpose(1, 0, 2)


def _unheads(a):
    h, s, d = a.shape
    return a.transpose(1, 0, 2).reshape(s, h * d)


def _width_groups(shapes):
    groups = {}
    for idx, (_, c) in enumerate(shapes):
        groups.setdefault(c, []).append(idx)
    return list(groups.values())


def _pack(arrs, lead):
    groups = _width_groups([a.shape[-2:] for a in arrs])
    return [jnp.concatenate([arrs[k] for k in grp], axis=lead) for grp in groups]


def _unpack(bufs, shapes, lead):
    outs = [None] * len(shapes)
    for buf, grp in zip(bufs, _width_groups(shapes)):
        off = 0
        for k in grp:
            outs[k] = lax.slice_in_dim(buf, off, off + shapes[k][0], axis=lead)
            off += shapes[k][0]
    return outs


class LayerWeights:
    FIELDS = ("w_in", "wm", "wo", "wg", "wu", "wd", "wkv")
    STAGE = (0, 0, 1, 2, 2, 2, 2)

    def __init__(self):
        for f in self.FIELDS:
            setattr(self, f, None)


def _pack_small(mix, ffn, kvn, memn, fin, conv):
    d = mix.shape[-1]
    flat = conv.reshape(-1)
    rows_conv = SMALL_ROWS - 11
    flat = jnp.pad(flat, (0, rows_conv * d - flat.shape[0]))
    return jnp.concatenate([mix, ffn, kvn.reshape(1, d), memn.reshape(1, d), fin.reshape(1, d),
                            flat.reshape(rows_conv, d)], axis=0)


def _unpack_small(buf, conv_shape):
    n = math.prod(conv_shape)
    return (buf[0:4], buf[4:8], buf[8], buf[9], buf[10], buf[11:].reshape(-1)[:n].reshape(conv_shape))


def kernel(x, mem, mix_norm, a_in, conv_w, b_in, kv_norm, w_kv_shared, w_mem_kv, w_o, ffn_norm, w_gate, w_up, w_down, mem_norm, final_norm, loss_target, m_mix_norm, m_a_in, m_conv_w, m_b_in, m_kv_norm, m_w_kv_shared, m_w_mem_kv, m_w_o, m_ffn_norm, m_w_gate, m_w_up, m_w_down, m_mem_norm, m_final_norm, v_mix_norm, v_a_in, v_conv_w, v_b_in, v_kv_norm, v_w_kv_shared, v_w_mem_kv, v_w_o, v_ffn_norm, v_w_gate, v_w_up, v_w_down, v_mem_norm, v_final_norm):
    s, d = x.shape[1], x.shape[2]
    n_mem = mem.shape[1]
    depth = mix_norm.shape[0]
    n_a = a_in.shape[0]
    main_w = conv_w.shape[2] * N_CHIPS
    mem_w = w_mem_kv.shape[2] // 2
    ffn_c = w_gate.shape[2]
    kv_c = w_kv_shared.shape[1]
    a_c = a_in.shape[2]
    chip = 2 * lax.axis_index("x") + lax.axis_index("y")

    x0 = x[0]
    mem0 = mem[0]
    tgt = loss_target[0]
    bs = _blk(s, 1024)

    def layer_shards(i, a_or_b, others, kv):
        ws = [a_or_b[0][i] if i < n_a else a_or_b[1][i - n_a]] + [w[i] for w in others]
        return ws + ([kv] if i == n_a - 1 else [])

    shards = [layer_shards(i, (a_in, b_in), (w_mem_kv, w_o, w_gate, w_up, w_down), w_kv_shared) for i in range(depth)]
    shapes = [[w.shape for w in ws] for ws in shards]
    layer_packs = [_pack([w.astype(BF16) for w in ws], 0) for ws in shards]
    first = [sum(len(p) for p in layer_packs[:i]) for i in range(depth + 1)]
    packs = [p for ps in layer_packs for p in ps]
    conv_parts, = _all_gather_chips([conv_w], name="gather_conv_weights")
    conv_full = jnp.concatenate([conv_parts[kk] for kk in range(N_CHIPS)], axis=-1)
    lands = [lax.empty((N_CHIPS, *p.shape), BF16) for p in packs]
    send_sems, recv_sems, packs, lands, started = _gather_start(packs, lands, conv_parts,
                                                                name="gather_weights_start")

    def gain(vec):
        return vec.reshape(1, d)

    lw = [LayerWeights() for _ in range(depth)]

    def fetch(i, stage, after):
        groups = _width_groups(shapes[i])
        sel = [gi for gi, grp in enumerate(groups) if min(LayerWeights.STAGE[k] for k in grp) == stage]
        if not sel:
            return
        lo, hi = first[i] + sel[0], first[i] + sel[-1] + 1
        assert hi - lo == len(sel)
        own, landed = _gather_wait(lo, packs[lo:hi], lands[lo:hi], send_sems, recv_sems, after,
                                   name=f"gather_weights_wait_l{i}_s{stage}")
        for gi, mine, buf in zip(sel, own, landed):
            buf = lax.dynamic_update_slice_in_dim(buf, mine[None], chip, axis=0)
            off = 0
            for k in groups[gi]:
                rows = shapes[i][k][0]
                setattr(lw[i], LayerWeights.FIELDS[k], lax.slice_in_dim(buf, off, off + rows, axis=1))
                off += rows

    mem_n = _rms_fwd(mem0, gain(mem_norm), name="mem_norm_fwd", dep=started)
    saved = []
    k_sh = v_sh = hk = x_kv = None
    xc = x0
    for i in range(depth):
        st = {"x_in": xc}
        h = _rms_fwd(xc, gain(mix_norm[i]), name="mix_norm_fwd")
        fetch(i, 0, h)
        mkv = _mm(Op(mem_n), Op(lw[i].wm, 'r'), name="mem_kv_proj", bm=n_mem, bn=2 * mem_w, bk=d,
                  out_dtype=BF16)
        mem_k, mem_v = _heads(mkv[:, :mem_w]), _heads(mkv[:, mem_w:])
        if i < n_a:
            p = _mm(Op(h), Op(lw[i].w_in, 'c'), name="a_in_proj", bm=bs, bn=a_c, bk=d, out_dtype=BF16)
            y_main = _conv_fwd(p, conv_full[i], name="conv_fwd")
            q_mem = _heads(p[:, 3 * main_w:])
        else:
            p = _mm(Op(h), Op(lw[i].w_in, 'r'), name="b_in_proj", bm=bs, bn=d, bk=d, out_dtype=BF16)
            q_sb = _heads(p[:, :main_w])
            o_sb, o_sb32 = _sb_fwd(q_sb, k_sh, v_sh, name="sb_fwd")
            y_main = _unheads(o_sb)
            q_mem = _heads(p[:, main_w:])
            st.update(q_sb=q_sb, o_sb32=o_sb32)
        y_mem = _mem_fwd(q_mem, mem_k, mem_v, name="mem_attn_fwd")
        y = jnp.concatenate([y_main, _unheads(y_mem)], axis=-1)
        fetch(i, 1, y)
        x_mid = _mm(Op(y), Op(lw[i].wo, 'r'), name="w_o_proj", bm=bs, bn=d, bk=d, out_dtype=F32,
                    res=Op(xc))
        h2 = _rms_fwd(x_mid, gain(ffn_norm[i]), name="ffn_norm_fwd")
        fetch(i, 2, h2)
        gate, up, act = _ffn_in(h2, lw[i].wg, lw[i].wu, name="ffn_gate_up")
        xc = _mm(Op(act, 'c'), Op(lw[i].wd, 'r'), name="w_down_proj", bm=bs, bn=d, bk=ffn_c, out_dtype=F32,
                 res=Op(x_mid))
        st.update(h=h, p=p, mem_k=mem_k, mem_v=mem_v, q_mem=q_mem, y=y, x_mid=x_mid, h2=h2, gate=gate, up=up,
                  act=act)
        saved.append(st)
        if i == n_a - 1:
            x_kv = xc
            hk = _rms_fwd(xc, gain(kv_norm), name="kv_norm_fwd")
            kv = _mm(Op(hk), Op(lw[n_a - 1].wkv, 'c'), name="kv_proj", bm=bs, bn=kv_c, bk=d, out_dtype=BF16)
            k_sh, v_sh = _heads(kv[:, :main_w]), _heads(kv[:, main_w:])

    dx, dg_final, loss_part = _final_loss(xc, gain(final_norm), tgt, name="final_norm_loss")
    loss = lax.psum(loss_part[0, 0], ("x", "y", "c"))

    g_a, g_b, g_m, g_o, g_g, g_u, g_d = ([None] * n_a, [None] * (depth - n_a), [None] * depth, [None] * depth,
                                         [None] * depth, [None] * depth, [None] * depth)
    dg_mix, dg_ffn, dconv = [None] * depth, [None] * depth, [None] * n_a
    in_flight = [None] * depth
    dk_sh = dv_sh = None
    dmem_n = None
    g_kv = dg_kv = None
    for i in reversed(range(depth)):
        st = saved[i]
        dgate, dup = _ffn_dact(dx, lw[i].wd, st["gate"], st["up"], name="ffn_dact_gate")
        g_d[i] = _mm(Op(st["act"], 'c'), Op(dx), name="w_down_grad", ta=True, bm=ffn_c, bn=d, bk=bs,
                     out_dtype=BF16, out_chunk='r')
        g_g[i] = _mm(Op(st["h2"]), Op(dgate, 'c'), name="w_gate_grad", ta=True, bm=d, bn=ffn_c, bk=bs,
                     out_dtype=BF16, out_chunk='c')
        g_u[i] = _mm(Op(st["h2"]), Op(dup, 'c'), name="w_up_grad", ta=True, bm=d, bn=ffn_c, bk=bs,
                     out_dtype=BF16, out_chunk='c')
        dh2 = _mm(Op(dgate, 'c'), Op(lw[i].wg, 'c'), name="w_gate_dh", tb=True, bm=bs, bn=d, bk=ffn_c,
                  out_dtype=F32)
        dh2 = _mm(Op(dup, 'c'), Op(lw[i].wu, 'c'), name="w_up_dh", tb=True, bm=bs, bn=d, bk=ffn_c, out_dtype=F32,
                  res=Op(dh2))
        dx_mid, dg_ffn[i] = _rms_bwd(st["x_mid"], gain(ffn_norm[i]), dh2, dx, name="ffn_norm_bwd")
        dy = _mm(Op(dx_mid), Op(lw[i].wo, 'r'), name="w_o_dy", tb=True, bm=bs, bn=d, bk=d,
                 out_dtype=BF16)
        g_o[i] = _mm(Op(st["y"]), Op(dx_mid), name="w_o_grad", ta=True, bm=d // N_CHIPS, bn=d, bk=bs,
                     out_dtype=BF16, out_chunk='r')
        dq_mem, dmk, dmv = _mem_bwd(st["q_mem"], st["mem_k"], st["mem_v"], _heads(dy[:, main_w:]),
                                    name="mem_attn_bwd")
        dmkv = jnp.concatenate([_unheads(dmk), _unheads(dmv)], axis=-1)
        g_m[i] = _mm(Op(mem_n), Op(dmkv), name="mem_kv_grad", ta=True, bm=d // N_CHIPS, bn=2 * mem_w, bk=n_mem,
                     out_dtype=BF16, out_chunk='r')
        dmem_n = _mm(Op(dmkv), Op(lw[i].wm, 'r'), name="mem_kv_dmem", tb=True, bm=n_mem, bn=d,
                     bk=2 * mem_w, out_dtype=F32, res=None if dmem_n is None else Op(dmem_n))
        if i < n_a:
            db, dc, du, dconv[i] = _conv_bwd(st["p"], conv_full[i], dy[:, :main_w], name="conv_bwd")
            dp = jnp.concatenate([db, dc, du, _unheads(dq_mem)], axis=-1)
            g_a[i] = _mm(Op(st["h"]), Op(dp), name="a_in_grad", ta=True, bm=d, bn=a_c, bk=bs, out_dtype=BF16,
                         out_chunk='c')
            dh = _mm(Op(dp), Op(lw[i].w_in, 'c'), name="a_in_dh", tb=True, bm=bs, bn=d, bk=a_c, out_dtype=F32)
        else:
            dq_sb, dk_sh, dv_sh = _sb_bwd(st["q_sb"], k_sh, v_sh, st["o_sb32"], _heads(dy[:, :main_w]),
                                          dk_sh, dv_sh, name="sb_bwd")
            dp = jnp.concatenate([_unheads(dq_sb), _unheads(dq_mem)], axis=-1)
            g_b[i - n_a] = _mm(Op(st["h"]), Op(dp), name="b_in_grad", ta=True, bm=d // N_CHIPS, bn=d, bk=bs,
                               out_dtype=BF16, out_chunk='r')
            dh = _mm(Op(dp), Op(lw[i].w_in, 'r'), name="b_in_dh", tb=True, bm=bs, bn=d, bk=d,
                     out_dtype=F32)
        grads = [g_a[i] if i < n_a else g_b[i - n_a], g_m[i], g_o[i], g_g[i], g_u[i], g_d[i]]
        grads += [g_kv] if i == n_a - 1 else []
        gpacks = _pack(grads, 1)
        in_flight[i] = _scatter_start(gpacks, [lax.empty((N_PEER_CHIPS, *g.shape[1:]), BF16) for g in gpacks],
                                      name=f"scatter_grads_start_l{i}")
        dx, dg_mix[i] = _rms_bwd(st["x_in"], gain(mix_norm[i]), dh, dx_mid, name="mix_norm_bwd",
                                 dep=in_flight[i][4])
        if i == n_a:
            dkv = jnp.concatenate([_unheads(dk_sh), _unheads(dv_sh)], axis=-1)
            g_kv = _mm(Op(hk), Op(dkv), name="kv_grad", ta=True, bm=d, bn=kv_c, bk=bs, out_dtype=BF16,
                       out_chunk='c')
            dhk = _mm(Op(dkv), Op(lw[n_a - 1].wkv, 'c'), name="kv_dh", tb=True, bm=bs, bn=d, bk=kv_c, out_dtype=F32)
            dx, dg_kv = _rms_bwd(x_kv, gain(kv_norm), dhk, dx, name="kv_norm_bwd")
    _, dg_mem = _rms_bwd(mem0, gain(mem_norm), dmem_n, None, name="mem_norm_bwd")

    chip_idx = chip.astype(jnp.int32).reshape(1)
    core_sums = []
    for i in range(depth):
        ssem, rsem, gthru, lthru, _ = in_flight[i]
        gthru, lthru = _scatter_wait(gthru, lthru, ssem, rsem, dx, name=f"scatter_grads_wait_l{i}")
        core_sums += [_sum_partials(chip_idx, g, l, name="sum_chip_partials") for g, l in zip(gthru, lthru)]
    sibling_sums = _sibling_exchange(core_sums, name="exchange_core_sums")
    own_parts = [_unpack(core_sums[first[i]:first[i + 1]], shapes[i], 0) for i in range(depth)]
    sib_parts = [_unpack(sibling_sums[first[i]:first[i + 1]], shapes[i], 0) for i in range(depth)]

    def stacked(parts, pos, layers):
        return jnp.concatenate([parts[i][pos] for i in layers], axis=0)

    a_layers, b_layers, all_layers = range(n_a), range(n_a, depth), range(depth)
    big = [("a_in", a_in, m_a_in, v_a_in, 0, a_layers), ("b_in", b_in, m_b_in, v_b_in, 0, b_layers),
           ("w_kv_shared", w_kv_shared, m_w_kv_shared, v_w_kv_shared, 6, [n_a - 1]),
           ("w_mem_kv", w_mem_kv, m_w_mem_kv, v_w_mem_kv, 1, all_layers), ("w_o", w_o, m_w_o, v_w_o, 2, all_layers),
           ("w_gate", w_gate, m_w_gate, v_w_gate, 3, all_layers), ("w_up", w_up, m_w_up, v_w_up, 4, all_layers),
           ("w_down", w_down, m_w_down, v_w_down, 5, all_layers)]
    results = {}
    for wname, w, mm_, vv_, pos, layers in big:
        flat = lambda t: t.reshape(-1, t.shape[-1])
        outs = _adamw(flat(w), flat(mm_), flat(vv_), stacked(own_parts, pos, layers), stacked(sib_parts, pos, layers),
                      name="adamw")
        results[wname] = [o.reshape(w.shape) for o in outs]

    small_g = _pack_small(jnp.concatenate(dg_mix, axis=0), jnp.concatenate(dg_ffn, axis=0), dg_kv, dg_mem,
                          dg_final, jnp.stack(dconv, axis=0))
    small_g = _small_all_reduce(small_g, name="all_reduce_small_grads")
    conv_shape_full = (n_a, CONV_TAPS, main_w)
    gs = list(_unpack_small(small_g, conv_shape_full))
    gs[5] = lax.dynamic_slice_in_dim(gs[5], chip * conv_w.shape[2], conv_w.shape[2], axis=2)
    small_outs = _adamw(_pack_small(mix_norm, ffn_norm, kv_norm, mem_norm, final_norm, conv_w),
                        _pack_small(m_mix_norm, m_ffn_norm, m_kv_norm, m_mem_norm, m_final_norm, m_conv_w),
                        _pack_small(v_mix_norm, v_ffn_norm, v_kv_norm, v_mem_norm, v_final_norm, v_conv_w),
                        _pack_small(*gs), None, name="adamw_small")
    small_names = ["mix_norm", "ffn_norm", "kv_norm", "mem_norm", "final_norm", "conv_w"]
    for kind, buf in enumerate(small_outs):
        for wname, val in zip(small_names, _unpack_small(buf, conv_w.shape)):
            results.setdefault(wname, [None] * 4)[kind] = val

    order = ["mix_norm", "a_in", "conv_w", "b_in", "kv_norm", "w_kv_shared", "w_mem_kv", "w_o", "ffn_norm",
             "w_gate", "w_up", "w_down", "mem_norm", "final_norm"]
    return (loss, dx[None], *[results[nm][0] for nm in order], *[results[nm][1] for nm in order],
            *[results[nm][2] for nm in order], *[results[nm][3] for nm in order])
```

```python
import math
from typing import NamedTuple, Optional

import jax
import jax.numpy as jnp
from jax import lax
from jax.experimental import pallas as pl
from jax.experimental.pallas import tpu as pltpu

F32 = jnp.float32
BF16 = jnp.bfloat16
MESH = pl.DeviceIdType.MESH

N_CHIPS = 4
N_DEVICES = 8
HEAD_DIM = 64
CONV_TAPS = 3
NORM_EPS = 1e-6
V7X_VMEM_BYTES = 64 * 1024 * 1024
VMEM_LIMIT = V7X_VMEM_BYTES - 8 * 1024 * 1024
LANES = 128
SMALL_ROWS = 16

ADAM_LR = 0.001
ADAM_B1 = 0.9
ADAM_B2 = 0.999
ADAM_EPS = 1e-08
ADAM_WD = 0.01
ADAM_STEP = 10


def _params(**kw):
    return pltpu.CompilerParams(vmem_limit_bytes=VMEM_LIMIT, **kw)


def _blk(n, pref):
    b = min(n, pref)
    assert n % b == 0, (n, pref)
    return b


class Op(NamedTuple):
    arr: jax.Array
    chunk: Optional[str] = None
    layer: Optional[int] = None


def _op_spec(op_chunk, op_layer, shape2, br, bc, pick):
    r, c = shape2
    lead = () if op_layer is None else (op_layer,)
    none = (None,) * len(lead)
    if op_chunk is None:
        def imap(i, j, k):
            rb, cb = pick(i, j, k)
            return (*lead, rb, cb)
        return pl.BlockSpec((*none, br, bc), imap)
    if op_chunk == 'r' and br == N_CHIPS * r:
        def imap(i, j, k):
            rb, cb = pick(i, j, k)
            return (0, *lead, 0, cb)
        return pl.BlockSpec((N_CHIPS, *none, r, bc), imap)
    if op_chunk == 'r':
        n = r // br
        assert r % br == 0

        def imap(i, j, k):
            rb, cb = pick(i, j, k)
            return (rb // n, *lead, rb % n, cb)
        return pl.BlockSpec((None, *none, br, bc), imap)
    n = c // bc
    assert c % bc == 0

    def imap(i, j, k):
        rb, cb = pick(i, j, k)
        return (cb // n, *lead, rb, cb % n)
    return pl.BlockSpec((None, *none, br, bc), imap)


def _mm(a, b, *, name, ta=False, tb=False, bm, bn, bk, out_dtype, out_chunk=None, res=None):
    def dims(op):
        r, c = op.arr.shape[-2:]
        return (r * N_CHIPS if op.chunk == 'r' else r, c * N_CHIPS if op.chunk == 'c' else c)

    ar, ac = dims(a)
    br_, bc_ = dims(b)
    m, ka = (ac, ar) if ta else (ar, ac)
    kb, n = (bc_, br_) if tb else (br_, bc_)
    assert ka == kb, (name, ka, kb)
    assert m % bm == 0 and n % bn == 0 and ka % bk == 0, (name, m, n, ka, bm, bn, bk)
    nk = ka // bk

    if ta:
        a_spec = _op_spec(a.chunk, a.layer, a.arr.shape[-2:], bk, bm, lambda i, j, k: (k, i))
    else:
        a_spec = _op_spec(a.chunk, a.layer, a.arr.shape[-2:], bm, bk, lambda i, j, k: (i, k))
    if tb:
        b_spec = _op_spec(b.chunk, b.layer, b.arr.shape[-2:], bn, bk, lambda i, j, k: (j, k))
    else:
        b_spec = _op_spec(b.chunk, b.layer, b.arr.shape[-2:], bk, bn, lambda i, j, k: (k, j))

    if out_chunk == 'r':
        out_shape2 = (m // N_CHIPS, n)
    elif out_chunk == 'c':
        out_shape2 = (m, n // N_CHIPS)
    else:
        out_shape2 = (m, n)
    o_spec = _op_spec(out_chunk, None, out_shape2, bm, bn, lambda i, j, k: (i, j))
    out_full = out_shape2 if out_chunk is None else (N_CHIPS, *out_shape2)

    contract = (((0 if ta else 1,), (1 if tb else 0,)), ((), ()))
    has_res = res is not None

    def block2(ref):
        v = ref[...]
        return v.reshape(-1, v.shape[-1]).astype(BF16)

    def body(*refs):
        r_ref = refs[2] if has_res else None
        a_ref, b_ref = refs[:2]
        o_ref = refs[3 if has_res else 2]
        prod = lax.dot_general(block2(a_ref), block2(b_ref), contract, preferred_element_type=F32)
        if nk == 1:
            if has_res:
                prod = prod + r_ref[...].astype(F32)
            o_ref[...] = prod.astype(o_ref.dtype)
            return
        acc_ref = refs[-1]
        k = pl.program_id(2)

        @pl.when(k == 0)
        def _():
            acc_ref[...] = prod

        @pl.when(k > 0)
        def _():
            acc_ref[...] += prod

        @pl.when(k == nk - 1)
        def _():
            acc = acc_ref[...]
            if has_res:
                acc = acc + r_ref[...].astype(F32)
            o_ref[...] = acc.astype(o_ref.dtype)

    in_specs = [a_spec, b_spec]
    operands = [a.arr, b.arr]
    if has_res:
        in_specs.append(_op_spec(res.chunk, res.layer, res.arr.shape[-2:], bm, bn, lambda i, j, k: (i, j)))
        operands.append(res.arr)
    return pl.pallas_call(
        body, name=name, grid=(m // bm, n // bn, nk),
        in_specs=in_specs, out_specs=o_spec,
        out_shape=jax.ShapeDtypeStruct(out_full, out_dtype),
        scratch_shapes=[pltpu.VMEM((bm, bn), F32)] if nk > 1 else [],
        compiler_params=_params(dimension_semantics=("parallel", "parallel", "arbitrary")),
    )(*operands)


def _rms_fwd(x, g, *, name, dep=None):
    r, d = x.shape
    bm = _blk(r, 512)

    def body(x_ref, g_ref, *rest):
        o_ref = rest[-1]
        xv = x_ref[...]
        rstd = lax.rsqrt(jnp.mean(xv * xv, axis=-1, keepdims=True) + NORM_EPS)
        o_ref[...] = ((xv * rstd) * g_ref[...]).astype(o_ref.dtype)

    deps = [] if dep is None else [dep]
    return pl.pallas_call(
        body, name=name, grid=(r // bm,),
        in_specs=[pl.BlockSpec((bm, d), lambda i: (i, 0)), pl.BlockSpec((1, d), lambda i: (0, 0))]
        + [pl.BlockSpec(memory_space=pl.ANY)] * len(deps),
        out_specs=pl.BlockSpec((bm, d), lambda i: (i, 0)),
        out_shape=jax.ShapeDtypeStruct((r, d), BF16),
        compiler_params=_params(dimension_semantics=("parallel",)),
    )(x, g, *deps)


def _rms_bwd(x, g, dh, dres, *, name, dep=None):
    r, d = x.shape
    bm = _blk(r, 512)
    has_res = dres is not None
    deps = [] if dep is None else [dep]

    def body(*refs):
        if has_res:
            x_ref, g_ref, dh_ref, dres_ref = refs[:4]
        else:
            x_ref, g_ref, dh_ref = refs[:3]
        dx_ref, dg_ref = refs[-2:]

        @pl.when(pl.program_id(0) == 0)
        def _():
            dg_ref[...] = jnp.zeros_like(dg_ref)

        xv = x_ref[...]
        rstd = lax.rsqrt(jnp.mean(xv * xv, axis=-1, keepdims=True) + NORM_EPS)
        xh = xv * rstd
        dhv = dh_ref[...].astype(F32)
        dg_ref[...] += jnp.sum(dhv * xh, axis=0, keepdims=True)
        dxh = dhv * g_ref[...]
        dx = rstd * (dxh - xh * jnp.mean(dxh * xh, axis=-1, keepdims=True))
        if has_res:
            dx = dres_ref[...] + dx
        dx_ref[...] = dx

    row = pl.BlockSpec((bm, d), lambda i: (i, 0))
    vec = pl.BlockSpec((1, d), lambda i: (0, 0))
    in_specs = [row, vec, row] + ([row] if has_res else []) + [pl.BlockSpec(memory_space=pl.ANY)] * len(deps)
    operands = [x, g, dh] + ([dres] if has_res else []) + deps
    return pl.pallas_call(
        body, name=name, grid=(r // bm,),
        in_specs=in_specs, out_specs=[row, vec],
        out_shape=[jax.ShapeDtypeStruct((r, d), F32), jax.ShapeDtypeStruct((1, d), F32)],
        compiler_params=_params(dimension_semantics=("arbitrary",)),
    )(*operands)


def _final_loss(x, g, tgt, *, name):
    r, d = x.shape
    bm = _blk(r, 512)

    def body(x_ref, g_ref, t_ref, dx_ref, dg_ref, loss_ref):
        @pl.when(pl.program_id(0) == 0)
        def _():
            dg_ref[...] = jnp.zeros_like(dg_ref)
            loss_ref[...] = jnp.zeros_like(loss_ref)

        xv = x_ref[...]
        gv = g_ref[...]
        rstd = lax.rsqrt(jnp.mean(xv * xv, axis=-1, keepdims=True) + NORM_EPS)
        xh = xv * rstd
        diff = xh * gv - t_ref[...]
        loss_ref[...] += jnp.sum(diff * diff) * (0.5 / d)
        dy = diff * (1.0 / d)
        dg_ref[...] += jnp.sum(dy * xh, axis=0, keepdims=True)
        dxh = dy * gv
        dx_ref[...] = rstd * (dxh - xh * jnp.mean(dxh * xh, axis=-1, keepdims=True))

    row = pl.BlockSpec((bm, d), lambda i: (i, 0))
    vec = pl.BlockSpec((1, d), lambda i: (0, 0))
    return pl.pallas_call(
        body, name=name, grid=(r // bm,),
        in_specs=[row, vec, row],
        out_specs=[row, vec, pl.BlockSpec((1, LANES), lambda i: (0, 0))],
        out_shape=[jax.ShapeDtypeStruct((r, d), F32), jax.ShapeDtypeStruct((1, d), F32),
                   jax.ShapeDtypeStruct((1, LANES), F32)],
        compiler_params=_params(dimension_semantics=("arbitrary",)),
    )(x, g, tgt)


def _shift_down(v, k, row):
    return jnp.where(row >= k, pltpu.roll(v, k, 0), 0.0)


def _shift_up(v, k, row, s):
    return jnp.where(row < s - k, pltpu.roll(v, s - k, 0), 0.0)


def _conv_fwd(p, w, *, name):
    s = p.shape[0]
    width = w.shape[1]
    nb = width // LANES

    def body(b_ref, c_ref, u_ref, w_ref, y_ref):
        cu = c_ref[...].astype(F32) * u_ref[...].astype(F32)
        row = lax.broadcasted_iota(jnp.int32, cu.shape, 0)
        wv = w_ref[...]
        conv = wv[2:3] * cu + wv[1:2] * _shift_down(cu, 1, row) + wv[0:1] * _shift_down(cu, 2, row)
        y_ref[...] = (b_ref[...].astype(F32) * conv).astype(y_ref.dtype)

    def col(o):
        return pl.BlockSpec((s, LANES), lambda j: (0, j + o * nb))

    return pl.pallas_call(
        body, name=name, grid=(nb,),
        in_specs=[col(0), col(1), col(2), pl.BlockSpec((CONV_TAPS, LANES), lambda j: (0, j))],
        out_specs=col(0),
        out_shape=jax.ShapeDtypeStruct((s, width), BF16),
        compiler_params=_params(dimension_semantics=("parallel",)),
    )(p, p, p, w)


def _conv_bwd(p, w, dy, *, name):
    s = p.shape[0]
    width = w.shape[1]
    nb = width // LANES

    def body(b_ref, c_ref, u_ref, w_ref, dy_ref, db_ref, dc_ref, du_ref, dw_ref):
        bv = b_ref[...].astype(F32)
        cv = c_ref[...].astype(F32)
        uv = u_ref[...].astype(F32)
        dyv = dy_ref[...].astype(F32)
        cu = cv * uv
        row = lax.broadcasted_iota(jnp.int32, cu.shape, 0)
        wv = w_ref[...]
        cu1 = _shift_down(cu, 1, row)
        cu2 = _shift_down(cu, 2, row)
        conv = wv[2:3] * cu + wv[1:2] * cu1 + wv[0:1] * cu2
        db_ref[...] = (dyv * conv).astype(db_ref.dtype)
        dconv = dyv * bv
        dcu = wv[2:3] * dconv + wv[1:2] * _shift_up(dconv, 1, row, s) + wv[0:1] * _shift_up(dconv, 2, row, s)
        dc_ref[...] = (dcu * uv).astype(dc_ref.dtype)
        du_ref[...] = (dcu * cv).astype(du_ref.dtype)
        dw_ref[0:1, :] = jnp.sum(dconv * cu2, axis=0, keepdims=True)
        dw_ref[1:2, :] = jnp.sum(dconv * cu1, axis=0, keepdims=True)
        dw_ref[2:3, :] = jnp.sum(dconv * cu, axis=0, keepdims=True)

    def col(o):
        return pl.BlockSpec((s, LANES), lambda j: (0, j + o * nb))

    wspec = pl.BlockSpec((CONV_TAPS, LANES), lambda j: (0, j))
    act = jax.ShapeDtypeStruct((s, width), BF16)
    return pl.pallas_call(
        body, name=name, grid=(nb,),
        in_specs=[col(0), col(1), col(2), wspec, col(0)],
        out_specs=[col(0), col(0), col(0), wspec],
        out_shape=[act, act, act, jax.ShapeDtypeStruct((CONV_TAPS, width), F32)],
        compiler_params=_params(dimension_semantics=("parallel",)),
    )(p, p, p, w, dy)


FFN_STRIP = 128


def _ffn_in(h2, wg, wu, *, name):
    s, d = h2.shape
    nc, _, f = wg.shape
    bm = _blk(s, 1024)

    def body(h_ref, wg_ref, wu_ref, g_ref, u_ref, a_ref, g_scr, u_scr):
        hv = h_ref[...]
        g_scr[...] = jnp.dot(hv, wg_ref[...], preferred_element_type=F32)
        u_scr[...] = jnp.dot(hv, wu_ref[...], preferred_element_type=F32)
        for r in range(0, bm, FFN_STRIP):
            rows = pl.ds(r, min(FFN_STRIP, bm))
            gv, uv = g_scr[rows, :], u_scr[rows, :]
            g_ref[rows, :] = gv.astype(BF16)
            u_ref[rows, :] = uv.astype(BF16)
            a_ref[rows, :] = (gv * jax.nn.sigmoid(gv) * uv).astype(BF16)

    wspec = pl.BlockSpec((None, d, f), lambda i, j: (j, 0, 0))
    ospec = pl.BlockSpec((None, bm, f), lambda i, j: (j, i, 0))
    out = jax.ShapeDtypeStruct((nc, s, f), BF16)
    return pl.pallas_call(
        body, name=name, grid=(s // bm, nc),
        in_specs=[pl.BlockSpec((bm, d), lambda i, j: (i, 0)), wspec, wspec], out_specs=[ospec, ospec, ospec],
        out_shape=[out, out, out], scratch_shapes=[pltpu.VMEM((bm, f), F32), pltpu.VMEM((bm, f), F32)],
        compiler_params=_params(dimension_semantics=("parallel", "parallel")),
    )(h2, wg, wu)


def _ffn_dact(dx, wd, gate, up, *, name):
    s, d = dx.shape
    nc, f, _ = wd.shape
    bm = _blk(s, 1024)

    def body(dx_ref, wd_ref, g_ref, u_ref, dg_ref, du_ref, d_scr):
        d_scr[...] = lax.dot_general(dx_ref[...].astype(BF16), wd_ref[...], _NT, preferred_element_type=F32)
        for r in range(0, bm, FFN_STRIP):
            rows = pl.ds(r, min(FFN_STRIP, bm))
            gv = g_ref[rows, :].astype(F32)
            dv = d_scr[rows, :]
            sg = jax.nn.sigmoid(gv)
            dg_ref[rows, :] = (dv * u_ref[rows, :].astype(F32) * (sg * (1.0 + gv * (1.0 - sg)))).astype(BF16)
            du_ref[rows, :] = (dv * (gv * sg)).astype(BF16)

    cspec = pl.BlockSpec((None, bm, f), lambda i, j: (j, i, 0))
    out = jax.ShapeDtypeStruct((nc, s, f), BF16)
    return pl.pallas_call(
        body, name=name, grid=(s // bm, nc),
        in_specs=[pl.BlockSpec((bm, d), lambda i, j: (i, 0)), pl.BlockSpec((None, f, d), lambda i, j: (j, 0, 0)),
                  cspec, cspec],
        out_specs=[cspec, cspec], out_shape=[out, out], scratch_shapes=[pltpu.VMEM((bm, f), F32)],
        compiler_params=_params(dimension_semantics=("parallel", "parallel")),
    )(dx, wd, gate, up)


def _ffn_dh(dgate, dup, wg, wu, *, name):
    nc, s, f = dgate.shape
    d = wg.shape[1]
    bm = _blk(s, 512)

    def body(dg_ref, du_ref, wg_ref, wu_ref, o_ref):
        acc = None
        for c in range(nc):
            for x_ref, w_ref in ((dg_ref, wg_ref), (du_ref, wu_ref)):
                term = lax.dot_general(x_ref[c], w_ref[c], _NT, preferred_element_type=F32)
                acc = term if acc is None else acc + term
        o_ref[...] = acc

    xspec = pl.BlockSpec((nc, bm, f), lambda i: (0, i, 0))
    wspec = pl.BlockSpec((nc, d, f), lambda i: (0, 0, 0), pipeline_mode=pl.Buffered(1))
    return pl.pallas_call(
        body, name=name, grid=(s // bm,), in_specs=[xspec, xspec, wspec, wspec],
        out_specs=pl.BlockSpec((bm, d), lambda i: (i, 0)), out_shape=jax.ShapeDtypeStruct((s, d), F32),
        compiler_params=_params(dimension_semantics=("parallel",)),
    )(dgate, dup, wg, wu)


_NT = (((1,), (1,)), ((), ()))
_TN = (((0,), (0,)), ((), ()))


def _mem_probs(q, k, scale):
    s = lax.dot_general(q, k, _NT, preferred_element_type=F32) * scale
    e = jnp.exp(s - jnp.max(s, axis=-1, keepdims=True))
    return e / jnp.sum(e, axis=-1, keepdims=True)


def _mem_fwd(q, k, v, *, name):
    h, s, d = q.shape
    m = k.shape[1]
    bq = _blk(s, 1024)
    scale = 1.0 / math.sqrt(d)

    def body(q_ref, k_ref, v_ref, o_ref):
        p = _mem_probs(q_ref[...], k_ref[...], scale)
        o_ref[...] = jnp.dot(p.astype(BF16), v_ref[...], preferred_element_type=F32).astype(o_ref.dtype)

    qs = pl.BlockSpec((None, bq, d), lambda hh, i: (hh, i, 0))
    ks = pl.BlockSpec((None, m, d), lambda hh, i: (hh, 0, 0))
    return pl.pallas_call(
        body, name=name, grid=(h, s // bq), in_specs=[qs, ks, ks], out_specs=qs,
        out_shape=jax.ShapeDtypeStruct(q.shape, BF16),
        compiler_params=_params(dimension_semantics=("parallel", "parallel")),
    )(q, k, v)


def _mem_bwd(q, k, v, do, *, name):
    h, s, d = q.shape
    m = k.shape[1]
    bq = _blk(s, 1024)
    scale = 1.0 / math.sqrt(d)

    def body(q_ref, k_ref, v_ref, do_ref, dq_ref, dk_ref, dv_ref):
        @pl.when(pl.program_id(1) == 0)
        def _():
            dk_ref[...] = jnp.zeros_like(dk_ref)
            dv_ref[...] = jnp.zeros_like(dv_ref)

        qv = q_ref[...]
        kv = k_ref[...]
        dov = do_ref[...]
        p = _mem_probs(qv, kv, scale)
        pb = p.astype(BF16)
        dp = lax.dot_general(dov, v_ref[...], _NT, preferred_element_type=F32)
        pf = pb.astype(F32)
        ds = (pf * (dp - jnp.sum(pf * dp, axis=-1, keepdims=True)) * scale).astype(BF16)
        dq_ref[...] = jnp.dot(ds, kv, preferred_element_type=F32).astype(dq_ref.dtype)
        dk_ref[...] += lax.dot_general(ds, qv, _TN, preferred_element_type=F32)
        dv_ref[...] += lax.dot_general(pb, dov, _TN, preferred_element_type=F32)

    qs = pl.BlockSpec((None, bq, d), lambda hh, i: (hh, i, 0))
    ks = pl.BlockSpec((None, m, d), lambda hh, i: (hh, 0, 0))
    kvout = jax.ShapeDtypeStruct(k.shape, F32)
    return pl.pallas_call(
        body, name=name, grid=(h, s // bq), in_specs=[qs, ks, ks, qs], out_specs=[qs, ks, ks],
        out_shape=[jax.ShapeDtypeStruct(q.shape, BF16), kvout, kvout],
        compiler_params=_params(dimension_semantics=("parallel", "arbitrary")),
    )(q, k, v, do)


SB_TILE = 256
SB_STRIP = 32
SB_HEAD_GROUP = 4
SB_BWD_HEAD_GROUP = 4


def _sb_scale(d):
    scale = 1.0 / math.sqrt(d)
    assert math.frexp(scale)[0] == 0.5, "the scale is folded into bf16 q, exact only for a power of two"
    return scale


def _sb_strip_mask(r, t):
    rr = r + lax.broadcasted_iota(jnp.int32, (SB_STRIP, t), 0)
    return lax.broadcasted_iota(jnp.int32, (SB_STRIP, t), 1) < rr


def _neg_abs(z):
    bits = lax.bitcast_convert_type(z, jnp.uint32) | jnp.uint32(0x80000000)
    return lax.bitcast_convert_type(bits, F32)


def _store_split(split_scr, rows, val, t):
    hi = val.astype(BF16)
    split_scr[rows, 0:t] = hi
    split_scr[rows, t:2 * t] = (val - hi.astype(F32)).astype(BF16)


def _sb_logs_phase(z_scr, nsplit_scr, beta_scr, t, diag):
    for r in range(0, t, SB_STRIP):
        rows = pl.ds(r, SB_STRIP)
        z = z_scr[rows, :]
        e = jnp.exp(_neg_abs(z))
        nlog = jnp.maximum(z, 0.0) + jnp.log(1.0 + e)
        if beta_scr is not None:
            inv = pl.reciprocal(1.0 + e, approx=True)
            beta_scr[rows, :] = jnp.where(z >= 0.0, inv, e * inv)
        if diag:
            nlog = jnp.where(_sb_strip_mask(r, t), nlog, 0.0)
        _store_split(nsplit_scr, rows, nlog, t)


def _sb_probs(z_scr, tin_scr, rsum_scr, rows, r, t, diag):
    rs = rsum_scr[rows, :]
    a = jnp.exp(z_scr[rows, :] - tin_scr[rows, :] - rs)
    if diag:
        a = jnp.where(_sb_strip_mask(r, t), a, 0.0)
    rsum_scr[rows, :] = rs + tin_scr[rows, 0:1]
    return a


def _sb_triangle(tri_scr, t):
    row = lax.broadcasted_iota(jnp.int32, (t, t), 0)
    col = lax.broadcasted_iota(jnp.int32, (t, t), 1)
    tri = (row >= col).astype(BF16)
    tri_scr[0:t, :] = tri
    tri_scr[t:2 * t, :] = tri


def _sb_fwd(q, k, v, *, name):
    h, s, d = q.shape
    t = _blk(s, SB_TILE)
    scale = _sb_scale(d)

    def body(q_ref, k_ref, v_ref, o_ref, o32_ref, qs_scr, tri_scr, z_scr, nsplit_scr, tin_scr, a_scr, rsum_scr):
        qi = pl.program_id(1)
        qs_scr[...] = q_ref[...] * scale
        _sb_triangle(tri_scr, t)
        rsum_scr[...] = jnp.zeros_like(rsum_scr)
        o32_ref[...] = jnp.zeros_like(o32_ref)

        def tile(kb, diag):
            keys = pl.ds(pl.multiple_of(kb * t, t), t)
            for g in range(grp):
                z_scr[g] = lax.dot_general(qs_scr[g], k_ref[g, keys, :], _NT, preferred_element_type=F32)
            for g in range(grp):
                _sb_logs_phase(z_scr.at[g], nsplit_scr.at[g], None, t, diag)
                tin_scr[g] = jnp.dot(nsplit_scr[g], tri_scr[...], preferred_element_type=F32)
            for g in range(grp):
                for r in range(0, t, SB_STRIP):
                    rows = pl.ds(r, SB_STRIP)
                    a = _sb_probs(z_scr.at[g], tin_scr.at[g], rsum_scr.at[g], rows, r, t, diag)
                    a_scr[g, rows, :] = a.astype(BF16)
                o32_ref[g] += jnp.dot(a_scr[g], v_ref[g, keys, :], preferred_element_type=F32)

        tile(qi, True)

        def walk(n, carry):
            tile(qi - 1 - n, False)
            return carry

        lax.fori_loop(0, qi, walk, 0)
        o_ref[...] = o32_ref[...].astype(o_ref.dtype)

    grp = SB_HEAD_GROUP
    assert h % grp == 0
    qs = pl.BlockSpec((grp, t, d), lambda hh, i: (hh, i, 0))
    ks = pl.BlockSpec((grp, s, d), lambda hh, i: (hh, 0, 0))
    tile_f32 = pltpu.VMEM((grp, t, t), F32)
    col_f32 = pltpu.VMEM((grp, t, 1), F32)
    return pl.pallas_call(
        body, name=name, grid=(h // grp, s // t), in_specs=[qs, ks, ks], out_specs=[qs, qs],
        out_shape=[jax.ShapeDtypeStruct(q.shape, BF16), jax.ShapeDtypeStruct(q.shape, F32)],
        scratch_shapes=[pltpu.VMEM((grp, t, d), BF16), pltpu.VMEM((2 * t, t), BF16), tile_f32,
                        pltpu.VMEM((grp, t, 2 * t), BF16), tile_f32, pltpu.VMEM((grp, t, t), BF16), col_f32],
        compiler_params=_params(dimension_semantics=("parallel", "parallel")),
    )(q, k, v)


def _sb_bwd(q, k, v, o32, do, dk0, dv0, *, name):
    h, s, d = q.shape
    t = _blk(s, SB_TILE)
    scale = _sb_scale(d)
    has_init = dk0 is not None
    n_in = 7 if has_init else 5

    def body(*refs):
        if has_init:
            q_ref, k_ref, v_ref, o_ref, do_ref, dk0_ref, dv0_ref, dq_ref, dk_ref, dv_ref = refs[:n_in + 3]
        else:
            q_ref, k_ref, v_ref, o_ref, do_ref, dq_ref, dk_ref, dv_ref = refs[:n_in + 3]
        (qs_scr, tri_scr, z_scr, beta_scr, nsplit_scr, tin_scr, da_scr, a_scr, dz_scr,
         dq_scr, rsum_scr, gsum_scr, dsum_scr) = refs[n_in + 3:]
        g_scr, gsplit_scr, gin_scr = da_scr, nsplit_scr, tin_scr
        qi = pl.program_id(1)

        @pl.when(qi == 0)
        def _():
            if has_init:
                dk_ref[...] = dk0_ref[...]
                dv_ref[...] = dv0_ref[...]
            else:
                dk_ref[...] = jnp.zeros_like(dk_ref)
                dv_ref[...] = jnp.zeros_like(dv_ref)

        qs_scr[...] = q_ref[...] * scale
        _sb_triangle(tri_scr, t)
        dsum_scr[...] = jnp.sum(o_ref[...] * do_ref[...].astype(F32), axis=2, keepdims=True)
        rsum_scr[...] = jnp.zeros_like(rsum_scr)
        gsum_scr[...] = jnp.zeros_like(gsum_scr)
        dq_scr[...] = jnp.zeros_like(dq_scr)

        def tile(kb, diag):
            keys = pl.ds(pl.multiple_of(kb * t, t), t)
            for g in range(grp):
                z_scr[g] = lax.dot_general(qs_scr[g], k_ref[g, keys, :], _NT, preferred_element_type=F32)
                da_scr[g] = lax.dot_general(do_ref[g], v_ref[g, keys, :], _NT, preferred_element_type=F32)
            for g in range(grp):
                _sb_logs_phase(z_scr.at[g], nsplit_scr.at[g], beta_scr.at[g], t, diag)
                tin_scr[g] = jnp.dot(nsplit_scr[g], tri_scr[...], preferred_element_type=F32)
            for g in range(grp):
                for r in range(0, t, SB_STRIP):
                    rows = pl.ds(r, SB_STRIP)
                    ab = _sb_probs(z_scr.at[g], tin_scr.at[g], rsum_scr.at[g], rows, r, t, diag).astype(BF16)
                    a_scr[g, rows, :] = ab
                    gv = ab.astype(F32) * da_scr[g, rows, :]
                    g_scr[g, rows, :] = gv
                    _store_split(gsplit_scr.at[g], rows, gv, t)
                gin_scr[g] = jnp.dot(gsplit_scr[g], tri_scr[...], preferred_element_type=F32)
                dv_ref[g, keys, :] += lax.dot_general(a_scr[g], do_ref[g], _TN, preferred_element_type=F32)
            for g in range(grp):
                for r in range(0, t, SB_STRIP):
                    rows = pl.ds(r, SB_STRIP)
                    gs = gsum_scr[g, rows, :]
                    gv = g_scr[g, rows, :]
                    dz = gv - beta_scr[g, rows, :] * ((gv - gin_scr[g, rows, :]) + (dsum_scr[g, rows, :] - gs))
                    if diag:
                        dz = jnp.where(_sb_strip_mask(r, t), dz, 0.0)
                    dz_scr[g, rows, :] = dz.astype(BF16)
                    gsum_scr[g, rows, :] = gs + gin_scr[g, rows, 0:1]
                dq_scr[g] += jnp.dot(dz_scr[g], k_ref[g, keys, :], preferred_element_type=F32)
                dk_ref[g, keys, :] += lax.dot_general(dz_scr[g], qs_scr[g], _TN, preferred_element_type=F32)

        tile(qi, True)

        def walk(n, carry):
            tile(qi - 1 - n, False)
            return carry

        lax.fori_loop(0, qi, walk, 0)
        dq_ref[...] = (dq_scr[...] * scale).astype(dq_ref.dtype)

    grp = SB_BWD_HEAD_GROUP
    assert h % grp == 0
    qs = pl.BlockSpec((grp, t, d), lambda hh, i: (hh, i, 0))
    ks = pl.BlockSpec((grp, s, d), lambda hh, i: (hh, 0, 0), pipeline_mode=pl.Buffered(1))
    in_specs = [qs, ks, ks, qs, qs] + ([ks, ks] if has_init else [])
    operands = [q, k, v, o32, do] + ([dk0, dv0] if has_init else [])
    acc = jax.ShapeDtypeStruct(q.shape, F32)
    tile_f32 = pltpu.VMEM((grp, t, t), F32)
    tile_bf16 = pltpu.VMEM((grp, t, t), BF16)
    split = pltpu.VMEM((grp, t, 2 * t), BF16)
    col_f32 = pltpu.VMEM((grp, t, 1), F32)
    return pl.pallas_call(
        body, name=name, grid=(h // grp, s // t), in_specs=in_specs, out_specs=[qs, ks, ks],
        out_shape=[jax.ShapeDtypeStruct(q.shape, BF16), acc, acc],
        scratch_shapes=[pltpu.VMEM((grp, t, d), BF16), pltpu.VMEM((2 * t, t), BF16), tile_f32, tile_f32, split,
                        tile_f32, tile_f32, tile_bf16, tile_bf16,
                        pltpu.VMEM((grp, t, d), F32), col_f32, col_f32, col_f32],
        compiler_params=_params(dimension_semantics=("parallel", "arbitrary")),
    )(*operands)


def _position():
    x, y, c = lax.axis_index("x"), lax.axis_index("y"), lax.axis_index("c")
    return x, y, c, [(1 - x, y), (x, 1 - y), (1 - x, 1 - y)]


_ANY = pl.BlockSpec(memory_space=pl.ANY)
N_PEER_CHIPS = N_CHIPS - 1


def _all_gather_chips(shards, *, name):
    n = len(shards)

    def body(*refs):
        ins, outs = refs[:n], refs[n:2 * n]
        send_sems, recv_sems, local_sems = refs[2 * n:]
        x, y, c, peers = _position()
        me = 2 * x + y
        copies = []
        for a in range(n):
            copies.append(pltpu.make_async_copy(ins[a], outs[a].at[me], local_sems.at[a]))
            for j, (px, py) in enumerate(peers):
                copies.append(pltpu.make_async_remote_copy(
                    src_ref=ins[a], dst_ref=outs[a].at[me],
                    send_sem=send_sems.at[a * N_PEER_CHIPS + j], recv_sem=recv_sems.at[a * N_PEER_CHIPS + j],
                    device_id=(px, py, c), device_id_type=MESH))
        for cp in copies:
            cp.start()
        for cp in copies:
            cp.wait()

    return pl.pallas_call(
        body, name=name, in_specs=[_ANY] * n, out_specs=[_ANY] * n,
        out_shape=[jax.ShapeDtypeStruct((N_CHIPS, *s.shape), s.dtype) for s in shards],
        scratch_shapes=[pltpu.SemaphoreType.DMA((n * N_PEER_CHIPS,)), pltpu.SemaphoreType.DMA((n * N_PEER_CHIPS,)),
                        pltpu.SemaphoreType.DMA((n,))],
        compiler_params=pltpu.CompilerParams(has_side_effects=True),
    )(*shards)


_HBM = pl.BlockSpec(memory_space=pltpu.HBM)
_SEM = pl.BlockSpec(memory_space=pltpu.SEMAPHORE)
_DATAFLOW = pltpu.SideEffectType.DATAFLOW_SIDE_EFFECTING
_TOKEN = jax.ShapeDtypeStruct((8, LANES), F32)


def _in_hbm(arr):
    return pltpu.with_memory_space_constraint(arr, pltpu.HBM)


def _gather_start(packs, lands, after, *, name):
    n = len(packs)

    def body(*refs):
        src, land = refs[:n], refs[n:2 * n]
        send_sems, recv_sems = refs[2 * n + 1:2 * n + 3]
        token = refs[-1]
        x, y, c, peers = _position()
        for i in range(n):
            for j, (px, py) in enumerate(peers):
                pltpu.make_async_remote_copy(
                    src_ref=src[i], dst_ref=land[i].at[2 * x + y],
                    send_sem=send_sems.at[N_PEER_CHIPS * i + j], recv_sem=recv_sems.at[N_PEER_CHIPS * i + j],
                    device_id=(px, py, c), device_id_type=MESH).start()
        token[...] = jnp.zeros_like(token)

    thru = [pltpu.HBM(a.shape, a.dtype) for a in (*packs, *lands)]
    outs = pl.pallas_call(
        body, name=name,
        out_shape=(pltpu.SemaphoreType.DMA((N_PEER_CHIPS * n,)), pltpu.SemaphoreType.DMA((N_PEER_CHIPS * n,)), *thru,
                   _TOKEN),
        in_specs=[_HBM] * (2 * n) + [_ANY],
        out_specs=(_SEM, _SEM, *[_HBM] * (2 * n), pl.BlockSpec(memory_space=pltpu.VMEM)),
        input_output_aliases={k: 2 + k for k in range(2 * n)},
        compiler_params=pltpu.CompilerParams(has_side_effects=_DATAFLOW),
    )(*[_in_hbm(a) for a in (*packs, *lands)], after)
    return outs[0], outs[1], list(outs[2:2 + n]), list(outs[2 + n:2 + 2 * n]), outs[-1]


def _gather_wait(first, packs, lands, send_sems, recv_sems, after, *, name):
    n = len(packs)

    def body(*refs):
        src, land = refs[:n], refs[n:2 * n]
        send_sems, recv_sems = refs[2 * n:2 * n + 2]
        _, _, c, peers = _position()
        for a in range(n):
            for j, (px, py) in enumerate(peers):
                sem = N_PEER_CHIPS * (first + a) + j
                cp = pltpu.make_async_remote_copy(
                    src_ref=src[a], dst_ref=land[a].at[2 * px + py], send_sem=send_sems.at[sem],
                    recv_sem=recv_sems.at[sem], device_id=(px, py, c), device_id_type=MESH)
                cp.wait_send()
                cp.wait_recv()

    outs = pl.pallas_call(
        body, name=name, out_shape=[pltpu.HBM(a.shape, a.dtype) for a in (*packs, *lands)],
        in_specs=[_HBM] * (2 * n) + [_SEM, _SEM, _ANY], out_specs=[_HBM] * (2 * n),
        input_output_aliases={k: k for k in range(2 * n)},
        compiler_params=pltpu.CompilerParams(has_side_effects=_DATAFLOW),
    )(*packs, *lands, send_sems, recv_sems, after)
    return list(outs[:n]), list(outs[n:])


def _scatter_start(gpacks, lands, *, name):
    n = len(gpacks)

    def body(*refs):
        src, land = refs[:n], refs[n:2 * n]
        send_sems, recv_sems = refs[2 * n:2 * n + 2]
        token = refs[-1]
        _, _, c, peers = _position()
        for a in range(n):
            for j, (px, py) in enumerate(peers):
                pltpu.make_async_remote_copy(
                    src_ref=src[a].at[2 * px + py], dst_ref=land[a].at[j], send_sem=send_sems.at[N_PEER_CHIPS * a + j],
                    recv_sem=recv_sems.at[N_PEER_CHIPS * a + j], device_id=(px, py, c), device_id_type=MESH).start()
        token[...] = jnp.zeros_like(token)

    thru = [pltpu.HBM(a.shape, a.dtype) for a in (*gpacks, *lands)]
    outs = pl.pallas_call(
        body, name=name,
        out_shape=(pltpu.SemaphoreType.DMA((N_PEER_CHIPS * n,)), pltpu.SemaphoreType.DMA((N_PEER_CHIPS * n,)), *thru,
                   _TOKEN),
        in_specs=[_HBM] * (2 * n), out_specs=(_SEM, _SEM, *[_HBM] * (2 * n), pl.BlockSpec(memory_space=pltpu.VMEM)),
        input_output_aliases={k: 2 + k for k in range(2 * n)},
        compiler_params=pltpu.CompilerParams(has_side_effects=_DATAFLOW),
    )(*[_in_hbm(a) for a in (*gpacks, *lands)])
    return outs[0], outs[1], list(outs[2:2 + n]), list(outs[2 + n:2 + 2 * n]), outs[-1]


def _scatter_wait(gpacks, lands, send_sems, recv_sems, after, *, name):
    n = len(gpacks)

    def body(*refs):
        src, land = refs[:n], refs[n:2 * n]
        send_sems, recv_sems = refs[2 * n:2 * n + 2]
        _, _, c, peers = _position()
        for a in range(n):
            for j, (px, py) in enumerate(peers):
                cp = pltpu.make_async_remote_copy(
                    src_ref=src[a].at[2 * px + py], dst_ref=land[a].at[j], send_sem=send_sems.at[N_PEER_CHIPS * a + j],
                    recv_sem=recv_sems.at[N_PEER_CHIPS * a + j], device_id=(px, py, c),
                    device_id_type=MESH)
                cp.wait_send()
                cp.wait_recv()

    outs = pl.pallas_call(
        body, name=name, out_shape=[pltpu.HBM(a.shape, a.dtype) for a in (*gpacks, *lands)],
        in_specs=[_HBM] * (2 * n) + [_SEM, _SEM, _ANY], out_specs=[_HBM] * (2 * n),
        input_output_aliases={k: k for k in range(2 * n)},
        compiler_params=pltpu.CompilerParams(has_side_effects=_DATAFLOW),
    )(*gpacks, *lands, send_sems, recv_sems, after)
    return list(outs[:n]), list(outs[n:])


def _sibling_exchange(arrs, *, name):
    n = len(arrs)

    def body(*refs):
        ins, outs = refs[:n], refs[n:2 * n]
        send_sems, recv_sems = refs[2 * n:]
        x, y, c, _ = _position()
        copies = [pltpu.make_async_remote_copy(
            src_ref=ins[a], dst_ref=outs[a], send_sem=send_sems.at[a], recv_sem=recv_sems.at[a],
            device_id=(x, y, 1 - c), device_id_type=MESH) for a in range(n)]
        for cp in copies:
            cp.start()
        for cp in copies:
            cp.wait()

    return pl.pallas_call(
        body, name=name, in_specs=[_ANY] * n, out_specs=[_ANY] * n,
        out_shape=[jax.ShapeDtypeStruct(a.shape, a.dtype) for a in arrs],
        scratch_shapes=[pltpu.SemaphoreType.DMA((n,)), pltpu.SemaphoreType.DMA((n,))],
        compiler_params=pltpu.CompilerParams(has_side_effects=True),
    )(*arrs)


def _small_all_reduce(v, *, name):
    r, cdim = v.shape

    def body(v_ref, o_ref, slots, send_sems, recv_sems):
        x, y, c, _ = _position()
        me = 4 * x + 2 * y + c
        slots[me] = v_ref[...]
        copies = []
        for j in range(1, N_DEVICES):
            peer = (x ^ ((j >> 2) & 1), y ^ ((j >> 1) & 1), c ^ (j & 1))
            copies.append(pltpu.make_async_remote_copy(
                src_ref=v_ref, dst_ref=slots.at[me], send_sem=send_sems.at[j - 1], recv_sem=recv_sems.at[j - 1],
                device_id=peer, device_id_type=MESH))
        for cp in copies:
            cp.start()
        for cp in copies:
            cp.wait()
        acc = slots[0]
        for dev in range(1, N_DEVICES):
            acc = acc + slots[dev]
        o_ref[...] = acc

    vm = pl.BlockSpec(memory_space=pltpu.VMEM)
    return pl.pallas_call(
        body, name=name, in_specs=[vm], out_specs=vm,
        out_shape=jax.ShapeDtypeStruct(v.shape, F32),
        scratch_shapes=[pltpu.VMEM((N_DEVICES, r, cdim), F32),
                        pltpu.SemaphoreType.DMA((N_DEVICES - 1,)), pltpu.SemaphoreType.DMA((N_DEVICES - 1,))],
        compiler_params=pltpu.CompilerParams(has_side_effects=True),
    )(v)


def _row_block(r, cap, mult):
    best = max(b for b in range(mult, cap + 1, mult) if r % b == 0)
    return best


def _sum_partials(chip_idx, own4, landed, *, name):
    _, r, c = landed.shape
    bm = _row_block(r, 512, 16)

    def body(chip_ref, own_ref, land_ref, o_ref):
        acc = own_ref[...].astype(F32)
        for j in range(N_PEER_CHIPS):
            acc = acc + land_ref[j].astype(F32)
        o_ref[...] = acc

    return pl.pallas_call(
        body, name=name,
        grid_spec=pltpu.PrefetchScalarGridSpec(
            num_scalar_prefetch=1, grid=(r // bm,),
            in_specs=[pl.BlockSpec((None, bm, c), lambda i, chip: (chip[0], i, 0)),
                      pl.BlockSpec((N_PEER_CHIPS, bm, c), lambda i, chip: (0, i, 0))],
            out_specs=pl.BlockSpec((bm, c), lambda i, chip: (i, 0))),
        out_shape=jax.ShapeDtypeStruct((r, c), F32),
        compiler_params=_params(dimension_semantics=("parallel",)),
    )(chip_idx, own4, landed)


def _adamw(w, m, v, g_a, g_b, *, name):
    r, c = w.shape
    bm = _blk(r, 256)
    two = g_b is not None
    bc1 = 1.0 - ADAM_B1 ** ADAM_STEP
    bc2 = 1.0 - ADAM_B2 ** ADAM_STEP

    def body(*refs):
        if two:
            w_ref, m_ref, v_ref, ga_ref, gb_ref, g_ref, d_ref, nm_ref, nv_ref = refs
            g = ga_ref[...] + gb_ref[...]
        else:
            w_ref, m_ref, v_ref, ga_ref, g_ref, d_ref, nm_ref, nv_ref = refs
            g = ga_ref[...]
        nm = ADAM_B1 * m_ref[...] + (1.0 - ADAM_B1) * g
        nv = ADAM_B2 * v_ref[...] + (1.0 - ADAM_B2) * (g * g)
        g_ref[...] = g
        nm_ref[...] = nm
        nv_ref[...] = nv
        d_ref[...] = -ADAM_LR * ((nm / bc1) / (jnp.sqrt(nv / bc2) + ADAM_EPS) + ADAM_WD * w_ref[...])

    spec = pl.BlockSpec((bm, c), lambda i: (i, 0))
    out = jax.ShapeDtypeStruct((r, c), F32)
    operands = [w, m, v, g_a] + ([g_b] if two else [])
    return pl.pallas_call(
        body, name=name, grid=(r // bm,),
        in_specs=[spec] * len(operands), out_specs=[spec] * 4, out_shape=[out] * 4,
        compiler_params=_params(dimension_semantics=("parallel",)),
    )(*operands)


def _heads(a):
    s, w = a.shape
    return a.reshape(s, w // HEAD_DIM, HEAD_DIM).transpose(1, 0, 2)


def _unheads(a):
    h, s, d = a.shape
    return a.transpose(1, 0, 2).reshape(s, h * d)


def _width_groups(shapes):
    groups = {}
    for idx, (_, c) in enumerate(shapes):
        groups.setdefault(c, []).append(idx)
    return list(groups.values())


def _pack(arrs, lead):
    groups = _width_groups([a.shape[-2:] for a in arrs])
    return [jnp.concatenate([arrs[k] for k in grp], axis=lead) for grp in groups]


def _unpack(bufs, shapes, lead):
    outs = [None] * len(shapes)
    for buf, grp in zip(bufs, _width_groups(shapes)):
        off = 0
        for k in grp:
            outs[k] = lax.slice_in_dim(buf, off, off + shapes[k][0], axis=lead)
            off += shapes[k][0]
    return outs


class LayerWeights:
    FIELDS = ("w_in", "wm", "wo", "wg", "wu", "wd", "wkv")
    STAGE = (0, 0, 1, 2, 2, 2, 2)

    def __init__(self):
        for f in self.FIELDS:
            setattr(self, f, None)


def _pack_small(mix, ffn, kvn, memn, fin, conv):
    d = mix.shape[-1]
    flat = conv.reshape(-1)
    rows_conv = SMALL_ROWS - 11
    flat = jnp.pad(flat, (0, rows_conv * d - flat.shape[0]))
    return jnp.concatenate([mix, ffn, kvn.reshape(1, d), memn.reshape(1, d), fin.reshape(1, d),
                            flat.reshape(rows_conv, d)], axis=0)


def _unpack_small(buf, conv_shape):
    n = math.prod(conv_shape)
    return (buf[0:4], buf[4:8], buf[8], buf[9], buf[10], buf[11:].reshape(-1)[:n].reshape(conv_shape))


def kernel(x, mem, mix_norm, a_in, conv_w, b_in, kv_norm, w_kv_shared, w_mem_kv, w_o, ffn_norm, w_gate, w_up, w_down, mem_norm, final_norm, loss_target, m_mix_norm, m_a_in, m_conv_w, m_b_in, m_kv_norm, m_w_kv_shared, m_w_mem_kv, m_w_o, m_ffn_norm, m_w_gate, m_w_up, m_w_down, m_mem_norm, m_final_norm, v_mix_norm, v_a_in, v_conv_w, v_b_in, v_kv_norm, v_w_kv_shared, v_w_mem_kv, v_w_o, v_ffn_norm, v_w_gate, v_w_up, v_w_down, v_mem_norm, v_final_norm):
    s, d = x.shape[1], x.shape[2]
    n_mem = mem.shape[1]
    depth = mix_norm.shape[0]
    n_a = a_in.shape[0]
    main_w = conv_w.shape[2] * N_CHIPS
    mem_w = w_mem_kv.shape[2] // 2
    ffn_c = w_gate.shape[2]
    kv_c = w_kv_shared.shape[1]
    a_c = a_in.shape[2]
    chip = 2 * lax.axis_index("x") + lax.axis_index("y")

    x0 = x[0]
    mem0 = mem[0]
    tgt = loss_target[0]
    bs = _blk(s, 1024)
    bg = _blk(s, 2048)

    def layer_shards(i, a_or_b, others, kv):
        ws = [a_or_b[0][i] if i < n_a else a_or_b[1][i - n_a]] + [w[i] for w in others]
        return ws + ([kv] if i == n_a - 1 else [])

    shards = [layer_shards(i, (a_in, b_in), (w_mem_kv, w_o, w_gate, w_up, w_down), w_kv_shared) for i in range(depth)]
    shapes = [[w.shape for w in ws] for ws in shards]
    layer_packs = [_pack([w.astype(BF16) for w in ws], 0) for ws in shards]
    first = [sum(len(p) for p in layer_packs[:i]) for i in range(depth + 1)]
    packs = [p for ps in layer_packs for p in ps]
    conv_parts, = _all_gather_chips([conv_w], name="gather_conv_weights")
    conv_full = jnp.concatenate([conv_parts[kk] for kk in range(N_CHIPS)], axis=-1)
    lands = [lax.empty((N_CHIPS, *p.shape), BF16) for p in packs]
    cut = first[1]
    sems0 = _gather_start(packs[:cut], lands[:cut], conv_parts, name="gather_weights_start_l0")
    sems1 = _gather_start(packs[cut:], lands[cut:], sems0[4], name="gather_weights_start_rest")
    packs, lands, started = sems0[2] + sems1[2], sems0[3] + sems1[3], sems1[4]

    def gain(vec):
        return vec.reshape(1, d)

    lw = [LayerWeights() for _ in range(depth)]

    def fetch(i, stage, after):
        groups = _width_groups(shapes[i])
        sel = [gi for gi, grp in enumerate(groups) if min(LayerWeights.STAGE[k] for k in grp) == stage]
        if not sel:
            return
        lo, hi = first[i] + sel[0], first[i] + sel[-1] + 1
        assert hi - lo == len(sel)
        sems, base = (sems0, 0) if i == 0 else (sems1, cut)
        own, landed = _gather_wait(lo - base, packs[lo:hi], lands[lo:hi], sems[0], sems[1], after,
                                   name=f"gather_weights_wait_l{i}_s{stage}")
        for gi, mine, buf in zip(sel, own, landed):
            buf = lax.dynamic_update_slice_in_dim(buf, mine[None], chip, axis=0)
            off = 0
            for k in groups[gi]:
                rows = shapes[i][k][0]
                setattr(lw[i], LayerWeights.FIELDS[k], lax.slice_in_dim(buf, off, off + rows, axis=1))
                off += rows

    mem_n = _rms_fwd(mem0, gain(mem_norm), name="mem_norm_fwd", dep=started)
    saved = []
    k_sh = v_sh = hk = x_kv = None
    xc = x0
    for i in range(depth):
        st = {"x_in": xc}
        h = _rms_fwd(xc, gain(mix_norm[i]), name="mix_norm_fwd")
        fetch(i, 0, h)
        mkv = _mm(Op(mem_n), Op(lw[i].wm, 'r'), name="mem_kv_proj", bm=n_mem, bn=2 * mem_w, bk=d,
                  out_dtype=BF16)
        mem_k, mem_v = _heads(mkv[:, :mem_w]), _heads(mkv[:, mem_w:])
        if i < n_a:
            p = _mm(Op(h), Op(lw[i].w_in, 'c'), name="a_in_proj", bm=bs, bn=a_c, bk=d, out_dtype=BF16)
            y_main = _conv_fwd(p, conv_full[i], name="conv_fwd")
            q_mem = _heads(p[:, 3 * main_w:])
        else:
            p = _mm(Op(h), Op(lw[i].w_in, 'r'), name="b_in_proj", bm=bs, bn=d, bk=d, out_dtype=BF16)
            q_sb = _heads(p[:, :main_w])
            o_sb, o_sb32 = _sb_fwd(q_sb, k_sh, v_sh, name="sb_fwd")
            y_main = _unheads(o_sb)
            q_mem = _heads(p[:, main_w:])
            st.update(q_sb=q_sb, o_sb32=o_sb32)
        y_mem = _mem_fwd(q_mem, mem_k, mem_v, name="mem_attn_fwd")
        y = jnp.concatenate([y_main, _unheads(y_mem)], axis=-1)
        fetch(i, 1, y)
        x_mid = _mm(Op(y), Op(lw[i].wo, 'r'), name="w_o_proj", bm=bs, bn=d, bk=d, out_dtype=F32,
                    res=Op(xc))
        h2 = _rms_fwd(x_mid, gain(ffn_norm[i]), name="ffn_norm_fwd")
        fetch(i, 2, h2)
        gate, up, act = _ffn_in(h2, lw[i].wg, lw[i].wu, name="ffn_gate_up")
        xc = _mm(Op(act, 'c'), Op(lw[i].wd, 'r'), name="w_down_proj", bm=bs, bn=d, bk=ffn_c, out_dtype=F32,
                 res=Op(x_mid))
        st.update(h=h, p=p, mem_k=mem_k, mem_v=mem_v, q_mem=q_mem, y=y, x_mid=x_mid, h2=h2, gate=gate, up=up,
                  act=act)
        saved.append(st)
        if i == n_a - 1:
            x_kv = xc
            hk = _rms_fwd(xc, gain(kv_norm), name="kv_norm_fwd")
            kv = _mm(Op(hk), Op(lw[n_a - 1].wkv, 'c'), name="kv_proj", bm=bs, bn=kv_c, bk=d, out_dtype=BF16)
            k_sh, v_sh = _heads(kv[:, :main_w]), _heads(kv[:, main_w:])

    dx, dg_final, loss_part = _final_loss(xc, gain(final_norm), tgt, name="final_norm_loss")
    loss = lax.psum(loss_part[0, 0], ("x", "y", "c"))

    g_a, g_b, g_m, g_o, g_g, g_u, g_d = ([None] * n_a, [None] * (depth - n_a), [None] * depth, [None] * depth,
                                         [None] * depth, [None] * depth, [None] * depth)
    dg_mix, dg_ffn, dconv = [None] * depth, [None] * depth, [None] * n_a
    in_flight = [[None, None] for _ in range(depth)]
    HALVES = ((0, 1, 2, 6), (3, 4, 5))

    def scatter(grads, name):
        gpacks = _pack(grads, 1)
        return _scatter_start(gpacks, [lax.empty((N_PEER_CHIPS, *g.shape[1:]), BF16) for g in gpacks], name=name)

    dk_sh = dv_sh = None
    dmem_n = None
    g_kv = dg_kv = None
    for i in reversed(range(depth)):
        st = saved[i]
        dgate, dup = _ffn_dact(dx, lw[i].wd, st["gate"], st["up"], name="ffn_dact_gate")
        g_d[i] = _mm(Op(st["act"], 'c'), Op(dx), name="w_down_grad", ta=True, bm=ffn_c, bn=d, bk=bg,
                     out_dtype=BF16, out_chunk='r')
        g_g[i] = _mm(Op(st["h2"]), Op(dgate, 'c'), name="w_gate_grad", ta=True, bm=d, bn=ffn_c, bk=bg,
                     out_dtype=BF16, out_chunk='c')
        g_u[i] = _mm(Op(st["h2"]), Op(dup, 'c'), name="w_up_grad", ta=True, bm=d, bn=ffn_c, bk=bg,
                     out_dtype=BF16, out_chunk='c')
        in_flight[i][1] = scatter([g_g[i], g_u[i], g_d[i]], f"scatter_ffn_grads_start_l{i}")
        dh2 = _ffn_dh(dgate, dup, lw[i].wg, lw[i].wu, name="ffn_dh")
        dx_mid, dg_ffn[i] = _rms_bwd(st["x_mid"], gain(ffn_norm[i]), dh2, dx, name="ffn_norm_bwd",
                                     dep=in_flight[i][1][4])
        dy = _mm(Op(dx_mid), Op(lw[i].wo, 'r'), name="w_o_dy", tb=True, bm=bs, bn=d, bk=d,
                 out_dtype=BF16)
        g_o[i] = _mm(Op(st["y"]), Op(dx_mid), name="w_o_grad", ta=True, bm=d // N_CHIPS, bn=d, bk=bg,
                     out_dtype=BF16, out_chunk='r')
        dq_mem, dmk, dmv = _mem_bwd(st["q_mem"], st["mem_k"], st["mem_v"], _heads(dy[:, main_w:]),
                                    name="mem_attn_bwd")
        dmkv = jnp.concatenate([_unheads(dmk), _unheads(dmv)], axis=-1)
        g_m[i] = _mm(Op(mem_n), Op(dmkv), name="mem_kv_grad", ta=True, bm=d // N_CHIPS, bn=2 * mem_w, bk=n_mem,
                     out_dtype=BF16, out_chunk='r')
        dmem_n = _mm(Op(dmkv), Op(lw[i].wm, 'r'), name="mem_kv_dmem", tb=True, bm=n_mem, bn=d,
                     bk=2 * mem_w, out_dtype=F32, res=None if dmem_n is None else Op(dmem_n))
        if i < n_a:
            db, dc, du, dconv[i] = _conv_bwd(st["p"], conv_full[i], dy[:, :main_w], name="conv_bwd")
            dp = jnp.concatenate([db, dc, du, _unheads(dq_mem)], axis=-1)
            g_a[i] = _mm(Op(st["h"]), Op(dp), name="a_in_grad", ta=True, bm=d, bn=a_c, bk=bg, out_dtype=BF16,
                         out_chunk='c')
            dh = _mm(Op(dp), Op(lw[i].w_in, 'c'), name="a_in_dh", tb=True, bm=bs, bn=d, bk=a_c, out_dtype=F32)
        else:
            dq_sb, dk_sh, dv_sh = _sb_bwd(st["q_sb"], k_sh, v_sh, st["o_sb32"], _heads(dy[:, :main_w]),
                                          dk_sh, dv_sh, name="sb_bwd")
            dp = jnp.concatenate([_unheads(dq_sb), _unheads(dq_mem)], axis=-1)
            g_b[i - n_a] = _mm(Op(st["h"]), Op(dp), name="b_in_grad", ta=True, bm=d // N_CHIPS, bn=d, bk=bg,
                               out_dtype=BF16, out_chunk='r')
            dh = _mm(Op(dp), Op(lw[i].w_in, 'r'), name="b_in_dh", tb=True, bm=bs, bn=d, bk=d,
                     out_dtype=F32)
        in_flight[i][0] = scatter([g_a[i] if i < n_a else g_b[i - n_a], g_m[i], g_o[i]]
                                  + ([g_kv] if i == n_a - 1 else []), f"scatter_mixer_grads_start_l{i}")
        dx, dg_mix[i] = _rms_bwd(st["x_in"], gain(mix_norm[i]), dh, dx_mid, name="mix_norm_bwd",
                                 dep=in_flight[i][0][4])
        if i == n_a:
            dkv = jnp.concatenate([_unheads(dk_sh), _unheads(dv_sh)], axis=-1)
            g_kv = _mm(Op(hk), Op(dkv), name="kv_grad", ta=True, bm=d, bn=kv_c, bk=bg, out_dtype=BF16,
                       out_chunk='c')
            dhk = _mm(Op(dkv), Op(lw[n_a - 1].wkv, 'c'), name="kv_dh", tb=True, bm=bs, bn=d, bk=kv_c, out_dtype=F32)
            dx, dg_kv = _rms_bwd(x_kv, gain(kv_norm), dhk, dx, name="kv_norm_bwd")
    _, dg_mem = _rms_bwd(mem0, gain(mem_norm), dmem_n, None, name="mem_norm_bwd")

    chip_idx = chip.astype(jnp.int32).reshape(1)
    core_sums, spans = [], {}
    for i in reversed(range(depth)):
        for half in (1, 0):
            ssem, rsem, gthru, lthru, _ = in_flight[i][half]
            gthru, lthru = _scatter_wait(gthru, lthru, ssem, rsem, dx, name=f"scatter_grads_wait_l{i}_h{half}")
            spans[i, half] = (len(core_sums), len(core_sums) + len(gthru))
            core_sums += [_sum_partials(chip_idx, g, l, name="sum_chip_partials") for g, l in zip(gthru, lthru)]
    sibling_sums = _sibling_exchange(core_sums, name="exchange_core_sums")

    def layer_parts(sums, i):
        parts = [None] * len(shapes[i])
        for half in (0, 1):
            pos = [k for k in HALVES[half] if k < len(shapes[i])]
            lo, hi = spans[i, half]
            for k, part in zip(pos, _unpack(sums[lo:hi], [shapes[i][k] for k in pos], 0)):
                parts[k] = part
        return parts

    own_parts = [layer_parts(core_sums, i) for i in range(depth)]
    sib_parts = [layer_parts(sibling_sums, i) for i in range(depth)]

    def stacked(parts, pos, layers):
        return jnp.concatenate([parts[i][pos] for i in layers], axis=0)

    a_layers, b_layers, all_layers = range(n_a), range(n_a, depth), range(depth)
    big = [("a_in", a_in, m_a_in, v_a_in, 0, a_layers), ("b_in", b_in, m_b_in, v_b_in, 0, b_layers),
           ("w_kv_shared", w_kv_shared, m_w_kv_shared, v_w_kv_shared, 6, [n_a - 1]),
           ("w_mem_kv", w_mem_kv, m_w_mem_kv, v_w_mem_kv, 1, all_layers), ("w_o", w_o, m_w_o, v_w_o, 2, all_layers),
           ("w_gate", w_gate, m_w_gate, v_w_gate, 3, all_layers), ("w_up", w_up, m_w_up, v_w_up, 4, all_layers),
           ("w_down", w_down, m_w_down, v_w_down, 5, all_layers)]
    results = {}
    for wname, w, mm_, vv_, pos, layers in big:
        flat = lambda t: t.reshape(-1, t.shape[-1])
        outs = _adamw(flat(w), flat(mm_), flat(vv_), stacked(own_parts, pos, layers), stacked(sib_parts, pos, layers),
                      name="adamw")
        results[wname] = [o.reshape(w.shape) for o in outs]

    small_g = _pack_small(jnp.concatenate(dg_mix, axis=0), jnp.concatenate(dg_ffn, axis=0), dg_kv, dg_mem,
                          dg_final, jnp.stack(dconv, axis=0))
    small_g = _small_all_reduce(small_g, name="all_reduce_small_grads")
    conv_shape_full = (n_a, CONV_TAPS, main_w)
    gs = list(_unpack_small(small_g, conv_shape_full))
    gs[5] = lax.dynamic_slice_in_dim(gs[5], chip * conv_w.shape[2], conv_w.shape[2], axis=2)
    small_outs = _adamw(_pack_small(mix_norm, ffn_norm, kv_norm, mem_norm, final_norm, conv_w),
                        _pack_small(m_mix_norm, m_ffn_norm, m_kv_norm, m_mem_norm, m_final_norm, m_conv_w),
                        _pack_small(v_mix_norm, v_ffn_norm, v_kv_norm, v_mem_norm, v_final_norm, v_conv_w),
                        _pack_small(*gs), None, name="adamw_small")
    small_names = ["mix_norm", "ffn_norm", "kv_norm", "mem_norm", "final_norm", "conv_w"]
    for kind, buf in enumerate(small_outs):
        for wname, val in zip(small_names, _unpack_small(buf, conv_w.shape)):
            results.setdefault(wname, [None] * 4)[kind] = val

    order = ["mix_norm", "a_in", "conv_w", "b_in", "kv_norm", "w_kv_shared", "w_mem_kv", "w_o", "ffn_norm",
             "w_gate", "w_up", "w_down", "mem_norm", "final_norm"]
    return (loss, dx[None], *[results[nm][0] for nm in order], *[results[nm][1] for nm in order],
            *[results[nm][2] for nm in order], *[results[nm][3] for nm in order])
```

```python
import math
from typing import NamedTuple, Optional

import jax
import jax.numpy as jnp
from jax import lax
from jax.experimental import pallas as pl
from jax.experimental.pallas import tpu as pltpu

F32 = jnp.float32
BF16 = jnp.bfloat16
MESH = pl.DeviceIdType.MESH

N_CHIPS = 4
N_DEVICES = 8
HEAD_DIM = 64
CONV_TAPS = 3
NORM_EPS = 1e-6
V7X_VMEM_BYTES = 64 * 1024 * 1024
VMEM_LIMIT = V7X_VMEM_BYTES - 8 * 1024 * 1024
LANES = 128
SMALL_ROWS = 16

ADAM_LR = 0.001
ADAM_B1 = 0.9
ADAM_B2 = 0.999
ADAM_EPS = 1e-08
ADAM_WD = 0.01
ADAM_STEP = 10


def _params(**kw):
    return pltpu.CompilerParams(vmem_limit_bytes=VMEM_LIMIT, **kw)


def _blk(n, pref):
    b = min(n, pref)
    assert n % b == 0, (n, pref)
    return b


class Op(NamedTuple):
    arr: jax.Array
    chunk: Optional[str] = None
    layer: Optional[int] = None


def _op_spec(op_chunk, op_layer, shape2, br, bc, pick):
    r, c = shape2
    lead = () if op_layer is None else (op_layer,)
    none = (None,) * len(lead)
    if op_chunk is None:
        def imap(i, j, k):
            rb, cb = pick(i, j, k)
            return (*lead, rb, cb)
        return pl.BlockSpec((*none, br, bc), imap)
    if op_chunk == 'r' and br == N_CHIPS * r:
        def imap(i, j, k):
            rb, cb = pick(i, j, k)
            return (0, *lead, 0, cb)
        return pl.BlockSpec((N_CHIPS, *none, r, bc), imap)
    if op_chunk == 'r':
        n = r // br
        assert r % br == 0

        def imap(i, j, k):
            rb, cb = pick(i, j, k)
            return (rb // n, *lead, rb % n, cb)
        return pl.BlockSpec((None, *none, br, bc), imap)
    n = c // bc
    assert c % bc == 0

    def imap(i, j, k):
        rb, cb = pick(i, j, k)
        return (cb // n, *lead, rb, cb % n)
    return pl.BlockSpec((None, *none, br, bc), imap)


def _mm(a, b, *, name, ta=False, tb=False, bm, bn, bk, out_dtype, out_chunk=None, res=None):
    def dims(op):
        r, c = op.arr.shape[-2:]
        return (r * N_CHIPS if op.chunk == 'r' else r, c * N_CHIPS if op.chunk == 'c' else c)

    ar, ac = dims(a)
    br_, bc_ = dims(b)
    m, ka = (ac, ar) if ta else (ar, ac)
    kb, n = (bc_, br_) if tb else (br_, bc_)
    assert ka == kb, (name, ka, kb)
    assert m % bm == 0 and n % bn == 0 and ka % bk == 0, (name, m, n, ka, bm, bn, bk)
    nk = ka // bk

    if ta:
        a_spec = _op_spec(a.chunk, a.layer, a.arr.shape[-2:], bk, bm, lambda i, j, k: (k, i))
    else:
        a_spec = _op_spec(a.chunk, a.layer, a.arr.shape[-2:], bm, bk, lambda i, j, k: (i, k))
    if tb:
        b_spec = _op_spec(b.chunk, b.layer, b.arr.shape[-2:], bn, bk, lambda i, j, k: (j, k))
    else:
        b_spec = _op_spec(b.chunk, b.layer, b.arr.shape[-2:], bk, bn, lambda i, j, k: (k, j))

    if out_chunk == 'r':
        out_shape2 = (m // N_CHIPS, n)
    elif out_chunk == 'c':
        out_shape2 = (m, n // N_CHIPS)
    else:
        out_shape2 = (m, n)
    o_spec = _op_spec(out_chunk, None, out_shape2, bm, bn, lambda i, j, k: (i, j))
    out_full = out_shape2 if out_chunk is None else (N_CHIPS, *out_shape2)

    contract = (((0 if ta else 1,), (1 if tb else 0,)), ((), ()))
    has_res = res is not None

    def block2(ref):
        v = ref[...]
        return v.reshape(-1, v.shape[-1]).astype(BF16)

    def body(*refs):
        r_ref = refs[2] if has_res else None
        a_ref, b_ref = refs[:2]
        o_ref = refs[3 if has_res else 2]
        prod = lax.dot_general(block2(a_ref), block2(b_ref), contract, preferred_element_type=F32)
        if nk == 1:
            if has_res:
                prod = prod + r_ref[...].astype(F32)
            o_ref[...] = prod.astype(o_ref.dtype)
            return
        acc_ref = refs[-1]
        k = pl.program_id(2)

        @pl.when(k == 0)
        def _():
            acc_ref[...] = prod

        @pl.when(k > 0)
        def _():
            acc_ref[...] += prod

        @pl.when(k == nk - 1)
        def _():
            acc = acc_ref[...]
            if has_res:
                acc = acc + r_ref[...].astype(F32)
            o_ref[...] = acc.astype(o_ref.dtype)

    in_specs = [a_spec, b_spec]
    operands = [a.arr, b.arr]
    if has_res:
        in_specs.append(_op_spec(res.chunk, res.layer, res.arr.shape[-2:], bm, bn, lambda i, j, k: (i, j)))
        operands.append(res.arr)
    return pl.pallas_call(
        body, name=name, grid=(m // bm, n // bn, nk),
        in_specs=in_specs, out_specs=o_spec,
        out_shape=jax.ShapeDtypeStruct(out_full, out_dtype),
        scratch_shapes=[pltpu.VMEM((bm, bn), F32)] if nk > 1 else [],
        compiler_params=_params(dimension_semantics=("parallel", "parallel", "arbitrary")),
    )(*operands)


def _rms_fwd(x, g, *, name, dep=None):
    r, d = x.shape
    bm = _blk(r, 512)

    def body(x_ref, g_ref, *rest):
        o_ref = rest[-1]
        xv = x_ref[...]
        rstd = lax.rsqrt(jnp.mean(xv * xv, axis=-1, keepdims=True) + NORM_EPS)
        o_ref[...] = ((xv * rstd) * g_ref[...]).astype(o_ref.dtype)

    deps = [] if dep is None else [dep]
    return pl.pallas_call(
        body, name=name, grid=(r // bm,),
        in_specs=[pl.BlockSpec((bm, d), lambda i: (i, 0)), pl.BlockSpec((1, d), lambda i: (0, 0))]
        + [pl.BlockSpec(memory_space=pl.ANY)] * len(deps),
        out_specs=pl.BlockSpec((bm, d), lambda i: (i, 0)),
        out_shape=jax.ShapeDtypeStruct((r, d), BF16),
        compiler_params=_params(dimension_semantics=("parallel",)),
    )(x, g, *deps)


def _rms_bwd(x, g, dh, dres, *, name, dep=None):
    r, d = x.shape
    bm = _blk(r, 512)
    has_res = dres is not None
    deps = [] if dep is None else [dep]

    def body(*refs):
        if has_res:
            x_ref, g_ref, dh_ref, dres_ref = refs[:4]
        else:
            x_ref, g_ref, dh_ref = refs[:3]
        dx_ref, dg_ref = refs[-2:]

        @pl.when(pl.program_id(0) == 0)
        def _():
            dg_ref[...] = jnp.zeros_like(dg_ref)

        xv = x_ref[...]
        rstd = lax.rsqrt(jnp.mean(xv * xv, axis=-1, keepdims=True) + NORM_EPS)
        xh = xv * rstd
        dhv = dh_ref[...].astype(F32)
        dg_ref[...] += jnp.sum(dhv * xh, axis=0, keepdims=True)
        dxh = dhv * g_ref[...]
        dx = rstd * (dxh - xh * jnp.mean(dxh * xh, axis=-1, keepdims=True))
        if has_res:
            dx = dres_ref[...] + dx
        dx_ref[...] = dx

    row = pl.BlockSpec((bm, d), lambda i: (i, 0))
    vec = pl.BlockSpec((1, d), lambda i: (0, 0))
    in_specs = [row, vec, row] + ([row] if has_res else []) + [pl.BlockSpec(memory_space=pl.ANY)] * len(deps)
    operands = [x, g, dh] + ([dres] if has_res else []) + deps
    return pl.pallas_call(
        body, name=name, grid=(r // bm,),
        in_specs=in_specs, out_specs=[row, vec],
        out_shape=[jax.ShapeDtypeStruct((r, d), F32), jax.ShapeDtypeStruct((1, d), F32)],
        compiler_params=_params(dimension_semantics=("arbitrary",)),
    )(*operands)


def _final_loss(x, g, tgt, *, name):
    r, d = x.shape
    bm = _blk(r, 512)

    def body(x_ref, g_ref, t_ref, dx_ref, dg_ref, loss_ref):
        @pl.when(pl.program_id(0) == 0)
        def _():
            dg_ref[...] = jnp.zeros_like(dg_ref)
            loss_ref[...] = jnp.zeros_like(loss_ref)

        xv = x_ref[...]
        gv = g_ref[...]
        rstd = lax.rsqrt(jnp.mean(xv * xv, axis=-1, keepdims=True) + NORM_EPS)
        xh = xv * rstd
        diff = xh * gv - t_ref[...]
        loss_ref[...] += jnp.sum(diff * diff) * (0.5 / d)
        dy = diff * (1.0 / d)
        dg_ref[...] += jnp.sum(dy * xh, axis=0, keepdims=True)
        dxh = dy * gv
        dx_ref[...] = rstd * (dxh - xh * jnp.mean(dxh * xh, axis=-1, keepdims=True))

    row = pl.BlockSpec((bm, d), lambda i: (i, 0))
    vec = pl.BlockSpec((1, d), lambda i: (0, 0))
    return pl.pallas_call(
        body, name=name, grid=(r // bm,),
        in_specs=[row, vec, row],
        out_specs=[row, vec, pl.BlockSpec((1, LANES), lambda i: (0, 0))],
        out_shape=[jax.ShapeDtypeStruct((r, d), F32), jax.ShapeDtypeStruct((1, d), F32),
                   jax.ShapeDtypeStruct((1, LANES), F32)],
        compiler_params=_params(dimension_semantics=("arbitrary",)),
    )(x, g, tgt)


def _shift_down(v, k, row):
    return jnp.where(row >= k, pltpu.roll(v, k, 0), 0.0)


def _shift_up(v, k, row, s):
    return jnp.where(row < s - k, pltpu.roll(v, s - k, 0), 0.0)


def _conv_fwd(p, w, *, name):
    s = p.shape[0]
    width = w.shape[1]
    nb = width // LANES

    def body(b_ref, c_ref, u_ref, w_ref, y_ref):
        cu = c_ref[...].astype(F32) * u_ref[...].astype(F32)
        row = lax.broadcasted_iota(jnp.int32, cu.shape, 0)
        wv = w_ref[...]
        conv = wv[2:3] * cu + wv[1:2] * _shift_down(cu, 1, row) + wv[0:1] * _shift_down(cu, 2, row)
        y_ref[...] = (b_ref[...].astype(F32) * conv).astype(y_ref.dtype)

    def col(o):
        return pl.BlockSpec((s, LANES), lambda j: (0, j + o * nb))

    return pl.pallas_call(
        body, name=name, grid=(nb,),
        in_specs=[col(0), col(1), col(2), pl.BlockSpec((CONV_TAPS, LANES), lambda j: (0, j))],
        out_specs=col(0),
        out_shape=jax.ShapeDtypeStruct((s, width), BF16),
        compiler_params=_params(dimension_semantics=("parallel",)),
    )(p, p, p, w)


def _conv_bwd(p, w, dy, *, name):
    s = p.shape[0]
    width = w.shape[1]
    nb = width // LANES

    def body(b_ref, c_ref, u_ref, w_ref, dy_ref, db_ref, dc_ref, du_ref, dw_ref):
        bv = b_ref[...].astype(F32)
        cv = c_ref[...].astype(F32)
        uv = u_ref[...].astype(F32)
        dyv = dy_ref[...].astype(F32)
        cu = cv * uv
        row = lax.broadcasted_iota(jnp.int32, cu.shape, 0)
        wv = w_ref[...]
        cu1 = _shift_down(cu, 1, row)
        cu2 = _shift_down(cu, 2, row)
        conv = wv[2:3] * cu + wv[1:2] * cu1 + wv[0:1] * cu2
        db_ref[...] = (dyv * conv).astype(db_ref.dtype)
        dconv = dyv * bv
        dcu = wv[2:3] * dconv + wv[1:2] * _shift_up(dconv, 1, row, s) + wv[0:1] * _shift_up(dconv, 2, row, s)
        dc_ref[...] = (dcu * uv).astype(dc_ref.dtype)
        du_ref[...] = (dcu * cv).astype(du_ref.dtype)
        dw_ref[0:1, :] = jnp.sum(dconv * cu2, axis=0, keepdims=True)
        dw_ref[1:2, :] = jnp.sum(dconv * cu1, axis=0, keepdims=True)
        dw_ref[2:3, :] = jnp.sum(dconv * cu, axis=0, keepdims=True)

    def col(o):
        return pl.BlockSpec((s, LANES), lambda j: (0, j + o * nb))

    wspec = pl.BlockSpec((CONV_TAPS, LANES), lambda j: (0, j))
    act = jax.ShapeDtypeStruct((s, width), BF16)
    return pl.pallas_call(
        body, name=name, grid=(nb,),
        in_specs=[col(0), col(1), col(2), wspec, col(0)],
        out_specs=[col(0), col(0), col(0), wspec],
        out_shape=[act, act, act, jax.ShapeDtypeStruct((CONV_TAPS, width), F32)],
        compiler_params=_params(dimension_semantics=("parallel",)),
    )(p, p, p, w, dy)


FFN_STRIP = 128


def _ffn_in(h2, wg, wu, *, name):
    s, d = h2.shape
    nc, _, f = wg.shape
    bm = _blk(s, 1024)

    def body(h_ref, wg_ref, wu_ref, g_ref, u_ref, a_ref, g_scr, u_scr):
        hv = h_ref[...]
        g_scr[...] = jnp.dot(hv, wg_ref[...], preferred_element_type=F32)
        u_scr[...] = jnp.dot(hv, wu_ref[...], preferred_element_type=F32)
        for r in range(0, bm, FFN_STRIP):
            rows = pl.ds(r, min(FFN_STRIP, bm))
            gv, uv = g_scr[rows, :], u_scr[rows, :]
            g_ref[rows, :] = gv.astype(BF16)
            u_ref[rows, :] = uv.astype(BF16)
            a_ref[rows, :] = (gv * jax.nn.sigmoid(gv) * uv).astype(BF16)

    wspec = pl.BlockSpec((None, d, f), lambda i, j: (j, 0, 0))
    ospec = pl.BlockSpec((None, bm, f), lambda i, j: (j, i, 0))
    out = jax.ShapeDtypeStruct((nc, s, f), BF16)
    return pl.pallas_call(
        body, name=name, grid=(s // bm, nc),
        in_specs=[pl.BlockSpec((bm, d), lambda i, j: (i, 0)), wspec, wspec], out_specs=[ospec, ospec, ospec],
        out_shape=[out, out, out], scratch_shapes=[pltpu.VMEM((bm, f), F32), pltpu.VMEM((bm, f), F32)],
        compiler_params=_params(dimension_semantics=("parallel", "parallel")),
    )(h2, wg, wu)


def _ffn_dact(dx, wd, gate, up, *, name):
    s, d = dx.shape
    nc, f, _ = wd.shape
    bm = _blk(s, 1024)

    def body(dx_ref, wd_ref, g_ref, u_ref, dg_ref, du_ref, d_scr):
        d_scr[...] = lax.dot_general(dx_ref[...].astype(BF16), wd_ref[...], _NT, preferred_element_type=F32)
        for r in range(0, bm, FFN_STRIP):
            rows = pl.ds(r, min(FFN_STRIP, bm))
            gv = g_ref[rows, :].astype(F32)
            dv = d_scr[rows, :]
            sg = jax.nn.sigmoid(gv)
            dg_ref[rows, :] = (dv * u_ref[rows, :].astype(F32) * (sg * (1.0 + gv * (1.0 - sg)))).astype(BF16)
            du_ref[rows, :] = (dv * (gv * sg)).astype(BF16)

    cspec = pl.BlockSpec((None, bm, f), lambda i, j: (j, i, 0))
    out = jax.ShapeDtypeStruct((nc, s, f), BF16)
    return pl.pallas_call(
        body, name=name, grid=(s // bm, nc),
        in_specs=[pl.BlockSpec((bm, d), lambda i, j: (i, 0)), pl.BlockSpec((None, f, d), lambda i, j: (j, 0, 0)),
                  cspec, cspec],
        out_specs=[cspec, cspec], out_shape=[out, out], scratch_shapes=[pltpu.VMEM((bm, f), F32)],
        compiler_params=_params(dimension_semantics=("parallel", "parallel")),
    )(dx, wd, gate, up)


def _ffn_out(act, wd, res, *, name):
    nc, s, f = act.shape
    d = wd.shape[2]
    bm = _blk(s, 512)

    def body(a_ref, w_ref, r_ref, o_ref):
        acc = r_ref[...]
        for c in range(nc):
            acc = acc + jnp.dot(a_ref[c], w_ref[c], preferred_element_type=F32)
        o_ref[...] = acc

    row = pl.BlockSpec((bm, d), lambda i: (i, 0))
    return pl.pallas_call(
        body, name=name, grid=(s // bm,),
        in_specs=[pl.BlockSpec((nc, bm, f), lambda i: (0, i, 0)),
                  pl.BlockSpec((nc, f, d), lambda i: (0, 0, 0), pipeline_mode=pl.Buffered(1)), row],
        out_specs=row, out_shape=jax.ShapeDtypeStruct((s, d), F32),
        compiler_params=_params(dimension_semantics=("parallel",)),
    )(act, wd, res)


def _proj_dh(dp, w, *, name):
    s, k = dp.shape
    nc, d, kc = w.shape
    assert k == nc * kc and kc % LANES == 0
    bm = _blk(s, 512)

    def body(p_ref, w_ref, o_ref):
        acc = None
        for c in range(nc):
            term = lax.dot_general(p_ref[:, c * kc:(c + 1) * kc].astype(BF16), w_ref[c], _NT,
                                   preferred_element_type=F32)
            acc = term if acc is None else acc + term
        o_ref[...] = acc

    return pl.pallas_call(
        body, name=name, grid=(s // bm,),
        in_specs=[pl.BlockSpec((bm, k), lambda i: (i, 0)),
                  pl.BlockSpec((nc, d, kc), lambda i: (0, 0, 0), pipeline_mode=pl.Buffered(1))],
        out_specs=pl.BlockSpec((bm, d), lambda i: (i, 0)), out_shape=jax.ShapeDtypeStruct((s, d), F32),
        compiler_params=_params(dimension_semantics=("parallel",)),
    )(dp, w)


def _ffn_dh(dgate, dup, wg, wu, *, name):
    nc, s, f = dgate.shape
    d = wg.shape[1]
    bm = _blk(s, 512)

    def body(dg_ref, du_ref, wg_ref, wu_ref, o_ref):
        acc = None
        for c in range(nc):
            for x_ref, w_ref in ((dg_ref, wg_ref), (du_ref, wu_ref)):
                term = lax.dot_general(x_ref[c], w_ref[c], _NT, preferred_element_type=F32)
                acc = term if acc is None else acc + term
        o_ref[...] = acc

    xspec = pl.BlockSpec((nc, bm, f), lambda i: (0, i, 0))
    wspec = pl.BlockSpec((nc, d, f), lambda i: (0, 0, 0), pipeline_mode=pl.Buffered(1))
    return pl.pallas_call(
        body, name=name, grid=(s // bm,), in_specs=[xspec, xspec, wspec, wspec],
        out_specs=pl.BlockSpec((bm, d), lambda i: (i, 0)), out_shape=jax.ShapeDtypeStruct((s, d), F32),
        compiler_params=_params(dimension_semantics=("parallel",)),
    )(dgate, dup, wg, wu)


_NT = (((1,), (1,)), ((), ()))
_TN = (((0,), (0,)), ((), ()))


def _mem_probs(q, k, scale):
    s = lax.dot_general(q, k, _NT, preferred_element_type=F32) * scale
    e = jnp.exp(s - jnp.max(s, axis=-1, keepdims=True))
    return e / jnp.sum(e, axis=-1, keepdims=True)


def _mem_fwd(q, k, v, *, name):
    h, s, d = q.shape
    m = k.shape[1]
    bq = _blk(s, 1024)
    scale = 1.0 / math.sqrt(d)

    def body(q_ref, k_ref, v_ref, o_ref):
        p = _mem_probs(q_ref[...], k_ref[...], scale)
        o_ref[...] = jnp.dot(p.astype(BF16), v_ref[...], preferred_element_type=F32).astype(o_ref.dtype)

    qs = pl.BlockSpec((None, bq, d), lambda hh, i: (hh, i, 0))
    ks = pl.BlockSpec((None, m, d), lambda hh, i: (hh, 0, 0))
    return pl.pallas_call(
        body, name=name, grid=(h, s // bq), in_specs=[qs, ks, ks], out_specs=qs,
        out_shape=jax.ShapeDtypeStruct(q.shape, BF16),
        compiler_params=_params(dimension_semantics=("parallel", "parallel")),
    )(q, k, v)


def _mem_bwd(q, k, v, do, *, name):
    h, s, d = q.shape
    m = k.shape[1]
    bq = _blk(s, 1024)
    scale = 1.0 / math.sqrt(d)

    def body(q_ref, k_ref, v_ref, do_ref, dq_ref, dk_ref, dv_ref):
        @pl.when(pl.program_id(1) == 0)
        def _():
            dk_ref[...] = jnp.zeros_like(dk_ref)
            dv_ref[...] = jnp.zeros_like(dv_ref)

        qv = q_ref[...]
        kv = k_ref[...]
        dov = do_ref[...]
        p = _mem_probs(qv, kv, scale)
        pb = p.astype(BF16)
        dp = lax.dot_general(dov, v_ref[...], _NT, preferred_element_type=F32)
        pf = pb.astype(F32)
        ds = (pf * (dp - jnp.sum(pf * dp, axis=-1, keepdims=True)) * scale).astype(BF16)
        dq_ref[...] = jnp.dot(ds, kv, preferred_element_type=F32).astype(dq_ref.dtype)
        dk_ref[...] += lax.dot_general(ds, qv, _TN, preferred_element_type=F32)
        dv_ref[...] += lax.dot_general(pb, dov, _TN, preferred_element_type=F32)

    qs = pl.BlockSpec((None, bq, d), lambda hh, i: (hh, i, 0))
    ks = pl.BlockSpec((None, m, d), lambda hh, i: (hh, 0, 0))
    kvout = jax.ShapeDtypeStruct(k.shape, F32)
    return pl.pallas_call(
        body, name=name, grid=(h, s // bq), in_specs=[qs, ks, ks, qs], out_specs=[qs, ks, ks],
        out_shape=[jax.ShapeDtypeStruct(q.shape, BF16), kvout, kvout],
        compiler_params=_params(dimension_semantics=("parallel", "arbitrary")),
    )(q, k, v, do)


SB_TILE = 256
SB_STRIP = 32
SB_HEAD_GROUP = 4
SB_BWD_HEAD_GROUP = 4


def _sb_scale(d):
    scale = 1.0 / math.sqrt(d)
    assert math.frexp(scale)[0] == 0.5, "the scale is folded into bf16 q, exact only for a power of two"
    return scale


def _sb_strip_mask(r, t):
    rr = r + lax.broadcasted_iota(jnp.int32, (SB_STRIP, t), 0)
    return lax.broadcasted_iota(jnp.int32, (SB_STRIP, t), 1) < rr


def _neg_abs(z):
    bits = lax.bitcast_convert_type(z, jnp.uint32) | jnp.uint32(0x80000000)
    return lax.bitcast_convert_type(bits, F32)


def _store_split(split_scr, rows, val, t):
    hi = val.astype(BF16)
    split_scr[rows, 0:t] = hi
    split_scr[rows, t:2 * t] = (val - hi.astype(F32)).astype(BF16)


def _sb_logs_phase(z_scr, nsplit_scr, beta_scr, t, diag):
    for r in range(0, t, SB_STRIP):
        rows = pl.ds(r, SB_STRIP)
        z = z_scr[rows, :]
        e = jnp.exp(_neg_abs(z))
        nlog = jnp.maximum(z, 0.0) + jnp.log(1.0 + e)
        if beta_scr is not None:
            inv = pl.reciprocal(1.0 + e, approx=True)
            beta_scr[rows, :] = jnp.where(z >= 0.0, inv, e * inv)
        if diag:
            nlog = jnp.where(_sb_strip_mask(r, t), nlog, 0.0)
        _store_split(nsplit_scr, rows, nlog, t)


def _sb_probs(z_scr, tin_scr, rsum_scr, rows, r, t, diag):
    rs = rsum_scr[rows, :]
    a = jnp.exp(z_scr[rows, :] - tin_scr[rows, :] - rs)
    if diag:
        a = jnp.where(_sb_strip_mask(r, t), a, 0.0)
    rsum_scr[rows, :] = rs + tin_scr[rows, 0:1]
    return a


def _sb_triangle(tri_scr, t):
    row = lax.broadcasted_iota(jnp.int32, (t, t), 0)
    col = lax.broadcasted_iota(jnp.int32, (t, t), 1)
    tri = (row >= col).astype(BF16)
    tri_scr[0:t, :] = tri
    tri_scr[t:2 * t, :] = tri


def _sb_fwd(q, k, v, *, name):
    h, s, d = q.shape
    t = _blk(s, SB_TILE)
    scale = _sb_scale(d)

    def body(q_ref, k_ref, v_ref, o_ref, o32_ref, qs_scr, tri_scr, z_scr, nsplit_scr, tin_scr, a_scr, rsum_scr):
        qi = pl.program_id(1)
        qs_scr[...] = q_ref[...] * scale
        _sb_triangle(tri_scr, t)
        rsum_scr[...] = jnp.zeros_like(rsum_scr)
        o32_ref[...] = jnp.zeros_like(o32_ref)

        def tile(kb, diag):
            keys = pl.ds(pl.multiple_of(kb * t, t), t)
            for g in range(grp):
                z_scr[g] = lax.dot_general(qs_scr[g], k_ref[g, keys, :], _NT, preferred_element_type=F32)
            for g in range(grp):
                _sb_logs_phase(z_scr.at[g], nsplit_scr.at[g], None, t, diag)
                tin_scr[g] = jnp.dot(nsplit_scr[g], tri_scr[...], preferred_element_type=F32)
            for g in range(grp):
                for r in range(0, t, SB_STRIP):
                    rows = pl.ds(r, SB_STRIP)
                    a = _sb_probs(z_scr.at[g], tin_scr.at[g], rsum_scr.at[g], rows, r, t, diag)
                    a_scr[g, rows, :] = a.astype(BF16)
                o32_ref[g] += jnp.dot(a_scr[g], v_ref[g, keys, :], preferred_element_type=F32)

        tile(qi, True)

        def walk(n, carry):
            tile(qi - 1 - n, False)
            return carry

        lax.fori_loop(0, qi, walk, 0)
        o_ref[...] = o32_ref[...].astype(o_ref.dtype)

    grp = SB_HEAD_GROUP
    assert h % grp == 0
    qs = pl.BlockSpec((grp, t, d), lambda hh, i: (hh, i, 0))
    ks = pl.BlockSpec((grp, s, d), lambda hh, i: (hh, 0, 0))
    tile_f32 = pltpu.VMEM((grp, t, t), F32)
    col_f32 = pltpu.VMEM((grp, t, 1), F32)
    return pl.pallas_call(
        body, name=name, grid=(h // grp, s // t), in_specs=[qs, ks, ks], out_specs=[qs, qs],
        out_shape=[jax.ShapeDtypeStruct(q.shape, BF16), jax.ShapeDtypeStruct(q.shape, F32)],
        scratch_shapes=[pltpu.VMEM((grp, t, d), BF16), pltpu.VMEM((2 * t, t), BF16), tile_f32,
                        pltpu.VMEM((grp, t, 2 * t), BF16), tile_f32, pltpu.VMEM((grp, t, t), BF16), col_f32],
        compiler_params=_params(dimension_semantics=("parallel", "parallel")),
    )(q, k, v)


def _sb_bwd(q, k, v, o32, do, dk0, dv0, *, name):
    h, s, d = q.shape
    t = _blk(s, SB_TILE)
    scale = _sb_scale(d)
    has_init = dk0 is not None
    n_in = 7 if has_init else 5

    def body(*refs):
        if has_init:
            q_ref, k_ref, v_ref, o_ref, do_ref, dk0_ref, dv0_ref, dq_ref, dk_ref, dv_ref = refs[:n_in + 3]
        else:
            q_ref, k_ref, v_ref, o_ref, do_ref, dq_ref, dk_ref, dv_ref = refs[:n_in + 3]
        (qs_scr, tri_scr, z_scr, beta_scr, nsplit_scr, tin_scr, da_scr, a_scr, dz_scr,
         dq_scr, rsum_scr, gsum_scr, dsum_scr) = refs[n_in + 3:]
        g_scr, gsplit_scr, gin_scr = da_scr, nsplit_scr, tin_scr
        qi = pl.program_id(1)

        @pl.when(qi == 0)
        def _():
            if has_init:
                dk_ref[...] = dk0_ref[...]
                dv_ref[...] = dv0_ref[...]
            else:
                dk_ref[...] = jnp.zeros_like(dk_ref)
                dv_ref[...] = jnp.zeros_like(dv_ref)

        qs_scr[...] = q_ref[...] * scale
        _sb_triangle(tri_scr, t)
        dsum_scr[...] = jnp.sum(o_ref[...] * do_ref[...].astype(F32), axis=2, keepdims=True)
        rsum_scr[...] = jnp.zeros_like(rsum_scr)
        gsum_scr[...] = jnp.zeros_like(gsum_scr)
        dq_scr[...] = jnp.zeros_like(dq_scr)

        def tile(kb, diag):
            keys = pl.ds(pl.multiple_of(kb * t, t), t)
            for g in range(grp):
                z_scr[g] = lax.dot_general(qs_scr[g], k_ref[g, keys, :], _NT, preferred_element_type=F32)
                da_scr[g] = lax.dot_general(do_ref[g], v_ref[g, keys, :], _NT, preferred_element_type=F32)
            for g in range(grp):
                _sb_logs_phase(z_scr.at[g], nsplit_scr.at[g], beta_scr.at[g], t, diag)
                tin_scr[g] = jnp.dot(nsplit_scr[g], tri_scr[...], preferred_element_type=F32)
            for g in range(grp):
                for r in range(0, t, SB_STRIP):
                    rows = pl.ds(r, SB_STRIP)
                    ab = _sb_probs(z_scr.at[g], tin_scr.at[g], rsum_scr.at[g], rows, r, t, diag).astype(BF16)
                    a_scr[g, rows, :] = ab
                    gv = ab.astype(F32) * da_scr[g, rows, :]
                    g_scr[g, rows, :] = gv
                    _store_split(gsplit_scr.at[g], rows, gv, t)
                gin_scr[g] = jnp.dot(gsplit_scr[g], tri_scr[...], preferred_element_type=F32)
                dv_ref[g, keys, :] += lax.dot_general(a_scr[g], do_ref[g], _TN, preferred_element_type=F32)
            for g in range(grp):
                for r in range(0, t, SB_STRIP):
                    rows = pl.ds(r, SB_STRIP)
                    gs = gsum_scr[g, rows, :]
                    gv = g_scr[g, rows, :]
                    dz = gv - beta_scr[g, rows, :] * ((gv - gin_scr[g, rows, :]) + (dsum_scr[g, rows, :] - gs))
                    if diag:
                        dz = jnp.where(_sb_strip_mask(r, t), dz, 0.0)
                    dz_scr[g, rows, :] = dz.astype(BF16)
                    gsum_scr[g, rows, :] = gs + gin_scr[g, rows, 0:1]
                dq_scr[g] += jnp.dot(dz_scr[g], k_ref[g, keys, :], preferred_element_type=F32)
                dk_ref[g, keys, :] += lax.dot_general(dz_scr[g], qs_scr[g], _TN, preferred_element_type=F32)

        tile(qi, True)

        def walk(n, carry):
            tile(qi - 1 - n, False)
            return carry

        lax.fori_loop(0, qi, walk, 0)
        dq_ref[...] = (dq_scr[...] * scale).astype(dq_ref.dtype)

    grp = SB_BWD_HEAD_GROUP
    assert h % grp == 0
    qs = pl.BlockSpec((grp, t, d), lambda hh, i: (hh, i, 0))
    ks = pl.BlockSpec((grp, s, d), lambda hh, i: (hh, 0, 0), pipeline_mode=pl.Buffered(1))
    in_specs = [qs, ks, ks, qs, qs] + ([ks, ks] if has_init else [])
    operands = [q, k, v, o32, do] + ([dk0, dv0] if has_init else [])
    acc = jax.ShapeDtypeStruct(q.shape, F32)
    tile_f32 = pltpu.VMEM((grp, t, t), F32)
    tile_bf16 = pltpu.VMEM((grp, t, t), BF16)
    split = pltpu.VMEM((grp, t, 2 * t), BF16)
    col_f32 = pltpu.VMEM((grp, t, 1), F32)
    return pl.pallas_call(
        body, name=name, grid=(h // grp, s // t), in_specs=in_specs, out_specs=[qs, ks, ks],
        out_shape=[jax.ShapeDtypeStruct(q.shape, BF16), acc, acc],
        scratch_shapes=[pltpu.VMEM((grp, t, d), BF16), pltpu.VMEM((2 * t, t), BF16), tile_f32, tile_f32, split,
                        tile_f32, tile_f32, tile_bf16, tile_bf16,
                        pltpu.VMEM((grp, t, d), F32), col_f32, col_f32, col_f32],
        compiler_params=_params(dimension_semantics=("parallel", "arbitrary")),
    )(*operands)


def _position():
    x, y, c = lax.axis_index("x"), lax.axis_index("y"), lax.axis_index("c")
    return x, y, c, [(1 - x, y), (x, 1 - y), (1 - x, 1 - y)]


_ANY = pl.BlockSpec(memory_space=pl.ANY)
N_PEER_CHIPS = N_CHIPS - 1


def _all_gather_chips(shards, *, name):
    n = len(shards)

    def body(*refs):
        ins, outs = refs[:n], refs[n:2 * n]
        send_sems, recv_sems, local_sems = refs[2 * n:]
        x, y, c, peers = _position()
        me = 2 * x + y
        copies = []
        for a in range(n):
            copies.append(pltpu.make_async_copy(ins[a], outs[a].at[me], local_sems.at[a]))
            for j, (px, py) in enumerate(peers):
                copies.append(pltpu.make_async_remote_copy(
                    src_ref=ins[a], dst_ref=outs[a].at[me],
                    send_sem=send_sems.at[a * N_PEER_CHIPS + j], recv_sem=recv_sems.at[a * N_PEER_CHIPS + j],
                    device_id=(px, py, c), device_id_type=MESH))
        for cp in copies:
            cp.start()
        for cp in copies:
            cp.wait()

    return pl.pallas_call(
        body, name=name, in_specs=[_ANY] * n, out_specs=[_ANY] * n,
        out_shape=[jax.ShapeDtypeStruct((N_CHIPS, *s.shape), s.dtype) for s in shards],
        scratch_shapes=[pltpu.SemaphoreType.DMA((n * N_PEER_CHIPS,)), pltpu.SemaphoreType.DMA((n * N_PEER_CHIPS,)),
                        pltpu.SemaphoreType.DMA((n,))],
        compiler_params=pltpu.CompilerParams(has_side_effects=True),
    )(*shards)


_HBM = pl.BlockSpec(memory_space=pltpu.HBM)
_SEM = pl.BlockSpec(memory_space=pltpu.SEMAPHORE)
_DATAFLOW = pltpu.SideEffectType.DATAFLOW_SIDE_EFFECTING
_TOKEN = jax.ShapeDtypeStruct((8, LANES), F32)


def _in_hbm(arr):
    return pltpu.with_memory_space_constraint(arr, pltpu.HBM)


def _gather_start(packs, lands, after, *, name):
    n = len(packs)

    def body(*refs):
        src, land = refs[:n], refs[n:2 * n]
        send_sems, recv_sems = refs[2 * n + 1:2 * n + 3]
        token = refs[-1]
        x, y, c, peers = _position()
        for i in range(n):
            for j, (px, py) in enumerate(peers):
                pltpu.make_async_remote_copy(
                    src_ref=src[i], dst_ref=land[i].at[2 * x + y],
                    send_sem=send_sems.at[N_PEER_CHIPS * i + j], recv_sem=recv_sems.at[N_PEER_CHIPS * i + j],
                    device_id=(px, py, c), device_id_type=MESH).start()
        token[...] = jnp.zeros_like(token)

    thru = [pltpu.HBM(a.shape, a.dtype) for a in (*packs, *lands)]
    outs = pl.pallas_call(
        body, name=name,
        out_shape=(pltpu.SemaphoreType.DMA((N_PEER_CHIPS * n,)), pltpu.SemaphoreType.DMA((N_PEER_CHIPS * n,)), *thru,
                   _TOKEN),
        in_specs=[_HBM] * (2 * n) + [_ANY],
        out_specs=(_SEM, _SEM, *[_HBM] * (2 * n), pl.BlockSpec(memory_space=pltpu.VMEM)),
        input_output_aliases={k: 2 + k for k in range(2 * n)},
        compiler_params=pltpu.CompilerParams(has_side_effects=_DATAFLOW),
    )(*[_in_hbm(a) for a in (*packs, *lands)], after)
    return outs[0], outs[1], list(outs[2:2 + n]), list(outs[2 + n:2 + 2 * n]), outs[-1]


def _gather_wait(first, packs, lands, send_sems, recv_sems, after, *, name):
    n = len(packs)

    def body(*refs):
        src, land = refs[:n], refs[n:2 * n]
        send_sems, recv_sems = refs[2 * n:2 * n + 2]
        _, _, c, peers = _position()
        for a in range(n):
            for j, (px, py) in enumerate(peers):
                sem = N_PEER_CHIPS * (first + a) + j
                cp = pltpu.make_async_remote_copy(
                    src_ref=src[a], dst_ref=land[a].at[2 * px + py], send_sem=send_sems.at[sem],
                    recv_sem=recv_sems.at[sem], device_id=(px, py, c), device_id_type=MESH)
                cp.wait_send()
                cp.wait_recv()

    outs = pl.pallas_call(
        body, name=name, out_shape=[pltpu.HBM(a.shape, a.dtype) for a in (*packs, *lands)],
        in_specs=[_HBM] * (2 * n) + [_SEM, _SEM, _ANY], out_specs=[_HBM] * (2 * n),
        input_output_aliases={k: k for k in range(2 * n)},
        compiler_params=pltpu.CompilerParams(has_side_effects=_DATAFLOW),
    )(*packs, *lands, send_sems, recv_sems, after)
    return list(outs[:n]), list(outs[n:])


def _scatter_start(gpacks, lands, *, name):
    n = len(gpacks)

    def body(*refs):
        src, land = refs[:n], refs[n:2 * n]
        send_sems, recv_sems = refs[2 * n:2 * n + 2]
        token = refs[-1]
        _, _, c, peers = _position()
        for a in range(n):
            for j, (px, py) in enumerate(peers):
                pltpu.make_async_remote_copy(
                    src_ref=src[a].at[2 * px + py], dst_ref=land[a].at[j], send_sem=send_sems.at[N_PEER_CHIPS * a + j],
                    recv_sem=recv_sems.at[N_PEER_CHIPS * a + j], device_id=(px, py, c), device_id_type=MESH).start()
        token[...] = jnp.zeros_like(token)

    thru = [pltpu.HBM(a.shape, a.dtype) for a in (*gpacks, *lands)]
    outs = pl.pallas_call(
        body, name=name,
        out_shape=(pltpu.SemaphoreType.DMA((N_PEER_CHIPS * n,)), pltpu.SemaphoreType.DMA((N_PEER_CHIPS * n,)), *thru,
                   _TOKEN),
        in_specs=[_HBM] * (2 * n), out_specs=(_SEM, _SEM, *[_HBM] * (2 * n), pl.BlockSpec(memory_space=pltpu.VMEM)),
        input_output_aliases={k: 2 + k for k in range(2 * n)},
        compiler_params=pltpu.CompilerParams(has_side_effects=_DATAFLOW),
    )(*[_in_hbm(a) for a in (*gpacks, *lands)])
    return outs[0], outs[1], list(outs[2:2 + n]), list(outs[2 + n:2 + 2 * n]), outs[-1]


def _scatter_wait(gpacks, lands, send_sems, recv_sems, after, *, name):
    n = len(gpacks)

    def body(*refs):
        src, land = refs[:n], refs[n:2 * n]
        send_sems, recv_sems = refs[2 * n:2 * n + 2]
        _, _, c, peers = _position()
        for a in range(n):
            for j, (px, py) in enumerate(peers):
                cp = pltpu.make_async_remote_copy(
                    src_ref=src[a].at[2 * px + py], dst_ref=land[a].at[j], send_sem=send_sems.at[N_PEER_CHIPS * a + j],
                    recv_sem=recv_sems.at[N_PEER_CHIPS * a + j], device_id=(px, py, c),
                    device_id_type=MESH)
                cp.wait_send()
                cp.wait_recv()

    outs = pl.pallas_call(
        body, name=name, out_shape=[pltpu.HBM(a.shape, a.dtype) for a in (*gpacks, *lands)],
        in_specs=[_HBM] * (2 * n) + [_SEM, _SEM, _ANY], out_specs=[_HBM] * (2 * n),
        input_output_aliases={k: k for k in range(2 * n)},
        compiler_params=pltpu.CompilerParams(has_side_effects=_DATAFLOW),
    )(*gpacks, *lands, send_sems, recv_sems, after)
    return list(outs[:n]), list(outs[n:])


def _sibling_exchange(arrs, *, name):
    n = len(arrs)

    def body(*refs):
        ins, outs = refs[:n], refs[n:2 * n]
        send_sems, recv_sems = refs[2 * n:]
        x, y, c, _ = _position()
        copies = [pltpu.make_async_remote_copy(
            src_ref=ins[a], dst_ref=outs[a], send_sem=send_sems.at[a], recv_sem=recv_sems.at[a],
            device_id=(x, y, 1 - c), device_id_type=MESH) for a in range(n)]
        for cp in copies:
            cp.start()
        for cp in copies:
            cp.wait()

    return pl.pallas_call(
        body, name=name, in_specs=[_ANY] * n, out_specs=[_ANY] * n,
        out_shape=[jax.ShapeDtypeStruct(a.shape, a.dtype) for a in arrs],
        scratch_shapes=[pltpu.SemaphoreType.DMA((n,)), pltpu.SemaphoreType.DMA((n,))],
        compiler_params=pltpu.CompilerParams(has_side_effects=True),
    )(*arrs)


def _small_all_reduce(v, *, name):
    r, cdim = v.shape

    def body(v_ref, o_ref, slots, send_sems, recv_sems):
        x, y, c, _ = _position()
        me = 4 * x + 2 * y + c
        slots[me] = v_ref[...]
        copies = []
        for j in range(1, N_DEVICES):
            peer = (x ^ ((j >> 2) & 1), y ^ ((j >> 1) & 1), c ^ (j & 1))
            copies.append(pltpu.make_async_remote_copy(
                src_ref=v_ref, dst_ref=slots.at[me], send_sem=send_sems.at[j - 1], recv_sem=recv_sems.at[j - 1],
                device_id=peer, device_id_type=MESH))
        for cp in copies:
            cp.start()
        for cp in copies:
            cp.wait()
        acc = slots[0]
        for dev in range(1, N_DEVICES):
            acc = acc + slots[dev]
        o_ref[...] = acc

    vm = pl.BlockSpec(memory_space=pltpu.VMEM)
    return pl.pallas_call(
        body, name=name, in_specs=[vm], out_specs=vm,
        out_shape=jax.ShapeDtypeStruct(v.shape, F32),
        scratch_shapes=[pltpu.VMEM((N_DEVICES, r, cdim), F32),
                        pltpu.SemaphoreType.DMA((N_DEVICES - 1,)), pltpu.SemaphoreType.DMA((N_DEVICES - 1,))],
        compiler_params=pltpu.CompilerParams(has_side_effects=True),
    )(v)


def _row_block(r, cap, mult):
    best = max(b for b in range(mult, cap + 1, mult) if r % b == 0)
    return best


def _sum_partials(chip_idx, own4, landed, *, name):
    _, r, c = landed.shape
    bm = _row_block(r, 512, 16)

    def body(chip_ref, own_ref, land_ref, o_ref):
        acc = own_ref[...].astype(F32)
        for j in range(N_PEER_CHIPS):
            acc = acc + land_ref[j].astype(F32)
        o_ref[...] = acc

    return pl.pallas_call(
        body, name=name,
        grid_spec=pltpu.PrefetchScalarGridSpec(
            num_scalar_prefetch=1, grid=(r // bm,),
            in_specs=[pl.BlockSpec((None, bm, c), lambda i, chip: (chip[0], i, 0)),
                      pl.BlockSpec((N_PEER_CHIPS, bm, c), lambda i, chip: (0, i, 0))],
            out_specs=pl.BlockSpec((bm, c), lambda i, chip: (i, 0))),
        out_shape=jax.ShapeDtypeStruct((r, c), F32),
        compiler_params=_params(dimension_semantics=("parallel",)),
    )(chip_idx, own4, landed)


def _adamw(w, m, v, g_a, g_b, *, name):
    r, c = w.shape
    bm = _blk(r, 256)
    two = g_b is not None
    bc1 = 1.0 - ADAM_B1 ** ADAM_STEP
    bc2 = 1.0 - ADAM_B2 ** ADAM_STEP

    def body(*refs):
        if two:
            w_ref, m_ref, v_ref, ga_ref, gb_ref, g_ref, d_ref, nm_ref, nv_ref = refs
            g = ga_ref[...] + gb_ref[...]
        else:
            w_ref, m_ref, v_ref, ga_ref, g_ref, d_ref, nm_ref, nv_ref = refs
            g = ga_ref[...]
        nm = ADAM_B1 * m_ref[...] + (1.0 - ADAM_B1) * g
        nv = ADAM_B2 * v_ref[...] + (1.0 - ADAM_B2) * (g * g)
        g_ref[...] = g
        nm_ref[...] = nm
        nv_ref[...] = nv
        d_ref[...] = -ADAM_LR * ((nm / bc1) / (jnp.sqrt(nv / bc2) + ADAM_EPS) + ADAM_WD * w_ref[...])

    spec = pl.BlockSpec((bm, c), lambda i: (i, 0))
    out = jax.ShapeDtypeStruct((r, c), F32)
    operands = [w, m, v, g_a] + ([g_b] if two else [])
    return pl.pallas_call(
        body, name=name, grid=(r // bm,),
        in_specs=[spec] * len(operands), out_specs=[spec] * 4, out_shape=[out] * 4,
        compiler_params=_params(dimension_semantics=("parallel",)),
    )(*operands)


def _heads(a):
    s, w = a.shape
    return a.reshape(s, w // HEAD_DIM, HEAD_DIM).transpose(1, 0, 2)


def _unheads(a):
    h, s, d = a.shape
    return a.transpose(1, 0, 2).reshape(s, h * d)


def _width_groups(shapes):
    groups = {}
    for idx, (_, c) in enumerate(shapes):
        groups.setdefault(c, []).append(idx)
    return list(groups.values())


def _pack(arrs, lead):
    groups = _width_groups([a.shape[-2:] for a in arrs])
    return [jnp.concatenate([arrs[k] for k in grp], axis=lead) for grp in groups]


def _unpack(bufs, shapes, lead):
    outs = [None] * len(shapes)
    for buf, grp in zip(bufs, _width_groups(shapes)):
        off = 0
        for k in grp:
            outs[k] = lax.slice_in_dim(buf, off, off + shapes[k][0], axis=lead)
            off += shapes[k][0]
    return outs


class LayerWeights:
    FIELDS = ("w_in", "wm", "wo", "wg", "wu", "wd", "wkv")
    STAGE = (0, 0, 1, 2, 2, 2, 2)

    def __init__(self):
        for f in self.FIELDS:
            setattr(self, f, None)


def _pack_small(mix, ffn, kvn, memn, fin, conv):
    d = mix.shape[-1]
    flat = conv.reshape(-1)
    rows_conv = SMALL_ROWS - 11
    flat = jnp.pad(flat, (0, rows_conv * d - flat.shape[0]))
    return jnp.concatenate([mix, ffn, kvn.reshape(1, d), memn.reshape(1, d), fin.reshape(1, d),
                            flat.reshape(rows_conv, d)], axis=0)


def _unpack_small(buf, conv_shape):
    n = math.prod(conv_shape)
    return (buf[0:4], buf[4:8], buf[8], buf[9], buf[10], buf[11:].reshape(-1)[:n].reshape(conv_shape))


def kernel(x, mem, mix_norm, a_in, conv_w, b_in, kv_norm, w_kv_shared, w_mem_kv, w_o, ffn_norm, w_gate, w_up, w_down, mem_norm, final_norm, loss_target, m_mix_norm, m_a_in, m_conv_w, m_b_in, m_kv_norm, m_w_kv_shared, m_w_mem_kv, m_w_o, m_ffn_norm, m_w_gate, m_w_up, m_w_down, m_mem_norm, m_final_norm, v_mix_norm, v_a_in, v_conv_w, v_b_in, v_kv_norm, v_w_kv_shared, v_w_mem_kv, v_w_o, v_ffn_norm, v_w_gate, v_w_up, v_w_down, v_mem_norm, v_final_norm):
    s, d = x.shape[1], x.shape[2]
    n_mem = mem.shape[1]
    depth = mix_norm.shape[0]
    n_a = a_in.shape[0]
    main_w = conv_w.shape[2] * N_CHIPS
    mem_w = w_mem_kv.shape[2] // 2
    ffn_c = w_gate.shape[2]
    kv_c = w_kv_shared.shape[1]
    a_c = a_in.shape[2]
    chip = 2 * lax.axis_index("x") + lax.axis_index("y")

    x0 = x[0]
    mem0 = mem[0]
    tgt = loss_target[0]
    bs = _blk(s, 1024)
    bg = _blk(s, 4096)

    def layer_shards(i, a_or_b, others, kv):
        ws = [a_or_b[0][i] if i < n_a else a_or_b[1][i - n_a]] + [w[i] for w in others]
        return ws + ([kv] if i == n_a - 1 else [])

    shards = [layer_shards(i, (a_in, b_in), (w_mem_kv, w_o, w_gate, w_up, w_down), w_kv_shared) for i in range(depth)]
    shapes = [[w.shape for w in ws] for ws in shards]
    layer_packs = [_pack([w.astype(BF16) for w in ws], 0) for ws in shards]
    first = [sum(len(p) for p in layer_packs[:i]) for i in range(depth + 1)]
    packs = [p for ps in layer_packs for p in ps]
    conv_parts, = _all_gather_chips([conv_w], name="gather_conv_weights")
    conv_full = jnp.concatenate([conv_parts[kk] for kk in range(N_CHIPS)], axis=-1)
    lands = [lax.empty((N_CHIPS, *p.shape), BF16) for p in packs]
    cut = first[1]
    sems0 = _gather_start(packs[:cut], lands[:cut], conv_parts, name="gather_weights_start_l0")
    sems1 = _gather_start(packs[cut:], lands[cut:], sems0[4], name="gather_weights_start_rest")
    packs, lands, started = sems0[2] + sems1[2], sems0[3] + sems1[3], sems1[4]

    def gain(vec):
        return vec.reshape(1, d)

    lw = [LayerWeights() for _ in range(depth)]

    def fetch(i, stage, after):
        groups = _width_groups(shapes[i])
        sel = [gi for gi, grp in enumerate(groups) if min(LayerWeights.STAGE[k] for k in grp) == stage]
        if not sel:
            return
        lo, hi = first[i] + sel[0], first[i] + sel[-1] + 1
        assert hi - lo == len(sel)
        sems, base = (sems0, 0) if i == 0 else (sems1, cut)
        own, landed = _gather_wait(lo - base, packs[lo:hi], lands[lo:hi], sems[0], sems[1], after,
                                   name=f"gather_weights_wait_l{i}_s{stage}")
        for gi, mine, buf in zip(sel, own, landed):
            buf = lax.dynamic_update_slice_in_dim(buf, mine[None], chip, axis=0)
            off = 0
            for k in groups[gi]:
                rows = shapes[i][k][0]
                setattr(lw[i], LayerWeights.FIELDS[k], lax.slice_in_dim(buf, off, off + rows, axis=1))
                off += rows

    mem_n = _rms_fwd(mem0, gain(mem_norm), name="mem_norm_fwd", dep=started)
    saved = []
    k_sh = v_sh = hk = x_kv = None
    xc = x0
    for i in range(depth):
        st = {"x_in": xc}
        h = _rms_fwd(xc, gain(mix_norm[i]), name="mix_norm_fwd")
        fetch(i, 0, h)
        mkv = _mm(Op(mem_n), Op(lw[i].wm, 'r'), name="mem_kv_proj", bm=n_mem, bn=2 * mem_w, bk=d,
                  out_dtype=BF16)
        mem_k, mem_v = _heads(mkv[:, :mem_w]), _heads(mkv[:, mem_w:])
        if i < n_a:
            p = _mm(Op(h), Op(lw[i].w_in, 'c'), name="a_in_proj", bm=bs, bn=a_c, bk=d, out_dtype=BF16)
            y_main = _conv_fwd(p, conv_full[i], name="conv_fwd")
            q_mem = _heads(p[:, 3 * main_w:])
        else:
            p = _mm(Op(h), Op(lw[i].w_in, 'r'), name="b_in_proj", bm=bs, bn=d, bk=d, out_dtype=BF16)
            q_sb = _heads(p[:, :main_w])
            o_sb, o_sb32 = _sb_fwd(q_sb, k_sh, v_sh, name="sb_fwd")
            y_main = _unheads(o_sb)
            q_mem = _heads(p[:, main_w:])
            st.update(q_sb=q_sb, o_sb32=o_sb32)
        y_mem = _mem_fwd(q_mem, mem_k, mem_v, name="mem_attn_fwd")
        y = jnp.concatenate([y_main, _unheads(y_mem)], axis=-1)
        fetch(i, 1, y)
        x_mid = _mm(Op(y), Op(lw[i].wo, 'r'), name="w_o_proj", bm=bs, bn=d, bk=d, out_dtype=F32,
                    res=Op(xc))
        h2 = _rms_fwd(x_mid, gain(ffn_norm[i]), name="ffn_norm_fwd")
        fetch(i, 2, h2)
        gate, up, act = _ffn_in(h2, lw[i].wg, lw[i].wu, name="ffn_gate_up")
        xc = _ffn_out(act, lw[i].wd, x_mid, name="ffn_down")
        st.update(h=h, p=p, mem_k=mem_k, mem_v=mem_v, q_mem=q_mem, y=y, x_mid=x_mid, h2=h2, gate=gate, up=up,
                  act=act)
        saved.append(st)
        if i == n_a - 1:
            x_kv = xc
            hk = _rms_fwd(xc, gain(kv_norm), name="kv_norm_fwd")
            kv = _mm(Op(hk), Op(lw[n_a - 1].wkv, 'c'), name="kv_proj", bm=bs, bn=kv_c, bk=d, out_dtype=BF16)
            k_sh, v_sh = _heads(kv[:, :main_w]), _heads(kv[:, main_w:])

    dx, dg_final, loss_part = _final_loss(xc, gain(final_norm), tgt, name="final_norm_loss")
    loss = lax.psum(loss_part[0, 0], ("x", "y", "c"))

    g_a, g_b, g_m, g_o, g_g, g_u, g_d = ([None] * n_a, [None] * (depth - n_a), [None] * depth, [None] * depth,
                                         [None] * depth, [None] * depth, [None] * depth)
    dg_mix, dg_ffn, dconv = [None] * depth, [None] * depth, [None] * n_a
    in_flight = [[None, None] for _ in range(depth)]
    HALVES = ((0, 1, 2, 6), (3, 4, 5))

    def scatter(grads, name):
        gpacks = _pack(grads, 1)
        return _scatter_start(gpacks, [lax.empty((N_PEER_CHIPS, *g.shape[1:]), BF16) for g in gpacks], name=name)

    dk_sh = dv_sh = None
    dmem_n = None
    g_kv = dg_kv = None
    for i in reversed(range(depth)):
        st = saved[i]
        dgate, dup = _ffn_dact(dx, lw[i].wd, st["gate"], st["up"], name="ffn_dact_gate")
        g_d[i] = _mm(Op(st["act"], 'c'), Op(dx), name="w_down_grad", ta=True, bm=ffn_c, bn=d, bk=bg,
                     out_dtype=BF16, out_chunk='r')
        g_g[i] = _mm(Op(st["h2"]), Op(dgate, 'c'), name="w_gate_grad", ta=True, bm=d, bn=ffn_c, bk=bg,
                     out_dtype=BF16, out_chunk='c')
        g_u[i] = _mm(Op(st["h2"]), Op(dup, 'c'), name="w_up_grad", ta=True, bm=d, bn=ffn_c, bk=bg,
                     out_dtype=BF16, out_chunk='c')
        in_flight[i][1] = scatter([g_g[i], g_u[i], g_d[i]], f"scatter_ffn_grads_start_l{i}")
        dh2 = _ffn_dh(dgate, dup, lw[i].wg, lw[i].wu, name="ffn_dh")
        dx_mid, dg_ffn[i] = _rms_bwd(st["x_mid"], gain(ffn_norm[i]), dh2, dx, name="ffn_norm_bwd",
                                     dep=in_flight[i][1][4])
        dy = _mm(Op(dx_mid), Op(lw[i].wo, 'r'), name="w_o_dy", tb=True, bm=bs, bn=d, bk=d,
                 out_dtype=BF16)
        g_o[i] = _mm(Op(st["y"]), Op(dx_mid), name="w_o_grad", ta=True, bm=d // N_CHIPS, bn=d, bk=bg,
                     out_dtype=BF16, out_chunk='r')
        dq_mem, dmk, dmv = _mem_bwd(st["q_mem"], st["mem_k"], st["mem_v"], _heads(dy[:, main_w:]),
                                    name="mem_attn_bwd")
        dmkv = jnp.concatenate([_unheads(dmk), _unheads(dmv)], axis=-1)
        g_m[i] = _mm(Op(mem_n), Op(dmkv), name="mem_kv_grad", ta=True, bm=d // N_CHIPS, bn=2 * mem_w, bk=n_mem,
                     out_dtype=BF16, out_chunk='r')
        dmem_n = _mm(Op(dmkv), Op(lw[i].wm, 'r'), name="mem_kv_dmem", tb=True, bm=n_mem, bn=d,
                     bk=2 * mem_w, out_dtype=F32, res=None if dmem_n is None else Op(dmem_n))
        if i < n_a:
            db, dc, du, dconv[i] = _conv_bwd(st["p"], conv_full[i], dy[:, :main_w], name="conv_bwd")
            dp = jnp.concatenate([db, dc, du, _unheads(dq_mem)], axis=-1)
            g_a[i] = _mm(Op(st["h"]), Op(dp), name="a_in_grad", ta=True, bm=d, bn=a_c, bk=bg, out_dtype=BF16,
                         out_chunk='c')
            dh = _proj_dh(dp, lw[i].w_in, name="a_in_dh")
        else:
            dq_sb, dk_sh, dv_sh = _sb_bwd(st["q_sb"], k_sh, v_sh, st["o_sb32"], _heads(dy[:, :main_w]),
                                          dk_sh, dv_sh, name="sb_bwd")
            dp = jnp.concatenate([_unheads(dq_sb), _unheads(dq_mem)], axis=-1)
            g_b[i - n_a] = _mm(Op(st["h"]), Op(dp), name="b_in_grad", ta=True, bm=d // N_CHIPS, bn=d, bk=bg,
                               out_dtype=BF16, out_chunk='r')
            dh = _mm(Op(dp), Op(lw[i].w_in, 'r'), name="b_in_dh", tb=True, bm=bs, bn=d, bk=d,
                     out_dtype=F32)
        in_flight[i][0] = scatter([g_a[i] if i < n_a else g_b[i - n_a], g_m[i], g_o[i]]
                                  + ([g_kv] if i == n_a - 1 else []), f"scatter_mixer_grads_start_l{i}")
        dx, dg_mix[i] = _rms_bwd(st["x_in"], gain(mix_norm[i]), dh, dx_mid, name="mix_norm_bwd",
                                 dep=in_flight[i][0][4])
        if i == n_a:
            dkv = jnp.concatenate([_unheads(dk_sh), _unheads(dv_sh)], axis=-1)
            g_kv = _mm(Op(hk), Op(dkv), name="kv_grad", ta=True, bm=d, bn=kv_c, bk=bg, out_dtype=BF16,
                       out_chunk='c')
            dhk = _proj_dh(dkv, lw[n_a - 1].wkv, name="kv_dh")
            dx, dg_kv = _rms_bwd(x_kv, gain(kv_norm), dhk, dx, name="kv_norm_bwd")
    _, dg_mem = _rms_bwd(mem0, gain(mem_norm), dmem_n, None, name="mem_norm_bwd")

    chip_idx = chip.astype(jnp.int32).reshape(1)
    core_sums, spans = [], {}
    for i in reversed(range(depth)):
        for half in (1, 0):
            ssem, rsem, gthru, lthru, _ = in_flight[i][half]
            gthru, lthru = _scatter_wait(gthru, lthru, ssem, rsem, dx, name=f"scatter_grads_wait_l{i}_h{half}")
            spans[i, half] = (len(core_sums), len(core_sums) + len(gthru))
            core_sums += [_sum_partials(chip_idx, g, l, name="sum_chip_partials") for g, l in zip(gthru, lthru)]
    sibling_sums = _sibling_exchange(core_sums, name="exchange_core_sums")

    def layer_parts(sums, i):
        parts = [None] * len(shapes[i])
        for half in (0, 1):
            pos = [k for k in HALVES[half] if k < len(shapes[i])]
            lo, hi = spans[i, half]
            for k, part in zip(pos, _unpack(sums[lo:hi], [shapes[i][k] for k in pos], 0)):
                parts[k] = part
        return parts

    own_parts = [layer_parts(core_sums, i) for i in range(depth)]
    sib_parts = [layer_parts(sibling_sums, i) for i in range(depth)]

    def stacked(parts, pos, layers):
        return jnp.concatenate([parts[i][pos] for i in layers], axis=0)

    a_layers, b_layers, all_layers = range(n_a), range(n_a, depth), range(depth)
    big = [("a_in", a_in, m_a_in, v_a_in, 0, a_layers), ("b_in", b_in, m_b_in, v_b_in, 0, b_layers),
           ("w_kv_shared", w_kv_shared, m_w_kv_shared, v_w_kv_shared, 6, [n_a - 1]),
           ("w_mem_kv", w_mem_kv, m_w_mem_kv, v_w_mem_kv, 1, all_layers), ("w_o", w_o, m_w_o, v_w_o, 2, all_layers),
           ("w_gate", w_gate, m_w_gate, v_w_gate, 3, all_layers), ("w_up", w_up, m_w_up, v_w_up, 4, all_layers),
           ("w_down", w_down, m_w_down, v_w_down, 5, all_layers)]
    results = {}
    for wname, w, mm_, vv_, pos, layers in big:
        flat = lambda t: t.reshape(-1, t.shape[-1])
        outs = _adamw(flat(w), flat(mm_), flat(vv_), stacked(own_parts, pos, layers), stacked(sib_parts, pos, layers),
                      name="adamw")
        results[wname] = [o.reshape(w.shape) for o in outs]

    small_g = _pack_small(jnp.concatenate(dg_mix, axis=0), jnp.concatenate(dg_ffn, axis=0), dg_kv, dg_mem,
                          dg_final, jnp.stack(dconv, axis=0))
    small_g = _small_all_reduce(small_g, name="all_reduce_small_grads")
    conv_shape_full = (n_a, CONV_TAPS, main_w)
    gs = list(_unpack_small(small_g, conv_shape_full))
    gs[5] = lax.dynamic_slice_in_dim(gs[5], chip * conv_w.shape[2], conv_w.shape[2], axis=2)
    small_outs = _adamw(_pack_small(mix_norm, ffn_norm, kv_norm, mem_norm, final_norm, conv_w),
                        _pack_small(m_mix_norm, m_ffn_norm, m_kv_norm, m_mem_norm, m_final_norm, m_conv_w),
                        _pack_small(v_mix_norm, v_ffn_norm, v_kv_norm, v_mem_norm, v_final_norm, v_conv_w),
                        _pack_small(*gs), None, name="adamw_small")
    small_names = ["mix_norm", "ffn_norm", "kv_norm", "mem_norm", "final_norm", "conv_w"]
    for kind, buf in enumerate(small_outs):
        for wname, val in zip(small_names, _unpack_small(buf, conv_w.shape)):
            results.setdefault(wname, [None] * 4)[kind] = val

    order = ["mix_norm", "a_in", "conv_w", "b_in", "kv_norm", "w_kv_shared", "w_mem_kv", "w_o", "ffn_norm",
             "w_gate", "w_up", "w_down", "mem_norm", "final_norm"]
    return (loss, dx[None], *[results[nm][0] for nm in order], *[results[nm][1] for nm in order],
            *[results[nm][2] for nm in order], *[results[nm][3] for nm in order])
```

```python
import math
from typing import NamedTuple, Optional

import jax
import jax.numpy as jnp
from jax import lax
from jax.experimental import pallas as pl
from jax.experimental.pallas import tpu as pltpu

F32 = jnp.float32
BF16 = jnp.bfloat16
MESH = pl.DeviceIdType.MESH

N_CHIPS = 4
N_DEVICES = 8
HEAD_DIM = 64
CONV_TAPS = 3
NORM_EPS = 1e-6
V7X_VMEM_BYTES = 64 * 1024 * 1024
VMEM_LIMIT = V7X_VMEM_BYTES - 8 * 1024 * 1024
LANES = 128
SMALL_ROWS = 16

ADAM_LR = 0.001
ADAM_B1 = 0.9
ADAM_B2 = 0.999
ADAM_EPS = 1e-08
ADAM_WD = 0.01
ADAM_STEP = 10


def _params(**kw):
    return pltpu.CompilerParams(vmem_limit_bytes=VMEM_LIMIT, **kw)


def _blk(n, pref):
    b = min(n, pref)
    assert n % b == 0, (n, pref)
    return b


class Op(NamedTuple):
    arr: jax.Array
    chunk: Optional[str] = None
    layer: Optional[int] = None


def _op_spec(op_chunk, op_layer, shape2, br, bc, pick):
    r, c = shape2
    lead = () if op_layer is None else (op_layer,)
    none = (None,) * len(lead)
    if op_chunk is None:
        def imap(i, j, k):
            rb, cb = pick(i, j, k)
            return (*lead, rb, cb)
        return pl.BlockSpec((*none, br, bc), imap)
    if op_chunk == 'r' and br == N_CHIPS * r:
        def imap(i, j, k):
            rb, cb = pick(i, j, k)
            return (0, *lead, 0, cb)
        return pl.BlockSpec((N_CHIPS, *none, r, bc), imap)
    if op_chunk == 'r':
        n = r // br
        assert r % br == 0

        def imap(i, j, k):
            rb, cb = pick(i, j, k)
            return (rb // n, *lead, rb % n, cb)
        return pl.BlockSpec((None, *none, br, bc), imap)
    n = c // bc
    assert c % bc == 0

    def imap(i, j, k):
        rb, cb = pick(i, j, k)
        return (cb // n, *lead, rb, cb % n)
    return pl.BlockSpec((None, *none, br, bc), imap)


def _mm(a, b, *, name, ta=False, tb=False, bm, bn, bk, out_dtype, out_chunk=None, res=None):
    def dims(op):
        r, c = op.arr.shape[-2:]
        return (r * N_CHIPS if op.chunk == 'r' else r, c * N_CHIPS if op.chunk == 'c' else c)

    ar, ac = dims(a)
    br_, bc_ = dims(b)
    m, ka = (ac, ar) if ta else (ar, ac)
    kb, n = (bc_, br_) if tb else (br_, bc_)
    assert ka == kb, (name, ka, kb)
    assert m % bm == 0 and n % bn == 0 and ka % bk == 0, (name, m, n, ka, bm, bn, bk)
    nk = ka // bk

    if ta:
        a_spec = _op_spec(a.chunk, a.layer, a.arr.shape[-2:], bk, bm, lambda i, j, k: (k, i))
    else:
        a_spec = _op_spec(a.chunk, a.layer, a.arr.shape[-2:], bm, bk, lambda i, j, k: (i, k))
    if tb:
        b_spec = _op_spec(b.chunk, b.layer, b.arr.shape[-2:], bn, bk, lambda i, j, k: (j, k))
    else:
        b_spec = _op_spec(b.chunk, b.layer, b.arr.shape[-2:], bk, bn, lambda i, j, k: (k, j))

    if out_chunk == 'r':
        out_shape2 = (m // N_CHIPS, n)
    elif out_chunk == 'c':
        out_shape2 = (m, n // N_CHIPS)
    else:
        out_shape2 = (m, n)
    o_spec = _op_spec(out_chunk, None, out_shape2, bm, bn, lambda i, j, k: (i, j))
    out_full = out_shape2 if out_chunk is None else (N_CHIPS, *out_shape2)

    contract = (((0 if ta else 1,), (1 if tb else 0,)), ((), ()))
    has_res = res is not None

    def block2(ref):
        v = ref[...]
        return v.reshape(-1, v.shape[-1]).astype(BF16)

    def body(*refs):
        r_ref = refs[2] if has_res else None
        a_ref, b_ref = refs[:2]
        o_ref = refs[3 if has_res else 2]
        prod = lax.dot_general(block2(a_ref), block2(b_ref), contract, preferred_element_type=F32)
        if nk == 1:
            if has_res:
                prod = prod + r_ref[...].astype(F32)
            o_ref[...] = prod.astype(o_ref.dtype)
            return
        acc_ref = refs[-1]
        k = pl.program_id(2)

        @pl.when(k == 0)
        def _():
            acc_ref[...] = prod

        @pl.when(k > 0)
        def _():
            acc_ref[...] += prod

        @pl.when(k == nk - 1)
        def _():
            acc = acc_ref[...]
            if has_res:
                acc = acc + r_ref[...].astype(F32)
            o_ref[...] = acc.astype(o_ref.dtype)

    in_specs = [a_spec, b_spec]
    operands = [a.arr, b.arr]
    if has_res:
        in_specs.append(_op_spec(res.chunk, res.layer, res.arr.shape[-2:], bm, bn, lambda i, j, k: (i, j)))
        operands.append(res.arr)
    return pl.pallas_call(
        body, name=name, grid=(m // bm, n // bn, nk),
        in_specs=in_specs, out_specs=o_spec,
        out_shape=jax.ShapeDtypeStruct(out_full, out_dtype),
        scratch_shapes=[pltpu.VMEM((bm, bn), F32)] if nk > 1 else [],
        compiler_params=_params(dimension_semantics=("parallel", "parallel", "arbitrary")),
    )(*operands)


def _rms_fwd(x, g, *, name, dep=None):
    r, d = x.shape
    bm = _blk(r, 1024)

    def body(x_ref, g_ref, *rest):
        o_ref = rest[-1]
        xv = x_ref[...]
        rstd = lax.rsqrt(jnp.mean(xv * xv, axis=-1, keepdims=True) + NORM_EPS)
        o_ref[...] = ((xv * rstd) * g_ref[...]).astype(o_ref.dtype)

    deps = [] if dep is None else [dep]
    return pl.pallas_call(
        body, name=name, grid=(r // bm,),
        in_specs=[pl.BlockSpec((bm, d), lambda i: (i, 0)), pl.BlockSpec((1, d), lambda i: (0, 0))]
        + [pl.BlockSpec(memory_space=pl.ANY)] * len(deps),
        out_specs=pl.BlockSpec((bm, d), lambda i: (i, 0)),
        out_shape=jax.ShapeDtypeStruct((r, d), BF16),
        compiler_params=_params(dimension_semantics=("parallel",)),
    )(x, g, *deps)


def _rms_bwd(x, g, dh, dres, *, name, dep=None):
    r, d = x.shape
    bm = _blk(r, 1024)
    has_res = dres is not None
    deps = [] if dep is None else [dep]

    def body(*refs):
        if has_res:
            x_ref, g_ref, dh_ref, dres_ref = refs[:4]
        else:
            x_ref, g_ref, dh_ref = refs[:3]
        dx_ref, dg_ref = refs[-2:]

        @pl.when(pl.program_id(0) == 0)
        def _():
            dg_ref[...] = jnp.zeros_like(dg_ref)

        xv = x_ref[...]
        rstd = lax.rsqrt(jnp.mean(xv * xv, axis=-1, keepdims=True) + NORM_EPS)
        xh = xv * rstd
        dhv = dh_ref[...].astype(F32)
        dg_ref[...] += jnp.sum(dhv * xh, axis=0, keepdims=True)
        dxh = dhv * g_ref[...]
        dx = rstd * (dxh - xh * jnp.mean(dxh * xh, axis=-1, keepdims=True))
        if has_res:
            dx = dres_ref[...] + dx
        dx_ref[...] = dx

    row = pl.BlockSpec((bm, d), lambda i: (i, 0))
    vec = pl.BlockSpec((1, d), lambda i: (0, 0))
    in_specs = [row, vec, row] + ([row] if has_res else []) + [pl.BlockSpec(memory_space=pl.ANY)] * len(deps)
    operands = [x, g, dh] + ([dres] if has_res else []) + deps
    return pl.pallas_call(
        body, name=name, grid=(r // bm,),
        in_specs=in_specs, out_specs=[row, vec],
        out_shape=[jax.ShapeDtypeStruct((r, d), F32), jax.ShapeDtypeStruct((1, d), F32)],
        compiler_params=_params(dimension_semantics=("arbitrary",)),
    )(*operands)


def _final_loss(x, g, tgt, *, name):
    r, d = x.shape
    bm = _blk(r, 1024)

    def body(x_ref, g_ref, t_ref, dx_ref, dg_ref, loss_ref):
        @pl.when(pl.program_id(0) == 0)
        def _():
            dg_ref[...] = jnp.zeros_like(dg_ref)
            loss_ref[...] = jnp.zeros_like(loss_ref)

        xv = x_ref[...]
        gv = g_ref[...]
        rstd = lax.rsqrt(jnp.mean(xv * xv, axis=-1, keepdims=True) + NORM_EPS)
        xh = xv * rstd
        diff = xh * gv - t_ref[...]
        loss_ref[...] += jnp.sum(diff * diff) * (0.5 / d)
        dy = diff * (1.0 / d)
        dg_ref[...] += jnp.sum(dy * xh, axis=0, keepdims=True)
        dxh = dy * gv
        dx_ref[...] = rstd * (dxh - xh * jnp.mean(dxh * xh, axis=-1, keepdims=True))

    row = pl.BlockSpec((bm, d), lambda i: (i, 0))
    vec = pl.BlockSpec((1, d), lambda i: (0, 0))
    return pl.pallas_call(
        body, name=name, grid=(r // bm,),
        in_specs=[row, vec, row],
        out_specs=[row, vec, pl.BlockSpec((1, LANES), lambda i: (0, 0))],
        out_shape=[jax.ShapeDtypeStruct((r, d), F32), jax.ShapeDtypeStruct((1, d), F32),
                   jax.ShapeDtypeStruct((1, LANES), F32)],
        compiler_params=_params(dimension_semantics=("arbitrary",)),
    )(x, g, tgt)


def _shift_down(v, k, row):
    return jnp.where(row >= k, pltpu.roll(v, k, 0), 0.0)


def _shift_up(v, k, row, s):
    return jnp.where(row < s - k, pltpu.roll(v, s - k, 0), 0.0)


def _conv_fwd(p, w, *, name):
    s = p.shape[0]
    width = w.shape[1]
    nb = width // LANES

    def body(b_ref, c_ref, u_ref, w_ref, y_ref):
        cu = c_ref[...].astype(F32) * u_ref[...].astype(F32)
        row = lax.broadcasted_iota(jnp.int32, cu.shape, 0)
        wv = w_ref[...]
        conv = wv[2:3] * cu + wv[1:2] * _shift_down(cu, 1, row) + wv[0:1] * _shift_down(cu, 2, row)
        y_ref[...] = (b_ref[...].astype(F32) * conv).astype(y_ref.dtype)

    def col(o):
        return pl.BlockSpec((s, LANES), lambda j: (0, j + o * nb))

    return pl.pallas_call(
        body, name=name, grid=(nb,),
        in_specs=[col(0), col(1), col(2), pl.BlockSpec((CONV_TAPS, LANES), lambda j: (0, j))],
        out_specs=col(0),
        out_shape=jax.ShapeDtypeStruct((s, width), BF16),
        compiler_params=_params(dimension_semantics=("parallel",)),
    )(p, p, p, w)


def _conv_bwd(p, w, dy, *, name):
    s = p.shape[0]
    width = w.shape[1]
    nb = width // LANES

    def body(b_ref, c_ref, u_ref, w_ref, dy_ref, db_ref, dc_ref, du_ref, dw_ref):
        bv = b_ref[...].astype(F32)
        cv = c_ref[...].astype(F32)
        uv = u_ref[...].astype(F32)
        dyv = dy_ref[...].astype(F32)
        cu = cv * uv
        row = lax.broadcasted_iota(jnp.int32, cu.shape, 0)
        wv = w_ref[...]
        cu1 = _shift_down(cu, 1, row)
        cu2 = _shift_down(cu, 2, row)
        conv = wv[2:3] * cu + wv[1:2] * cu1 + wv[0:1] * cu2
        db_ref[...] = (dyv * conv).astype(db_ref.dtype)
        dconv = dyv * bv
        dcu = wv[2:3] * dconv + wv[1:2] * _shift_up(dconv, 1, row, s) + wv[0:1] * _shift_up(dconv, 2, row, s)
        dc_ref[...] = (dcu * uv).astype(dc_ref.dtype)
        du_ref[...] = (dcu * cv).astype(du_ref.dtype)
        dw_ref[0:1, :] = jnp.sum(dconv * cu2, axis=0, keepdims=True)
        dw_ref[1:2, :] = jnp.sum(dconv * cu1, axis=0, keepdims=True)
        dw_ref[2:3, :] = jnp.sum(dconv * cu, axis=0, keepdims=True)

    def col(o):
        return pl.BlockSpec((s, LANES), lambda j: (0, j + o * nb))

    wspec = pl.BlockSpec((CONV_TAPS, LANES), lambda j: (0, j))
    act = jax.ShapeDtypeStruct((s, width), BF16)
    return pl.pallas_call(
        body, name=name, grid=(nb,),
        in_specs=[col(0), col(1), col(2), wspec, col(0)],
        out_specs=[col(0), col(0), col(0), wspec],
        out_shape=[act, act, act, jax.ShapeDtypeStruct((CONV_TAPS, width), F32)],
        compiler_params=_params(dimension_semantics=("parallel",)),
    )(p, p, p, w, dy)


FFN_STRIP = 128


def _ffn_in(h2, wg, wu, *, name):
    s, d = h2.shape
    nc, _, f = wg.shape
    bm = _blk(s, 1024)

    def body(h_ref, wg_ref, wu_ref, g_ref, u_ref, a_ref, g_scr, u_scr):
        hv = h_ref[...]
        g_scr[...] = jnp.dot(hv, wg_ref[...], preferred_element_type=F32)
        u_scr[...] = jnp.dot(hv, wu_ref[...], preferred_element_type=F32)
        for r in range(0, bm, FFN_STRIP):
            rows = pl.ds(r, min(FFN_STRIP, bm))
            gv, uv = g_scr[rows, :], u_scr[rows, :]
            g_ref[rows, :] = gv.astype(BF16)
            u_ref[rows, :] = uv.astype(BF16)
            a_ref[rows, :] = (gv * jax.nn.sigmoid(gv) * uv).astype(BF16)

    wspec = pl.BlockSpec((None, d, f), lambda i, j: (j, 0, 0))
    ospec = pl.BlockSpec((None, bm, f), lambda i, j: (j, i, 0))
    out = jax.ShapeDtypeStruct((nc, s, f), BF16)
    return pl.pallas_call(
        body, name=name, grid=(s // bm, nc),
        in_specs=[pl.BlockSpec((bm, d), lambda i, j: (i, 0)), wspec, wspec], out_specs=[ospec, ospec, ospec],
        out_shape=[out, out, out], scratch_shapes=[pltpu.VMEM((bm, f), F32), pltpu.VMEM((bm, f), F32)],
        compiler_params=_params(dimension_semantics=("parallel", "parallel")),
    )(h2, wg, wu)


def _ffn_dact(dx, wd, gate, up, *, name):
    s, d = dx.shape
    nc, f, _ = wd.shape
    bm = _blk(s, 1024)

    def body(dx_ref, wd_ref, g_ref, u_ref, dg_ref, du_ref, d_scr):
        d_scr[...] = lax.dot_general(dx_ref[...].astype(BF16), wd_ref[...], _NT, preferred_element_type=F32)
        for r in range(0, bm, FFN_STRIP):
            rows = pl.ds(r, min(FFN_STRIP, bm))
            gv = g_ref[rows, :].astype(F32)
            dv = d_scr[rows, :]
            sg = jax.nn.sigmoid(gv)
            dg_ref[rows, :] = (dv * u_ref[rows, :].astype(F32) * (sg * (1.0 + gv * (1.0 - sg)))).astype(BF16)
            du_ref[rows, :] = (dv * (gv * sg)).astype(BF16)

    cspec = pl.BlockSpec((None, bm, f), lambda i, j: (j, i, 0))
    out = jax.ShapeDtypeStruct((nc, s, f), BF16)
    return pl.pallas_call(
        body, name=name, grid=(s // bm, nc),
        in_specs=[pl.BlockSpec((bm, d), lambda i, j: (i, 0)), pl.BlockSpec((None, f, d), lambda i, j: (j, 0, 0)),
                  cspec, cspec],
        out_specs=[cspec, cspec], out_shape=[out, out], scratch_shapes=[pltpu.VMEM((bm, f), F32)],
        compiler_params=_params(dimension_semantics=("parallel", "parallel")),
    )(dx, wd, gate, up)


def _ffn_out(act, wd, res, *, name):
    nc, s, f = act.shape
    d = wd.shape[2]
    bm = _blk(s, 512)

    def body(a_ref, w_ref, r_ref, o_ref):
        acc = r_ref[...]
        for c in range(nc):
            acc = acc + jnp.dot(a_ref[c], w_ref[c], preferred_element_type=F32)
        o_ref[...] = acc

    row = pl.BlockSpec((bm, d), lambda i: (i, 0))
    return pl.pallas_call(
        body, name=name, grid=(s // bm,),
        in_specs=[pl.BlockSpec((nc, bm, f), lambda i: (0, i, 0)),
                  pl.BlockSpec((nc, f, d), lambda i: (0, 0, 0), pipeline_mode=pl.Buffered(1)), row],
        out_specs=row, out_shape=jax.ShapeDtypeStruct((s, d), F32),
        compiler_params=_params(dimension_semantics=("parallel",)),
    )(act, wd, res)


def _proj_dh(dp, w, *, name):
    s, k = dp.shape
    nc, d, kc = w.shape
    assert k == nc * kc and kc % LANES == 0
    bm = _blk(s, 512)

    def body(p_ref, w_ref, o_ref):
        acc = None
        for c in range(nc):
            term = lax.dot_general(p_ref[:, c * kc:(c + 1) * kc].astype(BF16), w_ref[c], _NT,
                                   preferred_element_type=F32)
            acc = term if acc is None else acc + term
        o_ref[...] = acc

    return pl.pallas_call(
        body, name=name, grid=(s // bm,),
        in_specs=[pl.BlockSpec((bm, k), lambda i: (i, 0)),
                  pl.BlockSpec((nc, d, kc), lambda i: (0, 0, 0), pipeline_mode=pl.Buffered(1))],
        out_specs=pl.BlockSpec((bm, d), lambda i: (i, 0)), out_shape=jax.ShapeDtypeStruct((s, d), F32),
        compiler_params=_params(dimension_semantics=("parallel",)),
    )(dp, w)


def _ffn_dh(dgate, dup, wg, wu, *, name):
    nc, s, f = dgate.shape
    d = wg.shape[1]
    bm = _blk(s, 512)

    def body(dg_ref, du_ref, wg_ref, wu_ref, o_ref):
        acc = None
        for c in range(nc):
            for x_ref, w_ref in ((dg_ref, wg_ref), (du_ref, wu_ref)):
                term = lax.dot_general(x_ref[c], w_ref[c], _NT, preferred_element_type=F32)
                acc = term if acc is None else acc + term
        o_ref[...] = acc

    xspec = pl.BlockSpec((nc, bm, f), lambda i: (0, i, 0))
    wspec = pl.BlockSpec((nc, d, f), lambda i: (0, 0, 0), pipeline_mode=pl.Buffered(1))
    return pl.pallas_call(
        body, name=name, grid=(s // bm,), in_specs=[xspec, xspec, wspec, wspec],
        out_specs=pl.BlockSpec((bm, d), lambda i: (i, 0)), out_shape=jax.ShapeDtypeStruct((s, d), F32),
        compiler_params=_params(dimension_semantics=("parallel",)),
    )(dgate, dup, wg, wu)


_NT = (((1,), (1,)), ((), ()))
_TN = (((0,), (0,)), ((), ()))


def _mem_probs(q, k, scale):
    s = lax.dot_general(q, k, _NT, preferred_element_type=F32) * scale
    e = jnp.exp(s - jnp.max(s, axis=-1, keepdims=True))
    return e / jnp.sum(e, axis=-1, keepdims=True)


def _mem_fwd(q, k, v, *, name):
    h, s, d = q.shape
    m = k.shape[1]
    bq = _blk(s, 1024)
    scale = 1.0 / math.sqrt(d)

    def body(q_ref, k_ref, v_ref, o_ref):
        p = _mem_probs(q_ref[...], k_ref[...], scale)
        o_ref[...] = jnp.dot(p.astype(BF16), v_ref[...], preferred_element_type=F32).astype(o_ref.dtype)

    qs = pl.BlockSpec((None, bq, d), lambda hh, i: (hh, i, 0))
    ks = pl.BlockSpec((None, m, d), lambda hh, i: (hh, 0, 0))
    return pl.pallas_call(
        body, name=name, grid=(h, s // bq), in_specs=[qs, ks, ks], out_specs=qs,
        out_shape=jax.ShapeDtypeStruct(q.shape, BF16),
        compiler_params=_params(dimension_semantics=("parallel", "parallel")),
    )(q, k, v)


def _mem_bwd(q, k, v, do, *, name):
    h, s, d = q.shape
    m = k.shape[1]
    bq = _blk(s, 1024)
    scale = 1.0 / math.sqrt(d)

    def body(q_ref, k_ref, v_ref, do_ref, dq_ref, dk_ref, dv_ref):
        @pl.when(pl.program_id(1) == 0)
        def _():
            dk_ref[...] = jnp.zeros_like(dk_ref)
            dv_ref[...] = jnp.zeros_like(dv_ref)

        qv = q_ref[...]
        kv = k_ref[...]
        dov = do_ref[...]
        p = _mem_probs(qv, kv, scale)
        pb = p.astype(BF16)
        dp = lax.dot_general(dov, v_ref[...], _NT, preferred_element_type=F32)
        pf = pb.astype(F32)
        ds = (pf * (dp - jnp.sum(pf * dp, axis=-1, keepdims=True)) * scale).astype(BF16)
        dq_ref[...] = jnp.dot(ds, kv, preferred_element_type=F32).astype(dq_ref.dtype)
        dk_ref[...] += lax.dot_general(ds, qv, _TN, preferred_element_type=F32)
        dv_ref[...] += lax.dot_general(pb, dov, _TN, preferred_element_type=F32)

    qs = pl.BlockSpec((None, bq, d), lambda hh, i: (hh, i, 0))
    ks = pl.BlockSpec((None, m, d), lambda hh, i: (hh, 0, 0))
    kvout = jax.ShapeDtypeStruct(k.shape, F32)
    return pl.pallas_call(
        body, name=name, grid=(h, s // bq), in_specs=[qs, ks, ks, qs], out_specs=[qs, ks, ks],
        out_shape=[jax.ShapeDtypeStruct(q.shape, BF16), kvout, kvout],
        compiler_params=_params(dimension_semantics=("parallel", "arbitrary")),
    )(q, k, v, do)


SB_TILE = 256
SB_STRIP = 32
SB_HEAD_GROUP = 4
SB_BWD_HEAD_GROUP = 4


def _sb_scale(d):
    scale = 1.0 / math.sqrt(d)
    assert math.frexp(scale)[0] == 0.5, "the scale is folded into bf16 q, exact only for a power of two"
    return scale


def _sb_strip_mask(r, t):
    rr = r + lax.broadcasted_iota(jnp.int32, (SB_STRIP, t), 0)
    return lax.broadcasted_iota(jnp.int32, (SB_STRIP, t), 1) < rr


def _neg_abs(z):
    bits = lax.bitcast_convert_type(z, jnp.uint32) | jnp.uint32(0x80000000)
    return lax.bitcast_convert_type(bits, F32)


def _store_split(split_scr, rows, val, t):
    hi = val.astype(BF16)
    split_scr[rows, 0:t] = hi
    split_scr[rows, t:2 * t] = (val - hi.astype(F32)).astype(BF16)


def _sb_logs_phase(z_scr, nsplit_scr, beta_scr, t, diag):
    for r in range(0, t, SB_STRIP):
        rows = pl.ds(r, SB_STRIP)
        z = z_scr[rows, :]
        e = jnp.exp(_neg_abs(z))
        nlog = jnp.maximum(z, 0.0) + jnp.log(1.0 + e)
        if beta_scr is not None:
            inv = pl.reciprocal(1.0 + e, approx=True)
            beta_scr[rows, :] = jnp.where(z >= 0.0, inv, e * inv)
        if diag:
            nlog = jnp.where(_sb_strip_mask(r, t), nlog, 0.0)
        _store_split(nsplit_scr, rows, nlog, t)


def _sb_probs(z_scr, tin_scr, rsum_scr, rows, r, t, diag):
    rs = rsum_scr[rows, :]
    a = jnp.exp(z_scr[rows, :] - tin_scr[rows, :] - rs)
    if diag:
        a = jnp.where(_sb_strip_mask(r, t), a, 0.0)
    rsum_scr[rows, :] = rs + tin_scr[rows, 0:1]
    return a


def _sb_triangle(tri_scr, t):
    row = lax.broadcasted_iota(jnp.int32, (t, t), 0)
    col = lax.broadcasted_iota(jnp.int32, (t, t), 1)
    tri = (row >= col).astype(BF16)
    tri_scr[0:t, :] = tri
    tri_scr[t:2 * t, :] = tri


def _sb_fwd(q, k, v, *, name):
    h, s, d = q.shape
    t = _blk(s, SB_TILE)
    scale = _sb_scale(d)

    def body(q_ref, k_ref, v_ref, o_ref, o32_ref, qs_scr, tri_scr, z_scr, nsplit_scr, tin_scr, a_scr, rsum_scr):
        qi = pl.program_id(1)
        qs_scr[...] = q_ref[...] * scale
        _sb_triangle(tri_scr, t)
        rsum_scr[...] = jnp.zeros_like(rsum_scr)
        o32_ref[...] = jnp.zeros_like(o32_ref)

        def tile(kb, diag):
            keys = pl.ds(pl.multiple_of(kb * t, t), t)
            for g in range(grp):
                z_scr[g] = lax.dot_general(qs_scr[g], k_ref[g, keys, :], _NT, preferred_element_type=F32)
            for g in range(grp):
                _sb_logs_phase(z_scr.at[g], nsplit_scr.at[g], None, t, diag)
                tin_scr[g] = jnp.dot(nsplit_scr[g], tri_scr[...], preferred_element_type=F32)
            for g in range(grp):
                for r in range(0, t, SB_STRIP):
                    rows = pl.ds(r, SB_STRIP)
                    a = _sb_probs(z_scr.at[g], tin_scr.at[g], rsum_scr.at[g], rows, r, t, diag)
                    a_scr[g, rows, :] = a.astype(BF16)
                o32_ref[g] += jnp.dot(a_scr[g], v_ref[g, keys, :], preferred_element_type=F32)

        tile(qi, True)

        def walk(n, carry):
            tile(qi - 1 - n, False)
            return carry

        lax.fori_loop(0, qi, walk, 0)
        o_ref[...] = o32_ref[...].astype(o_ref.dtype)

    grp = SB_HEAD_GROUP
    assert h % grp == 0
    qs = pl.BlockSpec((grp, t, d), lambda hh, i: (hh, i, 0))
    ks = pl.BlockSpec((grp, s, d), lambda hh, i: (hh, 0, 0))
    tile_f32 = pltpu.VMEM((grp, t, t), F32)
    col_f32 = pltpu.VMEM((grp, t, 1), F32)
    return pl.pallas_call(
        body, name=name, grid=(h // grp, s // t), in_specs=[qs, ks, ks], out_specs=[qs, qs],
        out_shape=[jax.ShapeDtypeStruct(q.shape, BF16), jax.ShapeDtypeStruct(q.shape, F32)],
        scratch_shapes=[pltpu.VMEM((grp, t, d), BF16), pltpu.VMEM((2 * t, t), BF16), tile_f32,
                        pltpu.VMEM((grp, t, 2 * t), BF16), tile_f32, pltpu.VMEM((grp, t, t), BF16), col_f32],
        compiler_params=_params(dimension_semantics=("parallel", "parallel")),
    )(q, k, v)


def _sb_bwd(q, k, v, o32, do, dk0, dv0, *, name):
    h, s, d = q.shape
    t = _blk(s, SB_TILE)
    scale = _sb_scale(d)
    has_init = dk0 is not None
    n_in = 7 if has_init else 5

    def body(*refs):
        if has_init:
            q_ref, k_ref, v_ref, o_ref, do_ref, dk0_ref, dv0_ref, dq_ref, dk_ref, dv_ref = refs[:n_in + 3]
        else:
            q_ref, k_ref, v_ref, o_ref, do_ref, dq_ref, dk_ref, dv_ref = refs[:n_in + 3]
        (qs_scr, tri_scr, z_scr, beta_scr, nsplit_scr, tin_scr, da_scr, a_scr, dz_scr,
         dq_scr, rsum_scr, gsum_scr, dsum_scr) = refs[n_in + 3:]
        g_scr, gsplit_scr, gin_scr = da_scr, nsplit_scr, tin_scr
        qi = pl.program_id(1)

        @pl.when(qi == 0)
        def _():
            if has_init:
                dk_ref[...] = dk0_ref[...]
                dv_ref[...] = dv0_ref[...]
            else:
                dk_ref[...] = jnp.zeros_like(dk_ref)
                dv_ref[...] = jnp.zeros_like(dv_ref)

        qs_scr[...] = q_ref[...] * scale
        _sb_triangle(tri_scr, t)
        dsum_scr[...] = jnp.sum(o_ref[...] * do_ref[...].astype(F32), axis=2, keepdims=True)
        rsum_scr[...] = jnp.zeros_like(rsum_scr)
        gsum_scr[...] = jnp.zeros_like(gsum_scr)
        dq_scr[...] = jnp.zeros_like(dq_scr)

        def tile(kb, diag):
            keys = pl.ds(pl.multiple_of(kb * t, t), t)
            for g in range(grp):
                z_scr[g] = lax.dot_general(qs_scr[g], k_ref[g, keys, :], _NT, preferred_element_type=F32)
                da_scr[g] = lax.dot_general(do_ref[g], v_ref[g, keys, :], _NT, preferred_element_type=F32)
            for g in range(grp):
                _sb_logs_phase(z_scr.at[g], nsplit_scr.at[g], beta_scr.at[g], t, diag)
                tin_scr[g] = jnp.dot(nsplit_scr[g], tri_scr[...], preferred_element_type=F32)
            for g in range(grp):
                for r in range(0, t, SB_STRIP):
                    rows = pl.ds(r, SB_STRIP)
                    ab = _sb_probs(z_scr.at[g], tin_scr.at[g], rsum_scr.at[g], rows, r, t, diag).astype(BF16)
                    a_scr[g, rows, :] = ab
                    gv = ab.astype(F32) * da_scr[g, rows, :]
                    g_scr[g, rows, :] = gv
                    _store_split(gsplit_scr.at[g], rows, gv, t)
                gin_scr[g] = jnp.dot(gsplit_scr[g], tri_scr[...], preferred_element_type=F32)
                dv_ref[g, keys, :] += lax.dot_general(a_scr[g], do_ref[g], _TN, preferred_element_type=F32)
            for g in range(grp):
                for r in range(0, t, SB_STRIP):
                    rows = pl.ds(r, SB_STRIP)
                    gs = gsum_scr[g, rows, :]
                    gv = g_scr[g, rows, :]
                    dz = gv - beta_scr[g, rows, :] * ((gv - gin_scr[g, rows, :]) + (dsum_scr[g, rows, :] - gs))
                    if diag:
                        dz = jnp.where(_sb_strip_mask(r, t), dz, 0.0)
                    dz_scr[g, rows, :] = dz.astype(BF16)
                    gsum_scr[g, rows, :] = gs + gin_scr[g, rows, 0:1]
                dq_scr[g] += jnp.dot(dz_scr[g], k_ref[g, keys, :], preferred_element_type=F32)
                dk_ref[g, keys, :] += lax.dot_general(dz_scr[g], qs_scr[g], _TN, preferred_element_type=F32)

        tile(qi, True)

        def walk(n, carry):
            tile(qi - 1 - n, False)
            return carry

        lax.fori_loop(0, qi, walk, 0)
        dq_ref[...] = (dq_scr[...] * scale).astype(dq_ref.dtype)

    grp = SB_BWD_HEAD_GROUP
    assert h % grp == 0
    qs = pl.BlockSpec((grp, t, d), lambda hh, i: (hh, i, 0))
    ks = pl.BlockSpec((grp, s, d), lambda hh, i: (hh, 0, 0), pipeline_mode=pl.Buffered(1))
    in_specs = [qs, ks, ks, qs, qs] + ([ks, ks] if has_init else [])
    operands = [q, k, v, o32, do] + ([dk0, dv0] if has_init else [])
    acc = jax.ShapeDtypeStruct(q.shape, F32)
    tile_f32 = pltpu.VMEM((grp, t, t), F32)
    tile_bf16 = pltpu.VMEM((grp, t, t), BF16)
    split = pltpu.VMEM((grp, t, 2 * t), BF16)
    col_f32 = pltpu.VMEM((grp, t, 1), F32)
    return pl.pallas_call(
        body, name=name, grid=(h // grp, s // t), in_specs=in_specs, out_specs=[qs, ks, ks],
        out_shape=[jax.ShapeDtypeStruct(q.shape, BF16), acc, acc],
        scratch_shapes=[pltpu.VMEM((grp, t, d), BF16), pltpu.VMEM((2 * t, t), BF16), tile_f32, tile_f32, split,
                        tile_f32, tile_f32, tile_bf16, tile_bf16,
                        pltpu.VMEM((grp, t, d), F32), col_f32, col_f32, col_f32],
        compiler_params=_params(dimension_semantics=("parallel", "arbitrary")),
    )(*operands)


def _position():
    x, y, c = lax.axis_index("x"), lax.axis_index("y"), lax.axis_index("c")
    return x, y, c, [(1 - x, y), (x, 1 - y), (1 - x, 1 - y)]


_ANY = pl.BlockSpec(memory_space=pl.ANY)
N_PEER_CHIPS = N_CHIPS - 1


def _all_gather_chips(shards, *, name):
    n = len(shards)

    def body(*refs):
        ins, outs = refs[:n], refs[n:2 * n]
        send_sems, recv_sems, local_sems = refs[2 * n:]
        x, y, c, peers = _position()
        me = 2 * x + y
        copies = []
        for a in range(n):
            copies.append(pltpu.make_async_copy(ins[a], outs[a].at[me], local_sems.at[a]))
            for j, (px, py) in enumerate(peers):
                copies.append(pltpu.make_async_remote_copy(
                    src_ref=ins[a], dst_ref=outs[a].at[me],
                    send_sem=send_sems.at[a * N_PEER_CHIPS + j], recv_sem=recv_sems.at[a * N_PEER_CHIPS + j],
                    device_id=(px, py, c), device_id_type=MESH))
        for cp in copies:
            cp.start()
        for cp in copies:
            cp.wait()

    return pl.pallas_call(
        body, name=name, in_specs=[_ANY] * n, out_specs=[_ANY] * n,
        out_shape=[jax.ShapeDtypeStruct((N_CHIPS, *s.shape), s.dtype) for s in shards],
        scratch_shapes=[pltpu.SemaphoreType.DMA((n * N_PEER_CHIPS,)), pltpu.SemaphoreType.DMA((n * N_PEER_CHIPS,)),
                        pltpu.SemaphoreType.DMA((n,))],
        compiler_params=pltpu.CompilerParams(has_side_effects=True),
    )(*shards)


_HBM = pl.BlockSpec(memory_space=pltpu.HBM)
_SEM = pl.BlockSpec(memory_space=pltpu.SEMAPHORE)
_DATAFLOW = pltpu.SideEffectType.DATAFLOW_SIDE_EFFECTING
_TOKEN = jax.ShapeDtypeStruct((8, LANES), F32)


def _in_hbm(arr):
    return pltpu.with_memory_space_constraint(arr, pltpu.HBM)


def _gather_start(packs, lands, after, *, name):
    n = len(packs)

    def body(*refs):
        src, land = refs[:n], refs[n:2 * n]
        send_sems, recv_sems = refs[2 * n + 1:2 * n + 3]
        token = refs[-1]
        x, y, c, peers = _position()
        for i in range(n):
            for j, (px, py) in enumerate(peers):
                pltpu.make_async_remote_copy(
                    src_ref=src[i], dst_ref=land[i].at[2 * x + y],
                    send_sem=send_sems.at[N_PEER_CHIPS * i + j], recv_sem=recv_sems.at[N_PEER_CHIPS * i + j],
                    device_id=(px, py, c), device_id_type=MESH).start()
        token[...] = jnp.zeros_like(token)

    thru = [pltpu.HBM(a.shape, a.dtype) for a in (*packs, *lands)]
    outs = pl.pallas_call(
        body, name=name,
        out_shape=(pltpu.SemaphoreType.DMA((N_PEER_CHIPS * n,)), pltpu.SemaphoreType.DMA((N_PEER_CHIPS * n,)), *thru,
                   _TOKEN),
        in_specs=[_HBM] * (2 * n) + [_ANY],
        out_specs=(_SEM, _SEM, *[_HBM] * (2 * n), pl.BlockSpec(memory_space=pltpu.VMEM)),
        input_output_aliases={k: 2 + k for k in range(2 * n)},
        compiler_params=pltpu.CompilerParams(has_side_effects=_DATAFLOW),
    )(*[_in_hbm(a) for a in (*packs, *lands)], after)
    return outs[0], outs[1], list(outs[2:2 + n]), list(outs[2 + n:2 + 2 * n]), outs[-1]


def _gather_wait(first, packs, lands, send_sems, recv_sems, after, *, name):
    n = len(packs)

    def body(*refs):
        src, land = refs[:n], refs[n:2 * n]
        send_sems, recv_sems = refs[2 * n:2 * n + 2]
        _, _, c, peers = _position()
        for a in range(n):
            for j, (px, py) in enumerate(peers):
                sem = N_PEER_CHIPS * (first + a) + j
                cp = pltpu.make_async_remote_copy(
                    src_ref=src[a], dst_ref=land[a].at[2 * px + py], send_sem=send_sems.at[sem],
                    recv_sem=recv_sems.at[sem], device_id=(px, py, c), device_id_type=MESH)
                cp.wait_send()
                cp.wait_recv()

    outs = pl.pallas_call(
        body, name=name, out_shape=[pltpu.HBM(a.shape, a.dtype) for a in (*packs, *lands)],
        in_specs=[_HBM] * (2 * n) + [_SEM, _SEM, _ANY], out_specs=[_HBM] * (2 * n),
        input_output_aliases={k: k for k in range(2 * n)},
        compiler_params=pltpu.CompilerParams(has_side_effects=_DATAFLOW),
    )(*packs, *lands, send_sems, recv_sems, after)
    return list(outs[:n]), list(outs[n:])


def _scatter_start(gpacks, lands, *, name):
    n = len(gpacks)

    def body(*refs):
        src, land = refs[:n], refs[n:2 * n]
        send_sems, recv_sems = refs[2 * n:2 * n + 2]
        token = refs[-1]
        _, _, c, peers = _position()
        for a in range(n):
            for j, (px, py) in enumerate(peers):
                pltpu.make_async_remote_copy(
                    src_ref=src[a].at[2 * px + py], dst_ref=land[a].at[j], send_sem=send_sems.at[N_PEER_CHIPS * a + j],
                    recv_sem=recv_sems.at[N_PEER_CHIPS * a + j], device_id=(px, py, c), device_id_type=MESH).start()
        token[...] = jnp.zeros_like(token)

    thru = [pltpu.HBM(a.shape, a.dtype) for a in (*gpacks, *lands)]
    outs = pl.pallas_call(
        body, name=name,
        out_shape=(pltpu.SemaphoreType.DMA((N_PEER_CHIPS * n,)), pltpu.SemaphoreType.DMA((N_PEER_CHIPS * n,)), *thru,
                   _TOKEN),
        in_specs=[_HBM] * (2 * n), out_specs=(_SEM, _SEM, *[_HBM] * (2 * n), pl.BlockSpec(memory_space=pltpu.VMEM)),
        input_output_aliases={k: 2 + k for k in range(2 * n)},
        compiler_params=pltpu.CompilerParams(has_side_effects=_DATAFLOW),
    )(*[_in_hbm(a) for a in (*gpacks, *lands)])
    return outs[0], outs[1], list(outs[2:2 + n]), list(outs[2 + n:2 + 2 * n]), outs[-1]


def _scatter_wait(gpacks, lands, send_sems, recv_sems, after, *, name):
    n = len(gpacks)

    def body(*refs):
        src, land = refs[:n], refs[n:2 * n]
        send_sems, recv_sems = refs[2 * n:2 * n + 2]
        _, _, c, peers = _position()
        for a in range(n):
            for j, (px, py) in enumerate(peers):
                cp = pltpu.make_async_remote_copy(
                    src_ref=src[a].at[2 * px + py], dst_ref=land[a].at[j], send_sem=send_sems.at[N_PEER_CHIPS * a + j],
                    recv_sem=recv_sems.at[N_PEER_CHIPS * a + j], device_id=(px, py, c),
                    device_id_type=MESH)
                cp.wait_send()
                cp.wait_recv()

    outs = pl.pallas_call(
        body, name=name, out_shape=[pltpu.HBM(a.shape, a.dtype) for a in (*gpacks, *lands)],
        in_specs=[_HBM] * (2 * n) + [_SEM, _SEM, _ANY], out_specs=[_HBM] * (2 * n),
        input_output_aliases={k: k for k in range(2 * n)},
        compiler_params=pltpu.CompilerParams(has_side_effects=_DATAFLOW),
    )(*gpacks, *lands, send_sems, recv_sems, after)
    return list(outs[:n]), list(outs[n:])


def _sibling_exchange(arrs, *, name):
    n = len(arrs)

    def body(*refs):
        ins, outs = refs[:n], refs[n:2 * n]
        send_sems, recv_sems = refs[2 * n:]
        x, y, c, _ = _position()
        copies = [pltpu.make_async_remote_copy(
            src_ref=ins[a], dst_ref=outs[a], send_sem=send_sems.at[a], recv_sem=recv_sems.at[a],
            device_id=(x, y, 1 - c), device_id_type=MESH) for a in range(n)]
        for cp in copies:
            cp.start()
        for cp in copies:
            cp.wait()

    return pl.pallas_call(
        body, name=name, in_specs=[_ANY] * n, out_specs=[_ANY] * n,
        out_shape=[jax.ShapeDtypeStruct(a.shape, a.dtype) for a in arrs],
        scratch_shapes=[pltpu.SemaphoreType.DMA((n,)), pltpu.SemaphoreType.DMA((n,))],
        compiler_params=pltpu.CompilerParams(has_side_effects=True),
    )(*arrs)


def _small_all_reduce(v, *, name):
    r, cdim = v.shape

    def body(v_ref, o_ref, slots, send_sems, recv_sems):
        x, y, c, _ = _position()
        me = 4 * x + 2 * y + c
        slots[me] = v_ref[...]
        copies = []
        for j in range(1, N_DEVICES):
            peer = (x ^ ((j >> 2) & 1), y ^ ((j >> 1) & 1), c ^ (j & 1))
            copies.append(pltpu.make_async_remote_copy(
                src_ref=v_ref, dst_ref=slots.at[me], send_sem=send_sems.at[j - 1], recv_sem=recv_sems.at[j - 1],
                device_id=peer, device_id_type=MESH))
        for cp in copies:
            cp.start()
        for cp in copies:
            cp.wait()
        acc = slots[0]
        for dev in range(1, N_DEVICES):
            acc = acc + slots[dev]
        o_ref[...] = acc

    vm = pl.BlockSpec(memory_space=pltpu.VMEM)
    return pl.pallas_call(
        body, name=name, in_specs=[vm], out_specs=vm,
        out_shape=jax.ShapeDtypeStruct(v.shape, F32),
        scratch_shapes=[pltpu.VMEM((N_DEVICES, r, cdim), F32),
                        pltpu.SemaphoreType.DMA((N_DEVICES - 1,)), pltpu.SemaphoreType.DMA((N_DEVICES - 1,))],
        compiler_params=pltpu.CompilerParams(has_side_effects=True),
    )(v)


def _row_block(r, cap, mult):
    best = max(b for b in range(mult, cap + 1, mult) if r % b == 0)
    return best


def _sum_partials(chip_idx, own4, landed, *, name):
    _, r, c = landed.shape
    bm = _row_block(r, 512, 16)

    def body(chip_ref, own_ref, land_ref, o_ref):
        acc = own_ref[...].astype(F32)
        for j in range(N_PEER_CHIPS):
            acc = acc + land_ref[j].astype(F32)
        o_ref[...] = acc

    return pl.pallas_call(
        body, name=name,
        grid_spec=pltpu.PrefetchScalarGridSpec(
            num_scalar_prefetch=1, grid=(r // bm,),
            in_specs=[pl.BlockSpec((None, bm, c), lambda i, chip: (chip[0], i, 0)),
                      pl.BlockSpec((N_PEER_CHIPS, bm, c), lambda i, chip: (0, i, 0))],
            out_specs=pl.BlockSpec((bm, c), lambda i, chip: (i, 0))),
        out_shape=jax.ShapeDtypeStruct((r, c), F32),
        compiler_params=_params(dimension_semantics=("parallel",)),
    )(chip_idx, own4, landed)


def _adamw(w, m, v, g_a, g_b, *, name):
    r, c = w.shape
    bm = _row_block(r, 512, 8)
    two = g_b is not None
    bc1 = 1.0 - ADAM_B1 ** ADAM_STEP
    bc2 = 1.0 - ADAM_B2 ** ADAM_STEP

    def body(*refs):
        if two:
            w_ref, m_ref, v_ref, ga_ref, gb_ref, g_ref, d_ref, nm_ref, nv_ref = refs
            g = ga_ref[...] + gb_ref[...]
        else:
            w_ref, m_ref, v_ref, ga_ref, g_ref, d_ref, nm_ref, nv_ref = refs
            g = ga_ref[...]
        nm = ADAM_B1 * m_ref[...] + (1.0 - ADAM_B1) * g
        nv = ADAM_B2 * v_ref[...] + (1.0 - ADAM_B2) * (g * g)
        g_ref[...] = g
        nm_ref[...] = nm
        nv_ref[...] = nv
        d_ref[...] = -ADAM_LR * ((nm / bc1) / (jnp.sqrt(nv / bc2) + ADAM_EPS) + ADAM_WD * w_ref[...])

    spec = pl.BlockSpec((bm, c), lambda i: (i, 0))
    out = jax.ShapeDtypeStruct((r, c), F32)
    operands = [w, m, v, g_a] + ([g_b] if two else [])
    return pl.pallas_call(
        body, name=name, grid=(r // bm,),
        in_specs=[spec] * len(operands), out_specs=[spec] * 4, out_shape=[out] * 4,
        compiler_params=_params(dimension_semantics=("parallel",)),
    )(*operands)


def _heads(a):
    s, w = a.shape
    return a.reshape(s, w // HEAD_DIM, HEAD_DIM).transpose(1, 0, 2)


def _unheads(a):
    h, s, d = a.shape
    return a.transpose(1, 0, 2).reshape(s, h * d)


def _width_groups(shapes):
    groups = {}
    for idx, (_, c) in enumerate(shapes):
        groups.setdefault(c, []).append(idx)
    return list(groups.values())


def _pack(arrs, lead):
    groups = _width_groups([a.shape[-2:] for a in arrs])
    return [jnp.concatenate([arrs[k] for k in grp], axis=lead) for grp in groups]


def _unpack(bufs, shapes, lead):
    outs = [None] * len(shapes)
    for buf, grp in zip(bufs, _width_groups(shapes)):
        off = 0
        for k in grp:
            outs[k] = lax.slice_in_dim(buf, off, off + shapes[k][0], axis=lead)
            off += shapes[k][0]
    return outs


class LayerWeights:
    FIELDS = ("w_in", "wm", "wo", "wg", "wu", "wd", "wkv")
    STAGE = (0, 0, 1, 2, 2, 2, 2)

    def __init__(self):
        for f in self.FIELDS:
            setattr(self, f, None)


def _pack_small(mix, ffn, kvn, memn, fin, conv):
    d = mix.shape[-1]
    flat = conv.reshape(-1)
    rows_conv = SMALL_ROWS - 11
    flat = jnp.pad(flat, (0, rows_conv * d - flat.shape[0]))
    return jnp.concatenate([mix, ffn, kvn.reshape(1, d), memn.reshape(1, d), fin.reshape(1, d),
                            flat.reshape(rows_conv, d)], axis=0)


def _unpack_small(buf, conv_shape):
    n = math.prod(conv_shape)
    return (buf[0:4], buf[4:8], buf[8], buf[9], buf[10], buf[11:].reshape(-1)[:n].reshape(conv_shape))


def kernel(x, mem, mix_norm, a_in, conv_w, b_in, kv_norm, w_kv_shared, w_mem_kv, w_o, ffn_norm, w_gate, w_up, w_down, mem_norm, final_norm, loss_target, m_mix_norm, m_a_in, m_conv_w, m_b_in, m_kv_norm, m_w_kv_shared, m_w_mem_kv, m_w_o, m_ffn_norm, m_w_gate, m_w_up, m_w_down, m_mem_norm, m_final_norm, v_mix_norm, v_a_in, v_conv_w, v_b_in, v_kv_norm, v_w_kv_shared, v_w_mem_kv, v_w_o, v_ffn_norm, v_w_gate, v_w_up, v_w_down, v_mem_norm, v_final_norm):
    s, d = x.shape[1], x.shape[2]
    n_mem = mem.shape[1]
    depth = mix_norm.shape[0]
    n_a = a_in.shape[0]
    main_w = conv_w.shape[2] * N_CHIPS
    mem_w = w_mem_kv.shape[2] // 2
    ffn_c = w_gate.shape[2]
    kv_c = w_kv_shared.shape[1]
    a_c = a_in.shape[2]
    chip = 2 * lax.axis_index("x") + lax.axis_index("y")

    x0 = x[0]
    mem0 = mem[0]
    tgt = loss_target[0]
    bs = _blk(s, 1024)
    bg = _blk(s, 4096)

    def layer_shards(i, a_or_b, others, kv):
        ws = [a_or_b[0][i] if i < n_a else a_or_b[1][i - n_a]] + [w[i] for w in others]
        return ws + ([kv] if i == n_a - 1 else [])

    shards = [layer_shards(i, (a_in, b_in), (w_mem_kv, w_o, w_gate, w_up, w_down), w_kv_shared) for i in range(depth)]
    shapes = [[w.shape for w in ws] for ws in shards]
    layer_packs = [_pack([w.astype(BF16) for w in ws], 0) for ws in shards]
    first = [sum(len(p) for p in layer_packs[:i]) for i in range(depth + 1)]
    packs = [p for ps in layer_packs for p in ps]
    conv_parts, = _all_gather_chips([conv_w], name="gather_conv_weights")
    conv_full = jnp.concatenate([conv_parts[kk] for kk in range(N_CHIPS)], axis=-1)
    lands = [lax.empty((N_CHIPS, *p.shape), BF16) for p in packs]
    cut = first[1]
    sems0 = _gather_start(packs[:cut], lands[:cut], conv_parts, name="gather_weights_start_l0")
    sems1 = _gather_start(packs[cut:], lands[cut:], sems0[4], name="gather_weights_start_rest")
    packs, lands, started = sems0[2] + sems1[2], sems0[3] + sems1[3], sems1[4]

    def gain(vec):
        return vec.reshape(1, d)

    lw = [LayerWeights() for _ in range(depth)]

    def fetch(i, stage, after):
        groups = _width_groups(shapes[i])
        sel = [gi for gi, grp in enumerate(groups) if min(LayerWeights.STAGE[k] for k in grp) == stage]
        if not sel:
            return
        lo, hi = first[i] + sel[0], first[i] + sel[-1] + 1
        assert hi - lo == len(sel)
        sems, base = (sems0, 0) if i == 0 else (sems1, cut)
        own, landed = _gather_wait(lo - base, packs[lo:hi], lands[lo:hi], sems[0], sems[1], after,
                                   name=f"gather_weights_wait_l{i}_s{stage}")
        for gi, mine, buf in zip(sel, own, landed):
            buf = lax.dynamic_update_slice_in_dim(buf, mine[None], chip, axis=0)
            off = 0
            for k in groups[gi]:
                rows = shapes[i][k][0]
                setattr(lw[i], LayerWeights.FIELDS[k], lax.slice_in_dim(buf, off, off + rows, axis=1))
                off += rows

    mem_n = _rms_fwd(mem0, gain(mem_norm), name="mem_norm_fwd", dep=started)
    saved = []
    k_sh = v_sh = hk = x_kv = None
    xc = x0
    for i in range(depth):
        st = {"x_in": xc}
        h = _rms_fwd(xc, gain(mix_norm[i]), name="mix_norm_fwd")
        fetch(i, 0, h)
        mkv = _mm(Op(mem_n), Op(lw[i].wm, 'r'), name="mem_kv_proj", bm=n_mem, bn=2 * mem_w, bk=d,
                  out_dtype=BF16)
        mem_k, mem_v = _heads(mkv[:, :mem_w]), _heads(mkv[:, mem_w:])
        if i < n_a:
            p = _mm(Op(h), Op(lw[i].w_in, 'c'), name="a_in_proj", bm=bs, bn=a_c, bk=d, out_dtype=BF16)
            y_main = _conv_fwd(p, conv_full[i], name="conv_fwd")
            q_mem = _heads(p[:, 3 * main_w:])
        else:
            p = _mm(Op(h), Op(lw[i].w_in, 'r'), name="b_in_proj", bm=bs, bn=d, bk=d, out_dtype=BF16)
            q_sb = _heads(p[:, :main_w])
            o_sb, o_sb32 = _sb_fwd(q_sb, k_sh, v_sh, name="sb_fwd")
            y_main = _unheads(o_sb)
            q_mem = _heads(p[:, main_w:])
            st.update(q_sb=q_sb, o_sb32=o_sb32)
        y_mem = _mem_fwd(q_mem, mem_k, mem_v, name="mem_attn_fwd")
        y = jnp.concatenate([y_main, _unheads(y_mem)], axis=-1)
        fetch(i, 1, y)
        x_mid = _mm(Op(y), Op(lw[i].wo, 'r'), name="w_o_proj", bm=bs, bn=d, bk=d, out_dtype=F32,
                    res=Op(xc))
        h2 = _rms_fwd(x_mid, gain(ffn_norm[i]), name="ffn_norm_fwd")
        fetch(i, 2, h2)
        gate, up, act = _ffn_in(h2, lw[i].wg, lw[i].wu, name="ffn_gate_up")
        xc = _ffn_out(act, lw[i].wd, x_mid, name="ffn_down")
        st.update(h=h, p=p, mem_k=mem_k, mem_v=mem_v, q_mem=q_mem, y=y, x_mid=x_mid, h2=h2, gate=gate, up=up,
                  act=act)
        saved.append(st)
        if i == n_a - 1:
            x_kv = xc
            hk = _rms_fwd(xc, gain(kv_norm), name="kv_norm_fwd")
            kv = _mm(Op(hk), Op(lw[n_a - 1].wkv, 'c'), name="kv_proj", bm=bs, bn=kv_c, bk=d, out_dtype=BF16)
            k_sh, v_sh = _heads(kv[:, :main_w]), _heads(kv[:, main_w:])

    dx, dg_final, loss_part = _final_loss(xc, gain(final_norm), tgt, name="final_norm_loss")
    loss = lax.psum(loss_part[0, 0], ("x", "y", "c"))

    g_a, g_b, g_m, g_o, g_g, g_u, g_d = ([None] * n_a, [None] * (depth - n_a), [None] * depth, [None] * depth,
                                         [None] * depth, [None] * depth, [None] * depth)
    dg_mix, dg_ffn, dconv = [None] * depth, [None] * depth, [None] * n_a
    in_flight = [[None, None] for _ in range(depth)]
    HALVES = ((0, 1, 2, 6), (3, 4, 5))

    def scatter(grads, name):
        gpacks = _pack(grads, 1)
        return _scatter_start(gpacks, [lax.empty((N_PEER_CHIPS, *g.shape[1:]), BF16) for g in gpacks], name=name)

    dk_sh = dv_sh = None
    dmem_n = None
    g_kv = dg_kv = None
    for i in reversed(range(depth)):
        st = saved[i]
        dgate, dup = _ffn_dact(dx, lw[i].wd, st["gate"], st["up"], name="ffn_dact_gate")
        g_d[i] = _mm(Op(st["act"], 'c'), Op(dx), name="w_down_grad", ta=True, bm=ffn_c, bn=d, bk=bg,
                     out_dtype=BF16, out_chunk='r')
        g_g[i] = _mm(Op(st["h2"]), Op(dgate, 'c'), name="w_gate_grad", ta=True, bm=d, bn=ffn_c, bk=bg,
                     out_dtype=BF16, out_chunk='c')
        g_u[i] = _mm(Op(st["h2"]), Op(dup, 'c'), name="w_up_grad", ta=True, bm=d, bn=ffn_c, bk=bg,
                     out_dtype=BF16, out_chunk='c')
        in_flight[i][1] = scatter([g_g[i], g_u[i], g_d[i]], f"scatter_ffn_grads_start_l{i}")
        dh2 = _ffn_dh(dgate, dup, lw[i].wg, lw[i].wu, name="ffn_dh")
        dx_mid, dg_ffn[i] = _rms_bwd(st["x_mid"], gain(ffn_norm[i]), dh2, dx, name="ffn_norm_bwd",
                                     dep=in_flight[i][1][4])
        dy = _mm(Op(dx_mid), Op(lw[i].wo, 'r'), name="w_o_dy", tb=True, bm=bs, bn=d, bk=d,
                 out_dtype=BF16)
        g_o[i] = _mm(Op(st["y"]), Op(dx_mid), name="w_o_grad", ta=True, bm=d // N_CHIPS, bn=d, bk=bg,
                     out_dtype=BF16, out_chunk='r')
        dq_mem, dmk, dmv = _mem_bwd(st["q_mem"], st["mem_k"], st["mem_v"], _heads(dy[:, main_w:]),
                                    name="mem_attn_bwd")
        dmkv = jnp.concatenate([_unheads(dmk), _unheads(dmv)], axis=-1)
        g_m[i] = _mm(Op(mem_n), Op(dmkv), name="mem_kv_grad", ta=True, bm=d // N_CHIPS, bn=2 * mem_w, bk=n_mem,
                     out_dtype=BF16, out_chunk='r')
        dmem_n = _mm(Op(dmkv), Op(lw[i].wm, 'r'), name="mem_kv_dmem", tb=True, bm=n_mem, bn=d,
                     bk=2 * mem_w, out_dtype=F32, res=None if dmem_n is None else Op(dmem_n))
        if i < n_a:
            db, dc, du, dconv[i] = _conv_bwd(st["p"], conv_full[i], dy[:, :main_w], name="conv_bwd")
            dp = jnp.concatenate([db, dc, du, _unheads(dq_mem)], axis=-1)
            g_a[i] = _mm(Op(st["h"]), Op(dp), name="a_in_grad", ta=True, bm=d, bn=a_c, bk=bg, out_dtype=BF16,
                         out_chunk='c')
            dh = _proj_dh(dp, lw[i].w_in, name="a_in_dh")
        else:
            dq_sb, dk_sh, dv_sh = _sb_bwd(st["q_sb"], k_sh, v_sh, st["o_sb32"], _heads(dy[:, :main_w]),
                                          dk_sh, dv_sh, name="sb_bwd")
            dp = jnp.concatenate([_unheads(dq_sb), _unheads(dq_mem)], axis=-1)
            g_b[i - n_a] = _mm(Op(st["h"]), Op(dp), name="b_in_grad", ta=True, bm=d // N_CHIPS, bn=d, bk=bg,
                               out_dtype=BF16, out_chunk='r')
            dh = _mm(Op(dp), Op(lw[i].w_in, 'r'), name="b_in_dh", tb=True, bm=bs, bn=d, bk=d,
                     out_dtype=F32)
        in_flight[i][0] = scatter([g_a[i] if i < n_a else g_b[i - n_a], g_m[i], g_o[i]]
                                  + ([g_kv] if i == n_a - 1 else []), f"scatter_mixer_grads_start_l{i}")
        dx, dg_mix[i] = _rms_bwd(st["x_in"], gain(mix_norm[i]), dh, dx_mid, name="mix_norm_bwd",
                                 dep=in_flight[i][0][4])
        if i == n_a:
            dkv = jnp.concatenate([_unheads(dk_sh), _unheads(dv_sh)], axis=-1)
            g_kv = _mm(Op(hk), Op(dkv), name="kv_grad", ta=True, bm=d, bn=kv_c, bk=bg, out_dtype=BF16,
                       out_chunk='c')
            dhk = _proj_dh(dkv, lw[n_a - 1].wkv, name="kv_dh")
            dx, dg_kv = _rms_bwd(x_kv, gain(kv_norm), dhk, dx, name="kv_norm_bwd")
    _, dg_mem = _rms_bwd(mem0, gain(mem_norm), dmem_n, None, name="mem_norm_bwd")

    chip_idx = chip.astype(jnp.int32).reshape(1)
    core_sums, spans = [], {}
    for i in reversed(range(depth)):
        for half in (1, 0):
            ssem, rsem, gthru, lthru, _ = in_flight[i][half]
            gthru, lthru = _scatter_wait(gthru, lthru, ssem, rsem, dx, name=f"scatter_grads_wait_l{i}_h{half}")
            spans[i, half] = (len(core_sums), len(core_sums) + len(gthru))
            core_sums += [_sum_partials(chip_idx, g, l, name="sum_chip_partials") for g, l in zip(gthru, lthru)]
    sibling_sums = _sibling_exchange(core_sums, name="exchange_core_sums")

    def layer_parts(sums, i):
        parts = [None] * len(shapes[i])
        for half in (0, 1):
            pos = [k for k in HALVES[half] if k < len(shapes[i])]
            lo, hi = spans[i, half]
            for k, part in zip(pos, _unpack(sums[lo:hi], [shapes[i][k] for k in pos], 0)):
                parts[k] = part
        return parts

    own_parts = [layer_parts(core_sums, i) for i in range(depth)]
    sib_parts = [layer_parts(sibling_sums, i) for i in range(depth)]

    def stacked(parts, pos, layers):
        return jnp.concatenate([parts[i][pos] for i in layers], axis=0)

    a_layers, b_layers, all_layers = range(n_a), range(n_a, depth), range(depth)
    big = [("a_in", a_in, m_a_in, v_a_in, 0, a_layers), ("b_in", b_in, m_b_in, v_b_in, 0, b_layers),
           ("w_kv_shared", w_kv_shared, m_w_kv_shared, v_w_kv_shared, 6, [n_a - 1]),
           ("w_mem_kv", w_mem_kv, m_w_mem_kv, v_w_mem_kv, 1, all_layers), ("w_o", w_o, m_w_o, v_w_o, 2, all_layers),
           ("w_gate", w_gate, m_w_gate, v_w_gate, 3, all_layers), ("w_up", w_up, m_w_up, v_w_up, 4, all_layers),
           ("w_down", w_down, m_w_down, v_w_down, 5, all_layers)]
    results = {}
    for wname, w, mm_, vv_, pos, layers in big:
        flat = lambda t: t.reshape(-1, t.shape[-1])
        outs = _adamw(flat(w), flat(mm_), flat(vv_), stacked(own_parts, pos, layers), stacked(sib_parts, pos, layers),
                      name="adamw")
        results[wname] = [o.reshape(w.shape) for o in outs]

    small_g = _pack_small(jnp.concatenate(dg_mix, axis=0), jnp.concatenate(dg_ffn, axis=0), dg_kv, dg_mem,
                          dg_final, jnp.stack(dconv, axis=0))
    small_g = _small_all_reduce(small_g, name="all_reduce_small_grads")
    conv_shape_full = (n_a, CONV_TAPS, main_w)
    gs = list(_unpack_small(small_g, conv_shape_full))
    gs[5] = lax.dynamic_slice_in_dim(gs[5], chip * conv_w.shape[2], conv_w.shape[2], axis=2)
    small_outs = _adamw(_pack_small(mix_norm, ffn_norm, kv_norm, mem_norm, final_norm, conv_w),
                        _pack_small(m_mix_norm, m_ffn_norm, m_kv_norm, m_mem_norm, m_final_norm, m_conv_w),
                        _pack_small(v_mix_norm, v_ffn_norm, v_kv_norm, v_mem_norm, v_final_norm, v_conv_w),
                        _pack_small(*gs), None, name="adamw_small")
    small_names = ["mix_norm", "ffn_norm", "kv_norm", "mem_norm", "final_norm", "conv_w"]
    for kind, buf in enumerate(small_outs):
        for wname, val in zip(small_names, _unpack_small(buf, conv_w.shape)):
            results.setdefault(wname, [None] * 4)[kind] = val

    order = ["mix_norm", "a_in", "conv_w", "b_in", "kv_norm", "w_kv_shared", "w_mem_kv", "w_o", "ffn_norm",
             "w_gate", "w_up", "w_down", "mem_norm", "final_norm"]
    return (loss, dx[None], *[results[nm][0] for nm in order], *[results[nm][1] for nm in order],
            *[results[nm][2] for nm in order], *[results[nm][3] for nm in order])
```

```python
import math
from typing import NamedTuple, Optional

import jax
import jax.numpy as jnp
from jax import lax
from jax.experimental import pallas as pl
from jax.experimental.pallas import tpu as pltpu

F32 = jnp.float32
BF16 = jnp.bfloat16
MESH = pl.DeviceIdType.MESH

N_CHIPS = 4
N_DEVICES = 8
HEAD_DIM = 64
CONV_TAPS = 3
NORM_EPS = 1e-6
V7X_VMEM_BYTES = 64 * 1024 * 1024
VMEM_LIMIT = V7X_VMEM_BYTES - 8 * 1024 * 1024
LANES = 128
SMALL_ROWS = 16

ADAM_LR = 0.001
ADAM_B1 = 0.9
ADAM_B2 = 0.999
ADAM_EPS = 1e-08
ADAM_WD = 0.01
ADAM_STEP = 10


def _params(**kw):
    return pltpu.CompilerParams(vmem_limit_bytes=VMEM_LIMIT, **kw)


def _blk(n, pref):
    b = min(n, pref)
    assert n % b == 0, (n, pref)
    return b


class Op(NamedTuple):
    arr: jax.Array
    chunk: Optional[str] = None
    layer: Optional[int] = None


def _op_spec(op_chunk, op_layer, shape2, br, bc, pick):
    r, c = shape2
    lead = () if op_layer is None else (op_layer,)
    none = (None,) * len(lead)
    if op_chunk is None:
        def imap(i, j, k):
            rb, cb = pick(i, j, k)
            return (*lead, rb, cb)
        return pl.BlockSpec((*none, br, bc), imap)
    if op_chunk == 'r' and br == N_CHIPS * r:
        def imap(i, j, k):
            rb, cb = pick(i, j, k)
            return (0, *lead, 0, cb)
        return pl.BlockSpec((N_CHIPS, *none, r, bc), imap)
    if op_chunk == 'r':
        n = r // br
        assert r % br == 0

        def imap(i, j, k):
            rb, cb = pick(i, j, k)
            return (rb // n, *lead, rb % n, cb)
        return pl.BlockSpec((None, *none, br, bc), imap)
    n = c // bc
    assert c % bc == 0

    def imap(i, j, k):
        rb, cb = pick(i, j, k)
        return (cb // n, *lead, rb, cb % n)
    return pl.BlockSpec((None, *none, br, bc), imap)


def _mm(a, b, *, name, ta=False, tb=False, bm, bn, bk, out_dtype, out_chunk=None, res=None):
    def dims(op):
        r, c = op.arr.shape[-2:]
        return (r * N_CHIPS if op.chunk == 'r' else r, c * N_CHIPS if op.chunk == 'c' else c)

    ar, ac = dims(a)
    br_, bc_ = dims(b)
    m, ka = (ac, ar) if ta else (ar, ac)
    kb, n = (bc_, br_) if tb else (br_, bc_)
    assert ka == kb, (name, ka, kb)
    assert m % bm == 0 and n % bn == 0 and ka % bk == 0, (name, m, n, ka, bm, bn, bk)
    nk = ka // bk

    if ta:
        a_spec = _op_spec(a.chunk, a.layer, a.arr.shape[-2:], bk, bm, lambda i, j, k: (k, i))
    else:
        a_spec = _op_spec(a.chunk, a.layer, a.arr.shape[-2:], bm, bk, lambda i, j, k: (i, k))
    if tb:
        b_spec = _op_spec(b.chunk, b.layer, b.arr.shape[-2:], bn, bk, lambda i, j, k: (j, k))
    else:
        b_spec = _op_spec(b.chunk, b.layer, b.arr.shape[-2:], bk, bn, lambda i, j, k: (k, j))

    if out_chunk == 'r':
        out_shape2 = (m // N_CHIPS, n)
    elif out_chunk == 'c':
        out_shape2 = (m, n // N_CHIPS)
    else:
        out_shape2 = (m, n)
    o_spec = _op_spec(out_chunk, None, out_shape2, bm, bn, lambda i, j, k: (i, j))
    out_full = out_shape2 if out_chunk is None else (N_CHIPS, *out_shape2)

    contract = (((0 if ta else 1,), (1 if tb else 0,)), ((), ()))
    has_res = res is not None

    def block2(ref):
        v = ref[...]
        return v.reshape(-1, v.shape[-1]).astype(BF16)

    def body(*refs):
        r_ref = refs[2] if has_res else None
        a_ref, b_ref = refs[:2]
        o_ref = refs[3 if has_res else 2]
        prod = lax.dot_general(block2(a_ref), block2(b_ref), contract, preferred_element_type=F32)
        if nk == 1:
            if has_res:
                prod = prod + r_ref[...].astype(F32)
            o_ref[...] = prod.astype(o_ref.dtype)
            return
        acc_ref = refs[-1]
        k = pl.program_id(2)

        @pl.when(k == 0)
        def _():
            acc_ref[...] = prod

        @pl.when(k > 0)
        def _():
            acc_ref[...] += prod

        @pl.when(k == nk - 1)
        def _():
            acc = acc_ref[...]
            if has_res:
                acc = acc + r_ref[...].astype(F32)
            o_ref[...] = acc.astype(o_ref.dtype)

    in_specs = [a_spec, b_spec]
    operands = [a.arr, b.arr]
    if has_res:
        in_specs.append(_op_spec(res.chunk, res.layer, res.arr.shape[-2:], bm, bn, lambda i, j, k: (i, j)))
        operands.append(res.arr)
    return pl.pallas_call(
        body, name=name, grid=(m // bm, n // bn, nk),
        in_specs=in_specs, out_specs=o_spec,
        out_shape=jax.ShapeDtypeStruct(out_full, out_dtype),
        scratch_shapes=[pltpu.VMEM((bm, bn), F32)] if nk > 1 else [],
        compiler_params=_params(dimension_semantics=("parallel", "parallel", "arbitrary")),
    )(*operands)


def _rms_fwd(x, g, *, name, dep=None):
    r, d = x.shape
    bm = _blk(r, 512)

    def body(x_ref, g_ref, *rest):
        o_ref = rest[-1]
        xv = x_ref[...]
        rstd = lax.rsqrt(jnp.mean(xv * xv, axis=-1, keepdims=True) + NORM_EPS)
        o_ref[...] = ((xv * rstd) * g_ref[...]).astype(o_ref.dtype)

    deps = [] if dep is None else [dep]
    return pl.pallas_call(
        body, name=name, grid=(r // bm,),
        in_specs=[pl.BlockSpec((bm, d), lambda i: (i, 0)), pl.BlockSpec((1, d), lambda i: (0, 0))]
        + [pl.BlockSpec(memory_space=pl.ANY)] * len(deps),
        out_specs=pl.BlockSpec((bm, d), lambda i: (i, 0)),
        out_shape=jax.ShapeDtypeStruct((r, d), BF16),
        compiler_params=_params(dimension_semantics=("parallel",)),
    )(x, g, *deps)


def _rms_bwd(x, g, dh, dres, *, name, dep=None):
    r, d = x.shape
    bm = _blk(r, 512)
    has_res = dres is not None
    deps = [] if dep is None else [dep]

    def body(*refs):
        if has_res:
            x_ref, g_ref, dh_ref, dres_ref = refs[:4]
        else:
            x_ref, g_ref, dh_ref = refs[:3]
        dx_ref, dxb_ref, dg_ref = refs[-3:]

        @pl.when(pl.program_id(0) == 0)
        def _():
            dg_ref[...] = jnp.zeros_like(dg_ref)

        xv = x_ref[...]
        rstd = lax.rsqrt(jnp.mean(xv * xv, axis=-1, keepdims=True) + NORM_EPS)
        xh = xv * rstd
        dhv = dh_ref[...].astype(F32)
        dg_ref[...] += jnp.sum(dhv * xh, axis=0, keepdims=True)
        dxh = dhv * g_ref[...]
        dx = rstd * (dxh - xh * jnp.mean(dxh * xh, axis=-1, keepdims=True))
        if has_res:
            dx = dres_ref[...] + dx
        dx_ref[...] = dx
        dxb_ref[...] = dx.astype(BF16)

    row = pl.BlockSpec((bm, d), lambda i: (i, 0))
    vec = pl.BlockSpec((1, d), lambda i: (0, 0))
    in_specs = [row, vec, row] + ([row] if has_res else []) + [pl.BlockSpec(memory_space=pl.ANY)] * len(deps)
    operands = [x, g, dh] + ([dres] if has_res else []) + deps
    return pl.pallas_call(
        body, name=name, grid=(r // bm,),
        in_specs=in_specs, out_specs=[row, row, vec],
        out_shape=[jax.ShapeDtypeStruct((r, d), F32), jax.ShapeDtypeStruct((r, d), BF16),
                   jax.ShapeDtypeStruct((1, d), F32)],
        compiler_params=_params(dimension_semantics=("arbitrary",)),
    )(*operands)


def _final_loss(x, g, tgt, *, name):
    r, d = x.shape
    bm = _blk(r, 512)

    def body(x_ref, g_ref, t_ref, dx_ref, dxb_ref, dg_ref, loss_ref):
        @pl.when(pl.program_id(0) == 0)
        def _():
            dg_ref[...] = jnp.zeros_like(dg_ref)
            loss_ref[...] = jnp.zeros_like(loss_ref)

        xv = x_ref[...]
        gv = g_ref[...]
        rstd = lax.rsqrt(jnp.mean(xv * xv, axis=-1, keepdims=True) + NORM_EPS)
        xh = xv * rstd
        diff = xh * gv - t_ref[...]
        loss_ref[...] += jnp.sum(diff * diff) * (0.5 / d)
        dy = diff * (1.0 / d)
        dg_ref[...] += jnp.sum(dy * xh, axis=0, keepdims=True)
        dxh = dy * gv
        dx = rstd * (dxh - xh * jnp.mean(dxh * xh, axis=-1, keepdims=True))
        dx_ref[...] = dx
        dxb_ref[...] = dx.astype(BF16)

    row = pl.BlockSpec((bm, d), lambda i: (i, 0))
    vec = pl.BlockSpec((1, d), lambda i: (0, 0))
    return pl.pallas_call(
        body, name=name, grid=(r // bm,),
        in_specs=[row, vec, row],
        out_specs=[row, row, vec, pl.BlockSpec((1, LANES), lambda i: (0, 0))],
        out_shape=[jax.ShapeDtypeStruct((r, d), F32), jax.ShapeDtypeStruct((r, d), BF16),
                   jax.ShapeDtypeStruct((1, d), F32), jax.ShapeDtypeStruct((1, LANES), F32)],
        compiler_params=_params(dimension_semantics=("arbitrary",)),
    )(x, g, tgt)


def _shift_down(v, k, row):
    return jnp.where(row >= k, pltpu.roll(v, k, 0), 0.0)


def _shift_up(v, k, row, s):
    return jnp.where(row < s - k, pltpu.roll(v, s - k, 0), 0.0)


def _conv_fwd(p, w, *, name):
    s = p.shape[0]
    width = w.shape[1]
    nb = width // LANES

    def body(b_ref, c_ref, u_ref, w_ref, y_ref):
        cu = c_ref[...].astype(F32) * u_ref[...].astype(F32)
        row = lax.broadcasted_iota(jnp.int32, cu.shape, 0)
        wv = w_ref[...]
        conv = wv[2:3] * cu + wv[1:2] * _shift_down(cu, 1, row) + wv[0:1] * _shift_down(cu, 2, row)
        y_ref[...] = (b_ref[...].astype(F32) * conv).astype(y_ref.dtype)

    def col(o):
        return pl.BlockSpec((s, LANES), lambda j: (0, j + o * nb))

    return pl.pallas_call(
        body, name=name, grid=(nb,),
        in_specs=[col(0), col(1), col(2), pl.BlockSpec((CONV_TAPS, LANES), lambda j: (0, j))],
        out_specs=col(0),
        out_shape=jax.ShapeDtypeStruct((s, width), BF16),
        compiler_params=_params(dimension_semantics=("parallel",)),
    )(p, p, p, w)


def _conv_bwd(p, w, dy, *, name):
    s = p.shape[0]
    width = w.shape[1]
    nb = width // LANES

    def body(b_ref, c_ref, u_ref, w_ref, dy_ref, db_ref, dc_ref, du_ref, dw_ref):
        bv = b_ref[...].astype(F32)
        cv = c_ref[...].astype(F32)
        uv = u_ref[...].astype(F32)
        dyv = dy_ref[...].astype(F32)
        cu = cv * uv
        row = lax.broadcasted_iota(jnp.int32, cu.shape, 0)
        wv = w_ref[...]
        cu1 = _shift_down(cu, 1, row)
        cu2 = _shift_down(cu, 2, row)
        conv = wv[2:3] * cu + wv[1:2] * cu1 + wv[0:1] * cu2
        db_ref[...] = (dyv * conv).astype(db_ref.dtype)
        dconv = dyv * bv
        dcu = wv[2:3] * dconv + wv[1:2] * _shift_up(dconv, 1, row, s) + wv[0:1] * _shift_up(dconv, 2, row, s)
        dc_ref[...] = (dcu * uv).astype(dc_ref.dtype)
        du_ref[...] = (dcu * cv).astype(du_ref.dtype)
        dw_ref[0:1, :] = jnp.sum(dconv * cu2, axis=0, keepdims=True)
        dw_ref[1:2, :] = jnp.sum(dconv * cu1, axis=0, keepdims=True)
        dw_ref[2:3, :] = jnp.sum(dconv * cu, axis=0, keepdims=True)

    def col(o):
        return pl.BlockSpec((s, LANES), lambda j: (0, j + o * nb))

    wspec = pl.BlockSpec((CONV_TAPS, LANES), lambda j: (0, j))
    act = jax.ShapeDtypeStruct((s, width), BF16)
    return pl.pallas_call(
        body, name=name, grid=(nb,),
        in_specs=[col(0), col(1), col(2), wspec, col(0)],
        out_specs=[col(0), col(0), col(0), wspec],
        out_shape=[act, act, act, jax.ShapeDtypeStruct((CONV_TAPS, width), F32)],
        compiler_params=_params(dimension_semantics=("parallel",)),
    )(p, p, p, w, dy)


FFN_STRIP = 128


def _ffn_in(h2, wg, wu, *, name):
    s, d = h2.shape
    nc, _, f = wg.shape
    bm = _blk(s, 1024)

    def body(h_ref, wg_ref, wu_ref, g_ref, u_ref, a_ref, g_scr, u_scr):
        hv = h_ref[...]
        g_scr[...] = jnp.dot(hv, wg_ref[...], preferred_element_type=F32)
        u_scr[...] = jnp.dot(hv, wu_ref[...], preferred_element_type=F32)
        for r in range(0, bm, FFN_STRIP):
            rows = pl.ds(r, min(FFN_STRIP, bm))
            gv, uv = g_scr[rows, :], u_scr[rows, :]
            g_ref[rows, :] = gv.astype(BF16)
            u_ref[rows, :] = uv.astype(BF16)
            a_ref[rows, :] = (gv * jax.nn.sigmoid(gv) * uv).astype(BF16)

    wspec = pl.BlockSpec((None, d, f), lambda i, j: (j, 0, 0))
    ospec = pl.BlockSpec((None, bm, f), lambda i, j: (j, i, 0))
    out = jax.ShapeDtypeStruct((nc, s, f), BF16)
    return pl.pallas_call(
        body, name=name, grid=(s // bm, nc),
        in_specs=[pl.BlockSpec((bm, d), lambda i, j: (i, 0)), wspec, wspec], out_specs=[ospec, ospec, ospec],
        out_shape=[out, out, out], scratch_shapes=[pltpu.VMEM((bm, f), F32), pltpu.VMEM((bm, f), F32)],
        compiler_params=_params(dimension_semantics=("parallel", "parallel")),
    )(h2, wg, wu)


def _ffn_dact(dx, wd, gate, up, *, name):
    s, d = dx.shape
    nc, f, _ = wd.shape
    bm = _blk(s, 1024)

    def body(dx_ref, wd_ref, g_ref, u_ref, dg_ref, du_ref, d_scr):
        d_scr[...] = lax.dot_general(dx_ref[...].astype(BF16), wd_ref[...], _NT, preferred_element_type=F32)
        for r in range(0, bm, FFN_STRIP):
            rows = pl.ds(r, min(FFN_STRIP, bm))
            gv = g_ref[rows, :].astype(F32)
            dv = d_scr[rows, :]
            sg = jax.nn.sigmoid(gv)
            dg_ref[rows, :] = (dv * u_ref[rows, :].astype(F32) * (sg * (1.0 + gv * (1.0 - sg)))).astype(BF16)
            du_ref[rows, :] = (dv * (gv * sg)).astype(BF16)

    cspec = pl.BlockSpec((None, bm, f), lambda i, j: (j, i, 0))
    out = jax.ShapeDtypeStruct((nc, s, f), BF16)
    return pl.pallas_call(
        body, name=name, grid=(s // bm, nc),
        in_specs=[pl.BlockSpec((bm, d), lambda i, j: (i, 0)), pl.BlockSpec((None, f, d), lambda i, j: (j, 0, 0)),
                  cspec, cspec],
        out_specs=[cspec, cspec], out_shape=[out, out], scratch_shapes=[pltpu.VMEM((bm, f), F32)],
        compiler_params=_params(dimension_semantics=("parallel", "parallel")),
    )(dx, wd, gate, up)


def _ffn_out(act, wd, res, *, name):
    nc, s, f = act.shape
    d = wd.shape[2]
    bm = _blk(s, 512)

    def body(a_ref, w_ref, r_ref, o_ref):
        acc = r_ref[...]
        for c in range(nc):
            acc = acc + jnp.dot(a_ref[c], w_ref[c], preferred_element_type=F32)
        o_ref[...] = acc

    row = pl.BlockSpec((bm, d), lambda i: (i, 0))
    return pl.pallas_call(
        body, name=name, grid=(s // bm,),
        in_specs=[pl.BlockSpec((nc, bm, f), lambda i: (0, i, 0)),
                  pl.BlockSpec((nc, f, d), lambda i: (0, 0, 0), pipeline_mode=pl.Buffered(1)), row],
        out_specs=row, out_shape=jax.ShapeDtypeStruct((s, d), F32),
        compiler_params=_params(dimension_semantics=("parallel",)),
    )(act, wd, res)


def _proj_dh(dp, w, *, name):
    s, k = dp.shape
    nc, d, kc = w.shape
    assert k == nc * kc and kc % LANES == 0
    bm = _blk(s, 512)

    def body(p_ref, w_ref, o_ref):
        acc = None
        for c in range(nc):
            term = lax.dot_general(p_ref[:, c * kc:(c + 1) * kc].astype(BF16), w_ref[c], _NT,
                                   preferred_element_type=F32)
            acc = term if acc is None else acc + term
        o_ref[...] = acc

    return pl.pallas_call(
        body, name=name, grid=(s // bm,),
        in_specs=[pl.BlockSpec((bm, k), lambda i: (i, 0)),
                  pl.BlockSpec((nc, d, kc), lambda i: (0, 0, 0), pipeline_mode=pl.Buffered(1))],
        out_specs=pl.BlockSpec((bm, d), lambda i: (i, 0)), out_shape=jax.ShapeDtypeStruct((s, d), F32),
        compiler_params=_params(dimension_semantics=("parallel",)),
    )(dp, w)


def _ffn_dh(dgate, dup, wg, wu, *, name):
    nc, s, f = dgate.shape
    d = wg.shape[1]
    bm = _blk(s, 512)

    def body(dg_ref, du_ref, wg_ref, wu_ref, o_ref):
        acc = None
        for c in range(nc):
            for x_ref, w_ref in ((dg_ref, wg_ref), (du_ref, wu_ref)):
                term = lax.dot_general(x_ref[c], w_ref[c], _NT, preferred_element_type=F32)
                acc = term if acc is None else acc + term
        o_ref[...] = acc

    xspec = pl.BlockSpec((nc, bm, f), lambda i: (0, i, 0))
    wspec = pl.BlockSpec((nc, d, f), lambda i: (0, 0, 0), pipeline_mode=pl.Buffered(1))
    return pl.pallas_call(
        body, name=name, grid=(s // bm,), in_specs=[xspec, xspec, wspec, wspec],
        out_specs=pl.BlockSpec((bm, d), lambda i: (i, 0)), out_shape=jax.ShapeDtypeStruct((s, d), F32),
        compiler_params=_params(dimension_semantics=("parallel",)),
    )(dgate, dup, wg, wu)


_NT = (((1,), (1,)), ((), ()))
_TN = (((0,), (0,)), ((), ()))


def _mem_probs(q, k, scale):
    s = lax.dot_general(q, k, _NT, preferred_element_type=F32) * scale
    e = jnp.exp(s - jnp.max(s, axis=-1, keepdims=True))
    return e / jnp.sum(e, axis=-1, keepdims=True)


def _mem_fwd(q, k, v, *, name):
    h, s, d = q.shape
    m = k.shape[1]
    bq = _blk(s, 1024)
    scale = 1.0 / math.sqrt(d)

    def body(q_ref, k_ref, v_ref, o_ref):
        p = _mem_probs(q_ref[...], k_ref[...], scale)
        o_ref[...] = jnp.dot(p.astype(BF16), v_ref[...], preferred_element_type=F32).astype(o_ref.dtype)

    qs = pl.BlockSpec((None, bq, d), lambda hh, i: (hh, i, 0))
    ks = pl.BlockSpec((None, m, d), lambda hh, i: (hh, 0, 0))
    return pl.pallas_call(
        body, name=name, grid=(h, s // bq), in_specs=[qs, ks, ks], out_specs=qs,
        out_shape=jax.ShapeDtypeStruct(q.shape, BF16),
        compiler_params=_params(dimension_semantics=("parallel", "parallel")),
    )(q, k, v)


def _mem_bwd(q, k, v, do, *, name):
    h, s, d = q.shape
    m = k.shape[1]
    bq = _blk(s, 1024)
    scale = 1.0 / math.sqrt(d)

    def body(q_ref, k_ref, v_ref, do_ref, dq_ref, dk_ref, dv_ref):
        @pl.when(pl.program_id(1) == 0)
        def _():
            dk_ref[...] = jnp.zeros_like(dk_ref)
            dv_ref[...] = jnp.zeros_like(dv_ref)

        qv = q_ref[...]
        kv = k_ref[...]
        dov = do_ref[...]
        p = _mem_probs(qv, kv, scale)
        pb = p.astype(BF16)
        dp = lax.dot_general(dov, v_ref[...], _NT, preferred_element_type=F32)
        pf = pb.astype(F32)
        ds = (pf * (dp - jnp.sum(pf * dp, axis=-1, keepdims=True)) * scale).astype(BF16)
        dq_ref[...] = jnp.dot(ds, kv, preferred_element_type=F32).astype(dq_ref.dtype)
        dk_ref[...] += lax.dot_general(ds, qv, _TN, preferred_element_type=F32)
        dv_ref[...] += lax.dot_general(pb, dov, _TN, preferred_element_type=F32)

    qs = pl.BlockSpec((None, bq, d), lambda hh, i: (hh, i, 0))
    ks = pl.BlockSpec((None, m, d), lambda hh, i: (hh, 0, 0))
    kvout = jax.ShapeDtypeStruct(k.shape, F32)
    return pl.pallas_call(
        body, name=name, grid=(h, s // bq), in_specs=[qs, ks, ks, qs], out_specs=[qs, ks, ks],
        out_shape=[jax.ShapeDtypeStruct(q.shape, BF16), kvout, kvout],
        compiler_params=_params(dimension_semantics=("parallel", "arbitrary")),
    )(q, k, v, do)


SB_TILE = 256
SB_STRIP = 32
SB_HEAD_GROUP = 4
SB_BWD_HEAD_GROUP = 4


def _sb_scale(d):
    scale = 1.0 / math.sqrt(d)
    assert math.frexp(scale)[0] == 0.5, "the scale is folded into bf16 q, exact only for a power of two"
    return scale


def _sb_strip_mask(r, t):
    rr = r + lax.broadcasted_iota(jnp.int32, (SB_STRIP, t), 0)
    return lax.broadcasted_iota(jnp.int32, (SB_STRIP, t), 1) < rr


def _neg_abs(z):
    bits = lax.bitcast_convert_type(z, jnp.uint32) | jnp.uint32(0x80000000)
    return lax.bitcast_convert_type(bits, F32)


def _store_split(split_scr, rows, val, t):
    hi = val.astype(BF16)
    split_scr[rows, 0:t] = hi
    split_scr[rows, t:2 * t] = (val - hi.astype(F32)).astype(BF16)


def _sb_logs_phase(z_scr, nsplit_scr, beta_scr, t, diag):
    for r in range(0, t, SB_STRIP):
        rows = pl.ds(r, SB_STRIP)
        z = z_scr[rows, :]
        e = jnp.exp(_neg_abs(z))
        nlog = jnp.maximum(z, 0.0) + jnp.log(1.0 + e)
        if beta_scr is not None:
            inv = pl.reciprocal(1.0 + e, approx=True)
            beta_scr[rows, :] = jnp.where(z >= 0.0, inv, e * inv)
        if diag:
            nlog = jnp.where(_sb_strip_mask(r, t), nlog, 0.0)
        _store_split(nsplit_scr, rows, nlog, t)


def _sb_probs(z_scr, tin_scr, rsum_scr, rows, r, t, diag):
    rs = rsum_scr[rows, :]
    a = jnp.exp(z_scr[rows, :] - tin_scr[rows, :] - rs)
    if diag:
        a = jnp.where(_sb_strip_mask(r, t), a, 0.0)
    rsum_scr[rows, :] = rs + tin_scr[rows, 0:1]
    return a


def _sb_triangle(tri_scr, t):
    row = lax.broadcasted_iota(jnp.int32, (t, t), 0)
    col = lax.broadcasted_iota(jnp.int32, (t, t), 1)
    tri = (row >= col).astype(BF16)
    tri_scr[0:t, :] = tri
    tri_scr[t:2 * t, :] = tri


def _sb_fwd(q, k, v, *, name):
    h, s, d = q.shape
    t = _blk(s, SB_TILE)
    scale = _sb_scale(d)

    def body(q_ref, k_ref, v_ref, o_ref, o32_ref, qs_scr, tri_scr, z_scr, nsplit_scr, tin_scr, a_scr, rsum_scr):
        qi = pl.program_id(1)
        qs_scr[...] = q_ref[...] * scale
        _sb_triangle(tri_scr, t)
        rsum_scr[...] = jnp.zeros_like(rsum_scr)
        o32_ref[...] = jnp.zeros_like(o32_ref)

        def tile(kb, diag):
            keys = pl.ds(pl.multiple_of(kb * t, t), t)
            for g in range(grp):
                z_scr[g] = lax.dot_general(qs_scr[g], k_ref[g, keys, :], _NT, preferred_element_type=F32)
            for g in range(grp):
                _sb_logs_phase(z_scr.at[g], nsplit_scr.at[g], None, t, diag)
                tin_scr[g] = jnp.dot(nsplit_scr[g], tri_scr[...], preferred_element_type=F32)
            for g in range(grp):
                for r in range(0, t, SB_STRIP):
                    rows = pl.ds(r, SB_STRIP)
                    a = _sb_probs(z_scr.at[g], tin_scr.at[g], rsum_scr.at[g], rows, r, t, diag)
                    a_scr[g, rows, :] = a.astype(BF16)
                o32_ref[g] += jnp.dot(a_scr[g], v_ref[g, keys, :], preferred_element_type=F32)

        tile(qi, True)

        def walk(n, carry):
            tile(qi - 1 - n, False)
            return carry

        lax.fori_loop(0, qi, walk, 0)
        o_ref[...] = o32_ref[...].astype(o_ref.dtype)

    grp = SB_HEAD_GROUP
    assert h % grp == 0
    qs = pl.BlockSpec((grp, t, d), lambda hh, i: (hh, i, 0))
    ks = pl.BlockSpec((grp, s, d), lambda hh, i: (hh, 0, 0))
    tile_f32 = pltpu.VMEM((grp, t, t), F32)
    col_f32 = pltpu.VMEM((grp, t, 1), F32)
    return pl.pallas_call(
        body, name=name, grid=(h // grp, s // t), in_specs=[qs, ks, ks], out_specs=[qs, qs],
        out_shape=[jax.ShapeDtypeStruct(q.shape, BF16), jax.ShapeDtypeStruct(q.shape, F32)],
        scratch_shapes=[pltpu.VMEM((grp, t, d), BF16), pltpu.VMEM((2 * t, t), BF16), tile_f32,
                        pltpu.VMEM((grp, t, 2 * t), BF16), tile_f32, pltpu.VMEM((grp, t, t), BF16), col_f32],
        compiler_params=_params(dimension_semantics=("parallel", "parallel")),
    )(q, k, v)


def _sb_bwd(q, k, v, o32, do, dk0, dv0, *, name):
    h, s, d = q.shape
    t = _blk(s, SB_TILE)
    scale = _sb_scale(d)
    has_init = dk0 is not None
    n_in = 7 if has_init else 5

    def body(*refs):
        if has_init:
            q_ref, k_ref, v_ref, o_ref, do_ref, dk0_ref, dv0_ref, dq_ref, dk_ref, dv_ref = refs[:n_in + 3]
        else:
            q_ref, k_ref, v_ref, o_ref, do_ref, dq_ref, dk_ref, dv_ref = refs[:n_in + 3]
        (qs_scr, tri_scr, z_scr, beta_scr, nsplit_scr, tin_scr, da_scr, a_scr, dz_scr,
         dq_scr, rsum_scr, gsum_scr, dsum_scr) = refs[n_in + 3:]
        g_scr, gsplit_scr, gin_scr = da_scr, nsplit_scr, tin_scr
        qi = pl.program_id(1)

        @pl.when(qi == 0)
        def _():
            if has_init:
                dk_ref[...] = dk0_ref[...]
                dv_ref[...] = dv0_ref[...]
            else:
                dk_ref[...] = jnp.zeros_like(dk_ref)
                dv_ref[...] = jnp.zeros_like(dv_ref)

        qs_scr[...] = q_ref[...] * scale
        _sb_triangle(tri_scr, t)
        dsum_scr[...] = jnp.sum(o_ref[...] * do_ref[...].astype(F32), axis=2, keepdims=True)
        rsum_scr[...] = jnp.zeros_like(rsum_scr)
        gsum_scr[...] = jnp.zeros_like(gsum_scr)
        dq_scr[...] = jnp.zeros_like(dq_scr)

        def tile(kb, diag):
            keys = pl.ds(pl.multiple_of(kb * t, t), t)
            for g in range(grp):
                z_scr[g] = lax.dot_general(qs_scr[g], k_ref[g, keys, :], _NT, preferred_element_type=F32)
                da_scr[g] = lax.dot_general(do_ref[g], v_ref[g, keys, :], _NT, preferred_element_type=F32)
            for g in range(grp):
                _sb_logs_phase(z_scr.at[g], nsplit_scr.at[g], beta_scr.at[g], t, diag)
                tin_scr[g] = jnp.dot(nsplit_scr[g], tri_scr[...], preferred_element_type=F32)
            for g in range(grp):
                for r in range(0, t, SB_STRIP):
                    rows = pl.ds(r, SB_STRIP)
                    ab = _sb_probs(z_scr.at[g], tin_scr.at[g], rsum_scr.at[g], rows, r, t, diag).astype(BF16)
                    a_scr[g, rows, :] = ab
                    gv = ab.astype(F32) * da_scr[g, rows, :]
                    g_scr[g, rows, :] = gv
                    _store_split(gsplit_scr.at[g], rows, gv, t)
                gin_scr[g] = jnp.dot(gsplit_scr[g], tri_scr[...], preferred_element_type=F32)
                dv_ref[g, keys, :] += lax.dot_general(a_scr[g], do_ref[g], _TN, preferred_element_type=F32)
            for g in range(grp):
                for r in range(0, t, SB_STRIP):
                    rows = pl.ds(r, SB_STRIP)
                    gs = gsum_scr[g, rows, :]
                    gv = g_scr[g, rows, :]
                    dz = gv - beta_scr[g, rows, :] * ((gv - gin_scr[g, rows, :]) + (dsum_scr[g, rows, :] - gs))
                    if diag:
                        dz = jnp.where(_sb_strip_mask(r, t), dz, 0.0)
                    dz_scr[g, rows, :] = dz.astype(BF16)
                    gsum_scr[g, rows, :] = gs + gin_scr[g, rows, 0:1]
                dq_scr[g] += jnp.dot(dz_scr[g], k_ref[g, keys, :], preferred_element_type=F32)
                dk_ref[g, keys, :] += lax.dot_general(dz_scr[g], qs_scr[g], _TN, preferred_element_type=F32)

        tile(qi, True)

        def walk(n, carry):
            tile(qi - 1 - n, False)
            return carry

        lax.fori_loop(0, qi, walk, 0)
        dq_ref[...] = (dq_scr[...] * scale).astype(dq_ref.dtype)

    grp = SB_BWD_HEAD_GROUP
    assert h % grp == 0
    qs = pl.BlockSpec((grp, t, d), lambda hh, i: (hh, i, 0))
    ks = pl.BlockSpec((grp, s, d), lambda hh, i: (hh, 0, 0), pipeline_mode=pl.Buffered(1))
    in_specs = [qs, ks, ks, qs, qs] + ([ks, ks] if has_init else [])
    operands = [q, k, v, o32, do] + ([dk0, dv0] if has_init else [])
    acc = jax.ShapeDtypeStruct(q.shape, F32)
    tile_f32 = pltpu.VMEM((grp, t, t), F32)
    tile_bf16 = pltpu.VMEM((grp, t, t), BF16)
    split = pltpu.VMEM((grp, t, 2 * t), BF16)
    col_f32 = pltpu.VMEM((grp, t, 1), F32)
    return pl.pallas_call(
        body, name=name, grid=(h // grp, s // t), in_specs=in_specs, out_specs=[qs, ks, ks],
        out_shape=[jax.ShapeDtypeStruct(q.shape, BF16), acc, acc],
        scratch_shapes=[pltpu.VMEM((grp, t, d), BF16), pltpu.VMEM((2 * t, t), BF16), tile_f32, tile_f32, split,
                        tile_f32, tile_f32, tile_bf16, tile_bf16,
                        pltpu.VMEM((grp, t, d), F32), col_f32, col_f32, col_f32],
        compiler_params=_params(dimension_semantics=("parallel", "arbitrary")),
    )(*operands)


def _position():
    x, y, c = lax.axis_index("x"), lax.axis_index("y"), lax.axis_index("c")
    return x, y, c, [(1 - x, y), (x, 1 - y), (1 - x, 1 - y)]


_ANY = pl.BlockSpec(memory_space=pl.ANY)
N_PEER_CHIPS = N_CHIPS - 1


def _all_gather_chips(shards, *, name):
    n = len(shards)

    def body(*refs):
        ins, outs = refs[:n], refs[n:2 * n]
        send_sems, recv_sems, local_sems = refs[2 * n:]
        x, y, c, peers = _position()
        me = 2 * x + y
        copies = []
        for a in range(n):
            copies.append(pltpu.make_async_copy(ins[a], outs[a].at[me], local_sems.at[a]))
            for j, (px, py) in enumerate(peers):
                copies.append(pltpu.make_async_remote_copy(
                    src_ref=ins[a], dst_ref=outs[a].at[me],
                    send_sem=send_sems.at[a * N_PEER_CHIPS + j], recv_sem=recv_sems.at[a * N_PEER_CHIPS + j],
                    device_id=(px, py, c), device_id_type=MESH))
        for cp in copies:
            cp.start()
        for cp in copies:
            cp.wait()

    return pl.pallas_call(
        body, name=name, in_specs=[_ANY] * n, out_specs=[_ANY] * n,
        out_shape=[jax.ShapeDtypeStruct((N_CHIPS, *s.shape), s.dtype) for s in shards],
        scratch_shapes=[pltpu.SemaphoreType.DMA((n * N_PEER_CHIPS,)), pltpu.SemaphoreType.DMA((n * N_PEER_CHIPS,)),
                        pltpu.SemaphoreType.DMA((n,))],
        compiler_params=pltpu.CompilerParams(has_side_effects=True),
    )(*shards)


_HBM = pl.BlockSpec(memory_space=pltpu.HBM)
_SEM = pl.BlockSpec(memory_space=pltpu.SEMAPHORE)
_DATAFLOW = pltpu.SideEffectType.DATAFLOW_SIDE_EFFECTING
_TOKEN = jax.ShapeDtypeStruct((8, LANES), F32)


def _in_hbm(arr):
    return pltpu.with_memory_space_constraint(arr, pltpu.HBM)


def _gather_start(packs, lands, after, *, name):
    n = len(packs)

    def body(*refs):
        src, land = refs[:n], refs[n:2 * n]
        send_sems, recv_sems = refs[2 * n + 1:2 * n + 3]
        token = refs[-1]
        x, y, c, peers = _position()
        for i in range(n):
            for j, (px, py) in enumerate(peers):
                pltpu.make_async_remote_copy(
                    src_ref=src[i], dst_ref=land[i].at[2 * x + y],
                    send_sem=send_sems.at[N_PEER_CHIPS * i + j], recv_sem=recv_sems.at[N_PEER_CHIPS * i + j],
                    device_id=(px, py, c), device_id_type=MESH).start()
        token[...] = jnp.zeros_like(token)

    thru = [pltpu.HBM(a.shape, a.dtype) for a in (*packs, *lands)]
    outs = pl.pallas_call(
        body, name=name,
        out_shape=(pltpu.SemaphoreType.DMA((N_PEER_CHIPS * n,)), pltpu.SemaphoreType.DMA((N_PEER_CHIPS * n,)), *thru,
                   _TOKEN),
        in_specs=[_HBM] * (2 * n) + [_ANY],
        out_specs=(_SEM, _SEM, *[_HBM] * (2 * n), pl.BlockSpec(memory_space=pltpu.VMEM)),
        input_output_aliases={k: 2 + k for k in range(2 * n)},
        compiler_params=pltpu.CompilerParams(has_side_effects=_DATAFLOW),
    )(*[_in_hbm(a) for a in (*packs, *lands)], after)
    return outs[0], outs[1], list(outs[2:2 + n]), list(outs[2 + n:2 + 2 * n]), outs[-1]


def _gather_wait(first, packs, lands, send_sems, recv_sems, after, *, name):
    n = len(packs)

    def body(*refs):
        src, land = refs[:n], refs[n:2 * n]
        send_sems, recv_sems = refs[2 * n:2 * n + 2]
        _, _, c, peers = _position()
        for a in range(n):
            for j, (px, py) in enumerate(peers):
                sem = N_PEER_CHIPS * (first + a) + j
                cp = pltpu.make_async_remote_copy(
                    src_ref=src[a], dst_ref=land[a].at[2 * px + py], send_sem=send_sems.at[sem],
                    recv_sem=recv_sems.at[sem], device_id=(px, py, c), device_id_type=MESH)
                cp.wait_send()
                cp.wait_recv()

    outs = pl.pallas_call(
        body, name=name, out_shape=[pltpu.HBM(a.shape, a.dtype) for a in (*packs, *lands)],
        in_specs=[_HBM] * (2 * n) + [_SEM, _SEM, _ANY], out_specs=[_HBM] * (2 * n),
        input_output_aliases={k: k for k in range(2 * n)},
        compiler_params=pltpu.CompilerParams(has_side_effects=_DATAFLOW),
    )(*packs, *lands, send_sems, recv_sems, after)
    return list(outs[:n]), list(outs[n:])


def _scatter_start(gpacks, lands, *, name):
    n = len(gpacks)

    def body(*refs):
        src, land = refs[:n], refs[n:2 * n]
        send_sems, recv_sems = refs[2 * n:2 * n + 2]
        token = refs[-1]
        _, _, c, peers = _position()
        for a in range(n):
            for j, (px, py) in enumerate(peers):
                pltpu.make_async_remote_copy(
                    src_ref=src[a].at[2 * px + py], dst_ref=land[a].at[j], send_sem=send_sems.at[N_PEER_CHIPS * a + j],
                    recv_sem=recv_sems.at[N_PEER_CHIPS * a + j], device_id=(px, py, c), device_id_type=MESH).start()
        token[...] = jnp.zeros_like(token)

    thru = [pltpu.HBM(a.shape, a.dtype) for a in (*gpacks, *lands)]
    outs = pl.pallas_call(
        body, name=name,
        out_shape=(pltpu.SemaphoreType.DMA((N_PEER_CHIPS * n,)), pltpu.SemaphoreType.DMA((N_PEER_CHIPS * n,)), *thru,
                   _TOKEN),
        in_specs=[_HBM] * (2 * n), out_specs=(_SEM, _SEM, *[_HBM] * (2 * n), pl.BlockSpec(memory_space=pltpu.VMEM)),
        input_output_aliases={k: 2 + k for k in range(2 * n)},
        compiler_params=pltpu.CompilerParams(has_side_effects=_DATAFLOW),
    )(*[_in_hbm(a) for a in (*gpacks, *lands)])
    return outs[0], outs[1], list(outs[2:2 + n]), list(outs[2 + n:2 + 2 * n]), outs[-1]


def _scatter_wait(gpacks, lands, send_sems, recv_sems, after, *, name):
    n = len(gpacks)

    def body(*refs):
        src, land = refs[:n], refs[n:2 * n]
        send_sems, recv_sems = refs[2 * n:2 * n + 2]
        _, _, c, peers = _position()
        for a in range(n):
            for j, (px, py) in enumerate(peers):
                cp = pltpu.make_async_remote_copy(
                    src_ref=src[a].at[2 * px + py], dst_ref=land[a].at[j], send_sem=send_sems.at[N_PEER_CHIPS * a + j],
                    recv_sem=recv_sems.at[N_PEER_CHIPS * a + j], device_id=(px, py, c),
                    device_id_type=MESH)
                cp.wait_send()
                cp.wait_recv()

    outs = pl.pallas_call(
        body, name=name, out_shape=[pltpu.HBM(a.shape, a.dtype) for a in (*gpacks, *lands)],
        in_specs=[_HBM] * (2 * n) + [_SEM, _SEM, _ANY], out_specs=[_HBM] * (2 * n),
        input_output_aliases={k: k for k in range(2 * n)},
        compiler_params=pltpu.CompilerParams(has_side_effects=_DATAFLOW),
    )(*gpacks, *lands, send_sems, recv_sems, after)
    return list(outs[:n]), list(outs[n:])


def _sibling_exchange(arrs, *, name):
    n = len(arrs)

    def body(*refs):
        ins, outs = refs[:n], refs[n:2 * n]
        send_sems, recv_sems = refs[2 * n:]
        x, y, c, _ = _position()
        copies = [pltpu.make_async_remote_copy(
            src_ref=ins[a], dst_ref=outs[a], send_sem=send_sems.at[a], recv_sem=recv_sems.at[a],
            device_id=(x, y, 1 - c), device_id_type=MESH) for a in range(n)]
        for cp in copies:
            cp.start()
        for cp in copies:
            cp.wait()

    return pl.pallas_call(
        body, name=name, in_specs=[_ANY] * n, out_specs=[_ANY] * n,
        out_shape=[jax.ShapeDtypeStruct(a.shape, a.dtype) for a in arrs],
        scratch_shapes=[pltpu.SemaphoreType.DMA((n,)), pltpu.SemaphoreType.DMA((n,))],
        compiler_params=pltpu.CompilerParams(has_side_effects=True),
    )(*arrs)


def _small_all_reduce(v, *, name):
    r, cdim = v.shape

    def body(v_ref, o_ref, slots, send_sems, recv_sems):
        x, y, c, _ = _position()
        me = 4 * x + 2 * y + c
        slots[me] = v_ref[...]
        copies = []
        for j in range(1, N_DEVICES):
            peer = (x ^ ((j >> 2) & 1), y ^ ((j >> 1) & 1), c ^ (j & 1))
            copies.append(pltpu.make_async_remote_copy(
                src_ref=v_ref, dst_ref=slots.at[me], send_sem=send_sems.at[j - 1], recv_sem=recv_sems.at[j - 1],
                device_id=peer, device_id_type=MESH))
        for cp in copies:
            cp.start()
        for cp in copies:
            cp.wait()
        acc = slots[0]
        for dev in range(1, N_DEVICES):
            acc = acc + slots[dev]
        o_ref[...] = acc

    vm = pl.BlockSpec(memory_space=pltpu.VMEM)
    return pl.pallas_call(
        body, name=name, in_specs=[vm], out_specs=vm,
        out_shape=jax.ShapeDtypeStruct(v.shape, F32),
        scratch_shapes=[pltpu.VMEM((N_DEVICES, r, cdim), F32),
                        pltpu.SemaphoreType.DMA((N_DEVICES - 1,)), pltpu.SemaphoreType.DMA((N_DEVICES - 1,))],
        compiler_params=pltpu.CompilerParams(has_side_effects=True),
    )(v)


def _row_block(r, cap, mult):
    best = max(b for b in range(mult, cap + 1, mult) if r % b == 0)
    return best


def _sum_partials(chip_idx, own4, landed, *, name):
    _, r, c = landed.shape
    bm = _row_block(r, 512, 16)

    def body(chip_ref, own_ref, land_ref, o_ref):
        acc = own_ref[...].astype(F32)
        for j in range(N_PEER_CHIPS):
            acc = acc + land_ref[j].astype(F32)
        o_ref[...] = acc

    return pl.pallas_call(
        body, name=name,
        grid_spec=pltpu.PrefetchScalarGridSpec(
            num_scalar_prefetch=1, grid=(r // bm,),
            in_specs=[pl.BlockSpec((None, bm, c), lambda i, chip: (chip[0], i, 0)),
                      pl.BlockSpec((N_PEER_CHIPS, bm, c), lambda i, chip: (0, i, 0))],
            out_specs=pl.BlockSpec((bm, c), lambda i, chip: (i, 0))),
        out_shape=jax.ShapeDtypeStruct((r, c), F32),
        compiler_params=_params(dimension_semantics=("parallel",)),
    )(chip_idx, own4, landed)


def _adamw(w, m, v, g_a, g_b, *, name):
    r, c = w.shape
    bm = _blk(r, 256)
    two = g_b is not None
    bc1 = 1.0 - ADAM_B1 ** ADAM_STEP
    bc2 = 1.0 - ADAM_B2 ** ADAM_STEP

    def body(*refs):
        if two:
            w_ref, m_ref, v_ref, ga_ref, gb_ref, g_ref, d_ref, nm_ref, nv_ref = refs
            g = ga_ref[...] + gb_ref[...]
        else:
            w_ref, m_ref, v_ref, ga_ref, g_ref, d_ref, nm_ref, nv_ref = refs
            g = ga_ref[...]
        nm = ADAM_B1 * m_ref[...] + (1.0 - ADAM_B1) * g
        nv = ADAM_B2 * v_ref[...] + (1.0 - ADAM_B2) * (g * g)
        g_ref[...] = g
        nm_ref[...] = nm
        nv_ref[...] = nv
        d_ref[...] = -ADAM_LR * ((nm / bc1) / (jnp.sqrt(nv / bc2) + ADAM_EPS) + ADAM_WD * w_ref[...])

    spec = pl.BlockSpec((bm, c), lambda i: (i, 0))
    out = jax.ShapeDtypeStruct((r, c), F32)
    operands = [w, m, v, g_a] + ([g_b] if two else [])
    return pl.pallas_call(
        body, name=name, grid=(r // bm,),
        in_specs=[spec] * len(operands), out_specs=[spec] * 4, out_shape=[out] * 4,
        compiler_params=_params(dimension_semantics=("parallel",)),
    )(*operands)


def _heads(a):
    s, w = a.shape
    return a.reshape(s, w // HEAD_DIM, HEAD_DIM).transpose(1, 0, 2)


def _unheads(a):
    h, s, d = a.shape
    return a.transpose(1, 0, 2).reshape(s, h * d)


def _width_groups(shapes):
    groups = {}
    for idx, (_, c) in enumerate(shapes):
        groups.setdefault(c, []).append(idx)
    return list(groups.values())


def _pack(arrs, lead):
    groups = _width_groups([a.shape[-2:] for a in arrs])
    return [jnp.concatenate([arrs[k] for k in grp], axis=lead) for grp in groups]


def _unpack(bufs, shapes, lead):
    outs = [None] * len(shapes)
    for buf, grp in zip(bufs, _width_groups(shapes)):
        off = 0
        for k in grp:
            outs[k] = lax.slice_in_dim(buf, off, off + shapes[k][0], axis=lead)
            off += shapes[k][0]
    return outs


class LayerWeights:
    FIELDS = ("w_in", "wm", "wo", "wg", "wu", "wd", "wkv")
    STAGE = (0, 0, 1, 2, 2, 2, 2)

    def __init__(self):
        for f in self.FIELDS:
            setattr(self, f, None)


def _pack_small(mix, ffn, kvn, memn, fin, conv):
    d = mix.shape[-1]
    flat = conv.reshape(-1)
    rows_conv = SMALL_ROWS - 11
    flat = jnp.pad(flat, (0, rows_conv * d - flat.shape[0]))
    return jnp.concatenate([mix, ffn, kvn.reshape(1, d), memn.reshape(1, d), fin.reshape(1, d),
                            flat.reshape(rows_conv, d)], axis=0)


def _unpack_small(buf, conv_shape):
    n = math.prod(conv_shape)
    return (buf[0:4], buf[4:8], buf[8], buf[9], buf[10], buf[11:].reshape(-1)[:n].reshape(conv_shape))


def kernel(x, mem, mix_norm, a_in, conv_w, b_in, kv_norm, w_kv_shared, w_mem_kv, w_o, ffn_norm, w_gate, w_up, w_down, mem_norm, final_norm, loss_target, m_mix_norm, m_a_in, m_conv_w, m_b_in, m_kv_norm, m_w_kv_shared, m_w_mem_kv, m_w_o, m_ffn_norm, m_w_gate, m_w_up, m_w_down, m_mem_norm, m_final_norm, v_mix_norm, v_a_in, v_conv_w, v_b_in, v_kv_norm, v_w_kv_shared, v_w_mem_kv, v_w_o, v_ffn_norm, v_w_gate, v_w_up, v_w_down, v_mem_norm, v_final_norm):
    s, d = x.shape[1], x.shape[2]
    n_mem = mem.shape[1]
    depth = mix_norm.shape[0]
    n_a = a_in.shape[0]
    main_w = conv_w.shape[2] * N_CHIPS
    mem_w = w_mem_kv.shape[2] // 2
    ffn_c = w_gate.shape[2]
    kv_c = w_kv_shared.shape[1]
    a_c = a_in.shape[2]
    chip = 2 * lax.axis_index("x") + lax.axis_index("y")

    x0 = x[0]
    mem0 = mem[0]
    tgt = loss_target[0]
    bs = _blk(s, 1024)
    bg = _blk(s, 4096)

    def layer_shards(i, a_or_b, others, kv):
        ws = [a_or_b[0][i] if i < n_a else a_or_b[1][i - n_a]] + [w[i] for w in others]
        return ws + ([kv] if i == n_a - 1 else [])

    shards = [layer_shards(i, (a_in, b_in), (w_mem_kv, w_o, w_gate, w_up, w_down), w_kv_shared) for i in range(depth)]
    shapes = [[w.shape for w in ws] for ws in shards]
    layer_packs = [_pack([w.astype(BF16) for w in ws], 0) for ws in shards]
    first = [sum(len(p) for p in layer_packs[:i]) for i in range(depth + 1)]
    packs = [p for ps in layer_packs for p in ps]
    conv_parts, = _all_gather_chips([conv_w], name="gather_conv_weights")
    conv_full = jnp.concatenate([conv_parts[kk] for kk in range(N_CHIPS)], axis=-1)
    lands = [lax.empty((N_CHIPS, *p.shape), BF16) for p in packs]
    cut = first[1]
    sems0 = _gather_start(packs[:cut], lands[:cut], conv_parts, name="gather_weights_start_l0")
    sems1 = _gather_start(packs[cut:], lands[cut:], sems0[4], name="gather_weights_start_rest")
    packs, lands, started = sems0[2] + sems1[2], sems0[3] + sems1[3], sems1[4]

    def gain(vec):
        return vec.reshape(1, d)

    lw = [LayerWeights() for _ in range(depth)]

    def fetch(i, stage, after):
        groups = _width_groups(shapes[i])
        sel = [gi for gi, grp in enumerate(groups) if min(LayerWeights.STAGE[k] for k in grp) == stage]
        if not sel:
            return
        lo, hi = first[i] + sel[0], first[i] + sel[-1] + 1
        assert hi - lo == len(sel)
        sems, base = (sems0, 0) if i == 0 else (sems1, cut)
        own, landed = _gather_wait(lo - base, packs[lo:hi], lands[lo:hi], sems[0], sems[1], after,
                                   name=f"gather_weights_wait_l{i}_s{stage}")
        for gi, mine, buf in zip(sel, own, landed):
            buf = lax.dynamic_update_slice_in_dim(buf, mine[None], chip, axis=0)
            off = 0
            for k in groups[gi]:
                rows = shapes[i][k][0]
                setattr(lw[i], LayerWeights.FIELDS[k], lax.slice_in_dim(buf, off, off + rows, axis=1))
                off += rows

    mem_n = _rms_fwd(mem0, gain(mem_norm), name="mem_norm_fwd", dep=started)
    saved = []
    k_sh = v_sh = hk = x_kv = None
    xc = x0
    for i in range(depth):
        st = {"x_in": xc}
        h = _rms_fwd(xc, gain(mix_norm[i]), name="mix_norm_fwd")
        fetch(i, 0, h)
        mkv = _mm(Op(mem_n), Op(lw[i].wm, 'r'), name="mem_kv_proj", bm=n_mem, bn=2 * mem_w, bk=d,
                  out_dtype=BF16)
        mem_k, mem_v = _heads(mkv[:, :mem_w]), _heads(mkv[:, mem_w:])
        if i < n_a:
            p = _mm(Op(h), Op(lw[i].w_in, 'c'), name="a_in_proj", bm=bs, bn=a_c, bk=d, out_dtype=BF16)
            y_main = _conv_fwd(p, conv_full[i], name="conv_fwd")
            q_mem = _heads(p[:, 3 * main_w:])
        else:
            p = _mm(Op(h), Op(lw[i].w_in, 'r'), name="b_in_proj", bm=bs, bn=d, bk=d, out_dtype=BF16)
            q_sb = _heads(p[:, :main_w])
            o_sb, o_sb32 = _sb_fwd(q_sb, k_sh, v_sh, name="sb_fwd")
            y_main = _unheads(o_sb)
            q_mem = _heads(p[:, main_w:])
            st.update(q_sb=q_sb, o_sb32=o_sb32)
        y_mem = _mem_fwd(q_mem, mem_k, mem_v, name="mem_attn_fwd")
        y = jnp.concatenate([y_main, _unheads(y_mem)], axis=-1)
        fetch(i, 1, y)
        x_mid = _mm(Op(y), Op(lw[i].wo, 'r'), name="w_o_proj", bm=bs, bn=d, bk=d, out_dtype=F32,
                    res=Op(xc))
        h2 = _rms_fwd(x_mid, gain(ffn_norm[i]), name="ffn_norm_fwd")
        fetch(i, 2, h2)
        gate, up, act = _ffn_in(h2, lw[i].wg, lw[i].wu, name="ffn_gate_up")
        xc = _ffn_out(act, lw[i].wd, x_mid, name="ffn_down")
        st.update(h=h, p=p, mem_k=mem_k, mem_v=mem_v, q_mem=q_mem, y=y, x_mid=x_mid, h2=h2, gate=gate, up=up,
                  act=act)
        saved.append(st)
        if i == n_a - 1:
            x_kv = xc
            hk = _rms_fwd(xc, gain(kv_norm), name="kv_norm_fwd")
            kv = _mm(Op(hk), Op(lw[n_a - 1].wkv, 'c'), name="kv_proj", bm=bs, bn=kv_c, bk=d, out_dtype=BF16)
            k_sh, v_sh = _heads(kv[:, :main_w]), _heads(kv[:, main_w:])

    dx, dxb, dg_final, loss_part = _final_loss(xc, gain(final_norm), tgt, name="final_norm_loss")
    loss = lax.psum(loss_part[0, 0], ("x", "y", "c"))

    g_a, g_b, g_m, g_o, g_g, g_u, g_d = ([None] * n_a, [None] * (depth - n_a), [None] * depth, [None] * depth,
                                         [None] * depth, [None] * depth, [None] * depth)
    dg_mix, dg_ffn, dconv = [None] * depth, [None] * depth, [None] * n_a
    in_flight = [[None, None] for _ in range(depth)]
    HALVES = ((0, 1, 2, 6), (3, 4, 5))

    def scatter(grads, name):
        gpacks = _pack(grads, 1)
        return _scatter_start(gpacks, [lax.empty((N_PEER_CHIPS, *g.shape[1:]), BF16) for g in gpacks], name=name)

    dk_sh = dv_sh = None
    dmem_n = None
    g_kv = dg_kv = None
    for i in reversed(range(depth)):
        st = saved[i]
        dgate, dup = _ffn_dact(dxb, lw[i].wd, st["gate"], st["up"], name="ffn_dact_gate")
        g_d[i] = _mm(Op(st["act"], 'c'), Op(dxb), name="w_down_grad", ta=True, bm=ffn_c, bn=d, bk=bg,
                     out_dtype=BF16, out_chunk='r')
        g_g[i] = _mm(Op(st["h2"]), Op(dgate, 'c'), name="w_gate_grad", ta=True, bm=d, bn=ffn_c, bk=bg,
                     out_dtype=BF16, out_chunk='c')
        g_u[i] = _mm(Op(st["h2"]), Op(dup, 'c'), name="w_up_grad", ta=True, bm=d, bn=ffn_c, bk=bg,
                     out_dtype=BF16, out_chunk='c')
        in_flight[i][1] = scatter([g_g[i], g_u[i], g_d[i]], f"scatter_ffn_grads_start_l{i}")
        dh2 = _ffn_dh(dgate, dup, lw[i].wg, lw[i].wu, name="ffn_dh")
        dx_mid, dx_mid_b, dg_ffn[i] = _rms_bwd(st["x_mid"], gain(ffn_norm[i]), dh2, dx, name="ffn_norm_bwd",
                                     dep=in_flight[i][1][4])
        dy = _mm(Op(dx_mid_b), Op(lw[i].wo, 'r'), name="w_o_dy", tb=True, bm=bs, bn=d, bk=d,
                 out_dtype=BF16)
        g_o[i] = _mm(Op(st["y"]), Op(dx_mid_b), name="w_o_grad", ta=True, bm=d // N_CHIPS, bn=d, bk=bg,
                     out_dtype=BF16, out_chunk='r')
        dq_mem, dmk, dmv = _mem_bwd(st["q_mem"], st["mem_k"], st["mem_v"], _heads(dy[:, main_w:]),
                                    name="mem_attn_bwd")
        dmkv = jnp.concatenate([_unheads(dmk), _unheads(dmv)], axis=-1)
        g_m[i] = _mm(Op(mem_n), Op(dmkv), name="mem_kv_grad", ta=True, bm=d // N_CHIPS, bn=2 * mem_w, bk=n_mem,
                     out_dtype=BF16, out_chunk='r')
        dmem_n = _mm(Op(dmkv), Op(lw[i].wm, 'r'), name="mem_kv_dmem", tb=True, bm=n_mem, bn=d,
                     bk=2 * mem_w, out_dtype=F32, res=None if dmem_n is None else Op(dmem_n))
        if i < n_a:
            db, dc, du, dconv[i] = _conv_bwd(st["p"], conv_full[i], dy[:, :main_w], name="conv_bwd")
            dp = jnp.concatenate([db, dc, du, _unheads(dq_mem)], axis=-1)
            g_a[i] = _mm(Op(st["h"]), Op(dp), name="a_in_grad", ta=True, bm=d, bn=a_c, bk=bg, out_dtype=BF16,
                         out_chunk='c')
            dh = _proj_dh(dp, lw[i].w_in, name="a_in_dh")
        else:
            dq_sb, dk_sh, dv_sh = _sb_bwd(st["q_sb"], k_sh, v_sh, st["o_sb32"], _heads(dy[:, :main_w]),
                                          dk_sh, dv_sh, name="sb_bwd")
            dp = jnp.concatenate([_unheads(dq_sb), _unheads(dq_mem)], axis=-1)
            g_b[i - n_a] = _mm(Op(st["h"]), Op(dp), name="b_in_grad", ta=True, bm=d // N_CHIPS, bn=d, bk=bg,
                               out_dtype=BF16, out_chunk='r')
            dh = _mm(Op(dp), Op(lw[i].w_in, 'r'), name="b_in_dh", tb=True, bm=bs, bn=d, bk=d,
                     out_dtype=F32)
        in_flight[i][0] = scatter([g_a[i] if i < n_a else g_b[i - n_a], g_m[i], g_o[i]]
                                  + ([g_kv] if i == n_a - 1 else []), f"scatter_mixer_grads_start_l{i}")
        dx, dxb, dg_mix[i] = _rms_bwd(st["x_in"], gain(mix_norm[i]), dh, dx_mid, name="mix_norm_bwd",
                                 dep=in_flight[i][0][4])
        if i == n_a:
            dkv = jnp.concatenate([_unheads(dk_sh), _unheads(dv_sh)], axis=-1)
            g_kv = _mm(Op(hk), Op(dkv), name="kv_grad", ta=True, bm=d, bn=kv_c, bk=bg, out_dtype=BF16,
                       out_chunk='c')
            dhk = _proj_dh(dkv, lw[n_a - 1].wkv, name="kv_dh")
            dx, dxb, dg_kv = _rms_bwd(x_kv, gain(kv_norm), dhk, dx, name="kv_norm_bwd")
    _, _, dg_mem = _rms_bwd(mem0, gain(mem_norm), dmem_n, None, name="mem_norm_bwd")

    chip_idx = chip.astype(jnp.int32).reshape(1)
    core_sums, spans = [], {}
    for i in reversed(range(depth)):
        for half in (1, 0):
            ssem, rsem, gthru, lthru, _ = in_flight[i][half]
            gthru, lthru = _scatter_wait(gthru, lthru, ssem, rsem, dx, name=f"scatter_grads_wait_l{i}_h{half}")
            spans[i, half] = (len(core_sums), len(core_sums) + len(gthru))
            core_sums += [_sum_partials(chip_idx, g, l, name="sum_chip_partials") for g, l in zip(gthru, lthru)]
    sibling_sums = _sibling_exchange(core_sums, name="exchange_core_sums")

    def layer_parts(sums, i):
        parts = [None] * len(shapes[i])
        for half in (0, 1):
            pos = [k for k in HALVES[half] if k < len(shapes[i])]
            lo, hi = spans[i, half]
            for k, part in zip(pos, _unpack(sums[lo:hi], [shapes[i][k] for k in pos], 0)):
                parts[k] = part
        return parts

    own_parts = [layer_parts(core_sums, i) for i in range(depth)]
    sib_parts = [layer_parts(sibling_sums, i) for i in range(depth)]

    def stacked(parts, pos, layers):
        return jnp.concatenate([parts[i][pos] for i in layers], axis=0)

    a_layers, b_layers, all_layers = range(n_a), range(n_a, depth), range(depth)
    big = [("a_in", a_in, m_a_in, v_a_in, 0, a_layers), ("b_in", b_in, m_b_in, v_b_in, 0, b_layers),
           ("w_kv_shared", w_kv_shared, m_w_kv_shared, v_w_kv_shared, 6, [n_a - 1]),
           ("w_mem_kv", w_mem_kv, m_w_mem_kv, v_w_mem_kv, 1, all_layers), ("w_o", w_o, m_w_o, v_w_o, 2, all_layers),
           ("w_gate", w_gate, m_w_gate, v_w_gate, 3, all_layers), ("w_up", w_up, m_w_up, v_w_up, 4, all_layers),
           ("w_down", w_down, m_w_down, v_w_down, 5, all_layers)]
    results = {}
    for wname, w, mm_, vv_, pos, layers in big:
        flat = lambda t: t.reshape(-1, t.shape[-1])
        outs = _adamw(flat(w), flat(mm_), flat(vv_), stacked(own_parts, pos, layers), stacked(sib_parts, pos, layers),
                      name="adamw")
        results[wname] = [o.reshape(w.shape) for o in outs]

    small_g = _pack_small(jnp.concatenate(dg_mix, axis=0), jnp.concatenate(dg_ffn, axis=0), dg_kv, dg_mem,
                          dg_final, jnp.stack(dconv, axis=0))
    small_g = _small_all_reduce(small_g, name="all_reduce_small_grads")
    conv_shape_full = (n_a, CONV_TAPS, main_w)
    gs = list(_unpack_small(small_g, conv_shape_full))
    gs[5] = lax.dynamic_slice_in_dim(gs[5], chip * conv_w.shape[2], conv_w.shape[2], axis=2)
    small_outs = _adamw(_pack_small(mix_norm, ffn_norm, kv_norm, mem_norm, final_norm, conv_w),
                        _pack_small(m_mix_norm, m_ffn_norm, m_kv_norm, m_mem_norm, m_final_norm, m_conv_w),
                        _pack_small(v_mix_norm, v_ffn_norm, v_kv_norm, v_mem_norm, v_final_norm, v_conv_w),
                        _pack_small(*gs), None, name="adamw_small")
    small_names = ["mix_norm", "ffn_norm", "kv_norm", "mem_norm", "final_norm", "conv_w"]
    for kind, buf in enumerate(small_outs):
        for wname, val in zip(small_names, _unpack_small(buf, conv_w.shape)):
            results.setdefault(wname, [None] * 4)[kind] = val

    order = ["mix_norm", "a_in", "conv_w", "b_in", "kv_norm", "w_kv_shared", "w_mem_kv", "w_o", "ffn_norm",
             "w_gate", "w_up", "w_down", "mem_norm", "final_norm"]
    return (loss, dx[None], *[results[nm][0] for nm in order], *[results[nm][1] for nm in order],
            *[results[nm][2] for nm in order], *[results[nm][3] for nm in order])
```
